```python
import math
import jax, jax.numpy as jnp
from jax import lax
import numpy as np

D_MODEL = 1024
BATCH = 8
SEQ = 4096
DEPTH = 4

HEAD_DIM = 64
ATTN_WIDTH = 3 * D_MODEL // 4
ATTN_HEADS = ATTN_WIDTH // HEAD_DIM
DILATED_GROUPS = ((128, 1), (512, 4), (2048, 16))
HEADS_PER_GROUP = ATTN_HEADS // len(DILATED_GROUPS)
ATTN_BLOCK = 128
ROT_DIM = HEAD_DIM // 4
ROPE_THETA = 500000.0
CONV_WIDTH = D_MODEL // 2
CONV_K = 3
SG_WIDTH = D_MODEL // 2
SG_CHUNK = 128
SG_GROUPS = 4
SG_GROUP_CH = SG_WIDTH // SG_GROUPS
N_BRANCH = 3
D_FF = 4 * D_MODEL
PLE_DIM = 256
RMS_EPS = 1e-6
LN_EPS = 1e-5
IN_WIDTH = 3 * ATTN_WIDTH + 3 * CONV_WIDTH + 2 * SG_WIDTH + N_BRANCH * D_MODEL

kernel_name = 'hybrid_gated_dilated_conv_sgu_trunk'


def rms_norm(x, g):
    xf = x.astype(jnp.float32)
    y = xf * lax.rsqrt(jnp.mean(xf * xf, axis=-1, keepdims=True) + RMS_EPS)
    return (y * g.astype(jnp.float32)).astype(x.dtype)


def layer_norm(x, g, b):
    xf = x.astype(jnp.float32)
    mu = jnp.mean(xf, axis=-1, keepdims=True)
    xc = xf - mu
    y = xc * lax.rsqrt(jnp.mean(xc * xc, axis=-1, keepdims=True) + LN_EPS)
    return (y * g.astype(jnp.float32) + b.astype(jnp.float32)).astype(x.dtype)


def rotary_partial(t, cos, sin):
    half = ROT_DIM // 2
    x1 = t[..., :half]
    x2 = t[..., half:ROT_DIM]
    rot = jnp.concatenate([x1 * cos - x2 * sin, x2 * cos + x1 * sin], axis=-1)
    return jnp.concatenate([rot, t[..., ROT_DIM:]], axis=-1)


def dilated_causal_attention(q, k, v, window, dilation):
    bsz, S, H, E = q.shape
    span = window // dilation
    assert span <= ATTN_BLOCK
    unit = dilation * ATTN_BLOCK
    L = -(-S // unit) * unit
    M = L // dilation
    NB = M // ATTN_BLOCK

    def to_blocks(t):
        t = jnp.pad(t, ((0, 0), (0, L - S), (0, 0), (0, 0)))
        t = t.reshape(bsz, M, dilation, H, E).transpose(0, 2, 3, 1, 4)
        return t.reshape(bsz, dilation, H, NB, ATTN_BLOCK, E)

    def with_prev(t):
        prev = jnp.pad(t, ((0, 0), (0, 0), (0, 0), (1, 0), (0, 0), (0, 0)))[:, :, :, :-1]
        return jnp.concatenate([prev, t], axis=-2)

    qb = to_blocks(q)
    kw = with_prev(to_blocks(k))
    vw = with_prev(to_blocks(v))
    s = jnp.einsum('brhnqe,brhnke->brhnqk', qb, kw).astype(jnp.float32) * (HEAD_DIM ** -0.5)
    qi = jnp.arange(ATTN_BLOCK)[:, None]
    kj = jnp.arange(2 * ATTN_BLOCK)[None, :]
    dist = qi + ATTN_BLOCK - kj
    band = (dist >= 0) & (dist <= span)
    blk = jnp.arange(NB)[:, None, None]
    mask = band[None] & (blk * ATTN_BLOCK + kj[None] - ATTN_BLOCK >= 0)
    s = jnp.where(mask, s, -jnp.inf)
    lse = jax.nn.logsumexp(s, axis=-1)
    probs = jnp.exp(s - lse[..., None])
    o = jnp.einsum('brhnqk,brhnke->brhnqe', probs.astype(v.dtype), vw)
    o = o.reshape(bsz, dilation, H, M, E).transpose(0, 3, 1, 2, 4).reshape(bsz, L, H, E)[:, :S]
    lse = lse.reshape(bsz, dilation, H, M).transpose(0, 3, 1, 2).reshape(bsz, L, H)[:, :S]
    return o, lse


def dilated_mixture(q, k, v):
    outs, lses = [], []
    for g, (window, dilation) in enumerate(DILATED_GROUPS):
        sl = slice(g * HEADS_PER_GROUP, (g + 1) * HEADS_PER_GROUP)
        o, l = dilated_causal_attention(q[:, :, sl], k[:, :, sl], v[:, :, sl], window, dilation)
        outs.append(o)
        lses.append(l)
    alpha = jax.nn.softmax(jnp.stack(lses, axis=0), axis=0)
    ya = jnp.concatenate([o * alpha[g][..., None].astype(o.dtype) for g, o in enumerate(outs)], axis=2)
    bsz, S = q.shape[0], q.shape[1]
    return ya.reshape(bsz, S, ATTN_WIDTH)


def short_conv(z, w):
    C = z.shape[-1]
    return lax.conv_general_dilated(z, w[:, None, :].astype(z.dtype), window_strides=(1,),
                                    padding=[(CONV_K - 1, 0)],
                                    dimension_numbers=('NWC', 'WIO', 'NWC'),
                                    feature_group_count=C)


def spatial_gating(zs, ln_g, ln_b, w_s, b_s):
    bsz, S, _ = zs.shape
    zs = jax.nn.gelu(zs, approximate=False)
    u, v = zs[..., :SG_WIDTH], zs[..., SG_WIDTH:]
    v = layer_norm(v, ln_g, ln_b)
    v = v.reshape(bsz, S // SG_CHUNK, SG_CHUNK, SG_GROUPS, SG_GROUP_CH)
    causal = jnp.tril(jnp.ones((SG_CHUNK, SG_CHUNK), dtype=bool))
    w = jnp.where(causal[None], w_s, jnp.zeros_like(w_s))
    sv = jnp.einsum('gts,bnsgc->bntgc', w, v) + b_s.T[None, None, :, :, None]
    return u * sv.reshape(bsz, S, SG_WIDTH)


def setup_inputs(seed: int = 0) -> dict:
    key = jax.random.key(seed)
    ks = jax.random.split(key, 24)

    def nrm(k, shape, scale):
        return jax.random.normal(k, shape, dtype=jnp.float32) * scale

    def gain(k, shape):
        return 1.0 + 0.05 * jax.random.normal(k, shape, dtype=jnp.float32)

    x = nrm(ks[0], (BATCH, SEQ, D_MODEL), 1.0)
    p = nrm(ks[1], (DEPTH, BATCH, SEQ, PLE_DIM), 1.0)
    offsets = jax.random.randint(ks[2], (BATCH, 1), 0, 1024, dtype=jnp.int32)
    positions = offsets + jnp.arange(SEQ, dtype=jnp.int32)[None, :]
    return {
        'x': x,
        'p': p,
        'positions': positions,
        'norm_mix_g': gain(ks[3], (DEPTH, D_MODEL)),
        'w_in': nrm(ks[4], (DEPTH, D_MODEL, IN_WIDTH), D_MODEL ** -0.5),
        'conv_w': nrm(ks[5], (DEPTH, CONV_K, CONV_WIDTH), CONV_K ** -0.5),
        'sg_ln_g': gain(ks[6], (DEPTH, SG_WIDTH)),
        'sg_ln_b': nrm(ks[7], (DEPTH, SG_WIDTH), 0.02),
        'sg_w': nrm(ks[8], (DEPTH, SG_GROUPS, SG_CHUNK, SG_CHUNK), SG_CHUNK ** -0.5),
        'sg_b': gain(ks[9], (DEPTH, SG_GROUPS, SG_CHUNK)),
        'w_branch_a': nrm(ks[10], (DEPTH, ATTN_WIDTH, D_MODEL), ATTN_WIDTH ** -0.5),
        'w_branch_b': nrm(ks[11], (DEPTH, CONV_WIDTH, D_MODEL), CONV_WIDTH ** -0.5),
        'w_branch_c': nrm(ks[12], (DEPTH, SG_WIDTH, D_MODEL), SG_WIDTH ** -0.5),
        'w_out': nrm(ks[13], (DEPTH, D_MODEL, D_MODEL), 0.5 * D_MODEL ** -0.5),
        'norm_mlp_g': gain(ks[14], (DEPTH, D_MODEL)),
        'w_up': nrm(ks[15], (DEPTH, D_MODEL, D_FF), D_MODEL ** -0.5),
        'w_down': nrm(ks[16], (DEPTH, D_FF, D_MODEL), 0.5 * D_FF ** -0.5),
        'norm_ple_g': gain(ks[17], (DEPTH, D_MODEL)),
        'w_ple_gate': nrm(ks[18], (DEPTH, D_MODEL, D_MODEL), D_MODEL ** -0.5),
        'w_ple_proj': nrm(ks[19], (DEPTH, PLE_DIM, D_MODEL), 0.5 * PLE_DIM ** -0.5),
        'norm_final_g': gain(ks[20], (D_MODEL,)),
    }


def reference(x, p, positions, norm_mix_g, w_in, conv_w, sg_ln_g, sg_ln_b, sg_w, sg_b,
              w_branch_a, w_branch_b, w_branch_c, w_out, norm_mlp_g, w_up, w_down,
              norm_ple_g, w_ple_gate, w_ple_proj, norm_final_g):
    bsz, S, _ = x.shape
    inv_freq = ROPE_THETA ** (-(jnp.arange(0, ROT_DIM, 2, dtype=jnp.float32) / ROT_DIM))
    ang = positions.astype(jnp.float32)[..., None] * inv_freq
    cos = jnp.cos(ang)[:, :, None, :].astype(x.dtype)
    sin = jnp.sin(ang)[:, :, None, :].astype(x.dtype)
    widths = [ATTN_WIDTH, ATTN_WIDTH, ATTN_WIDTH, CONV_WIDTH, CONV_WIDTH, CONV_WIDTH, 2 * SG_WIDTH]
    splits = [int(c) for c in np.cumsum(widths)]
    h = x
    for i in range(DEPTH):
        a = rms_norm(h, norm_mix_g[i])
        z = a @ w_in[i]
        zq, zk, zv, zx, zb, zc, zs, zg = jnp.split(z, splits, axis=-1)
        q = rotary_partial(zq.reshape(bsz, S, ATTN_HEADS, HEAD_DIM), cos, sin)
        k = rotary_partial(zk.reshape(bsz, S, ATTN_HEADS, HEAD_DIM), cos, sin)
        v = zv.reshape(bsz, S, ATTN_HEADS, HEAD_DIM)
        ya = dilated_mixture(q, k, v)
        yb = zb * short_conv(zc * zx, conv_w[i])
        yc = spatial_gating(zs, sg_ln_g[i], sg_ln_b[i], sg_w[i], sg_b[i])
        gates = jax.nn.sigmoid(zg.reshape(bsz, S, N_BRANCH, D_MODEL))
        m = (gates[:, :, 0] * (ya @ w_branch_a[i])
             + gates[:, :, 1] * (yb @ w_branch_b[i])
             + gates[:, :, 2] * (yc @ w_branch_c[i]))
        h = h + m @ w_out[i]
        c = rms_norm(h, norm_mlp_g[i])
        h = h + jnp.square(jax.nn.relu(c @ w_up[i])) @ w_down[i]
        e = rms_norm(h, norm_ple_g[i])
        h = h + jax.nn.sigmoid(e @ w_ple_gate[i]) * (p[i] @ w_ple_proj[i])
    return rms_norm(h, norm_final_g)
```

```python
import functools
import math

import jax
import jax.numpy as jnp
from jax import lax
from jax.experimental import pallas as pl
from jax.experimental.pallas import tpu as pltpu

F32 = jnp.float32
BF16 = jnp.bfloat16

HEAD_DIM = 64
HEADS_PER_GROUP = 4
GROUP_WIDTH = HEADS_PER_GROUP * HEAD_DIM
DILATIONS = (1, 4, 16)
ATTN_BLOCK = 128
ROT_HALF = 8
ROPE_THETA = 500000.0
CONV_K = 3
SG_CHUNK = 128
SG_GROUPS = 4
RMS_EPS = 1e-6
LN_EPS = 1e-5

LANES = 128
SUBLANES = 8
VMEM_LIMIT_BYTES = 56 * 1024 * 1024

QKV_ROWS = 512
ATTN_ROWS_MAX = 1024
MIX_ROWS = 256
MLP_ROWS = 512
MLP_FF_TILE = 1024
NORM_ROWS = 1024


def _rms_norm(h, g):
    ms = jnp.mean(h * h, axis=-1, keepdims=True)
    return h * lax.rsqrt(ms + RMS_EPS) * g


def _sigmoid(x):
    return 1.0 / (1.0 + jnp.exp(-x))


def _gelu(x):
    return 0.5 * x * (1.0 + lax.erf(x * (1.0 / math.sqrt(2.0))))


def _resident(shape):
    zeros = (0,) * len(shape)
    return pl.BlockSpec(shape, lambda *_: zeros, pipeline_mode=pl.Buffered(1))


def _qkv_kernel(h_ref, g_ref, w_ref, cos_ref, sin_ref, out_ref):
    a = _rms_norm(h_ref[...], g_ref[...]).astype(BF16)
    z = jnp.dot(a, w_ref[...], preferred_element_type=F32)
    cos_t = cos_ref[...]
    sin_t = sin_ref[...]
    lane = lax.broadcasted_iota(jnp.int32, cos_t.shape, 1)
    low_half = (lane % HEAD_DIM) < ROT_HALF
    n_tiles = out_ref.shape[0]
    for j in range(n_tiles):
        for half in range(GROUP_WIDTH // LANES):
            c0 = j * GROUP_WIDTH + half * LANES
            t = z[:, c0:c0 + LANES]
            if j < 2 * len(DILATIONS):
                partner = jnp.where(low_half,
                                    pltpu.roll(t, LANES - ROT_HALF, 1),
                                    pltpu.roll(t, ROT_HALF, 1))
                t = t * cos_t + partner * sin_t
            out_ref[j, :, half * LANES:(half + 1) * LANES] = t


def _qkv_projection(h2d, g, w_qkv, cos_t, sin_t):
    n_tok, d_model = h2d.shape
    n_tiles = w_qkv.shape[1] // GROUP_WIDTH
    tm = QKV_ROWS
    return pl.pallas_call(
        _qkv_kernel,
        grid=(n_tok // tm,),
        in_specs=[
            pl.BlockSpec((tm, d_model), lambda i: (i, 0)),
            _resident((1, d_model)),
            _resident(w_qkv.shape),
            pl.BlockSpec((tm, LANES), lambda i: (i, 0)),
            pl.BlockSpec((tm, LANES), lambda i: (i, 0)),
        ],
        out_specs=pl.BlockSpec((n_tiles, tm, GROUP_WIDTH), lambda i: (0, i, 0)),
        out_shape=jax.ShapeDtypeStruct((n_tiles, n_tok, GROUP_WIDTH), F32),
        compiler_params=pltpu.CompilerParams(
            dimension_semantics=("arbitrary",), vmem_limit_bytes=VMEM_LIMIT_BYTES),
        name="qkv_proj",
    )(h2d, g, w_qkv, cos_t, sin_t)


def _attn_kernel(q_ref, k_ref, kp_ref, v_ref, vp_ref, o_ref, l_ref, ks_ref, vs_ref, *, nblk):
    first_step = pl.program_id(2) == 0
    ks_ref[0:ATTN_BLOCK] = kp_ref[...].astype(BF16)
    ks_ref[ATTN_BLOCK:] = k_ref[...].astype(BF16)
    vs_ref[0:ATTN_BLOCK] = vp_ref[...].astype(BF16)
    vs_ref[ATTN_BLOCK:] = v_ref[...].astype(BF16)

    two = 2 * ATTN_BLOCK
    qi = lax.broadcasted_iota(jnp.int32, (two, two), 0) % ATTN_BLOCK
    kj = lax.broadcasted_iota(jnp.int32, (two, two), 1)
    dist = qi + ATTN_BLOCK - kj
    band = (dist >= 0) & (dist <= ATTN_BLOCK)
    band_first = band & ((kj >= ATTN_BLOCK) | jnp.logical_not(first_step))
    lane = lax.broadcasted_iota(jnp.int32, (ATTN_BLOCK, LANES), 1)
    head0 = lane < HEAD_DIM

    for i in range(nblk):
        valid = band_first if i == 0 else band
        r0 = i * ATTN_BLOCK
        for p in range(GROUP_WIDTH // LANES):
            c0 = p * LANES
            q2 = q_ref[r0:r0 + ATTN_BLOCK, c0:c0 + LANES] * (HEAD_DIM ** -0.5)
            qs = jnp.concatenate([jnp.where(head0, q2, 0.0), jnp.where(head0, 0.0, q2)],
                                 axis=0).astype(BF16)
            kk = ks_ref[r0:r0 + two, c0:c0 + LANES]
            s = lax.dot_general(qs, kk, (((1,), (1,)), ((), ())),
                                preferred_element_type=F32)
            s = jnp.where(valid, s, -jnp.inf)
            m = jnp.max(s, axis=-1, keepdims=True)
            e = jnp.exp(s - m)
            den = jnp.sum(e, axis=-1, keepdims=True)
            vv = vs_ref[r0:r0 + two, c0:c0 + LANES]
            pv = jnp.dot(e.astype(BF16), vv, preferred_element_type=F32)
            pv = pv * (1.0 / den)
            lse = m + jnp.log(den)
            o_ref[r0:r0 + ATTN_BLOCK, c0:c0 + LANES] = jnp.where(
                head0, pv[:ATTN_BLOCK], pv[ATTN_BLOCK:])
            l_ref[r0:r0 + ATTN_BLOCK, c0:c0 + LANES] = jnp.where(
                head0, lse[:ATTN_BLOCK], lse[ATTN_BLOCK:])


def _attention_group(qkv, group, dilation, batch, seq):
    n_groups = len(DILATIONS)
    m_len = seq // dilation
    mb = min(m_len, ATTN_ROWS_MAX)
    nblk = mb // ATTN_BLOCK
    view = qkv.reshape(qkv.shape[0], batch, m_len, dilation * GROUP_WIDTH)

    def rows(which):
        return pl.BlockSpec((None, None, mb, GROUP_WIDTH),
                            lambda b, r, n: (which * n_groups + group, b, n, r))

    def prev(which):
        return pl.BlockSpec((None, None, ATTN_BLOCK, GROUP_WIDTH),
                            lambda b, r, n: (which * n_groups + group, b,
                                             jnp.maximum(n * nblk - 1, 0), r))

    out_spec = pl.BlockSpec((None, mb, GROUP_WIDTH), lambda b, r, n: (b, n, r))
    out_sds = jax.ShapeDtypeStruct((batch, m_len, dilation * GROUP_WIDTH), F32)
    o, l = pl.pallas_call(
        functools.partial(_attn_kernel, nblk=nblk),
        grid=(batch, dilation, m_len // mb),
        in_specs=[rows(0), rows(1), prev(1), rows(2), prev(2)],
        out_specs=[out_spec, out_spec],
        out_shape=[out_sds, out_sds],
        scratch_shapes=[pltpu.VMEM((mb + ATTN_BLOCK, GROUP_WIDTH), BF16),
                        pltpu.VMEM((mb + ATTN_BLOCK, GROUP_WIDTH), BF16)],
        compiler_params=pltpu.CompilerParams(
            dimension_semantics=("arbitrary", "arbitrary", "arbitrary"),
            vmem_limit_bytes=VMEM_LIMIT_BYTES),
        name=f"attn_d{dilation}",
    )(view, view, view, view, view)
    return (o.reshape(batch * seq, GROUP_WIDTH), l.reshape(batch * seq, GROUP_WIDTH))


def _mix_kernel(h_ref, g_ref, o0_ref, o1_ref, o2_ref, l0_ref, l1_ref, l2_ref,
                wr_ref, cw_ref, lng_ref, lnb_ref, sgw_ref, sgb_ref,
                wa_ref, wb_ref, wc_ref, wo_ref, out_ref, carry_ref, yc_ref,
                *, tiles_per_seq, conv_w, sg_w):
    tm = h_ref.shape[0]

    @pl.when(pl.program_id(0) % tiles_per_seq == 0)
    def _():
        carry_ref[...] = jnp.zeros_like(carry_ref)

    h = h_ref[...]
    a = _rms_norm(h, g_ref[...]).astype(BF16)
    zr = jnp.dot(a, wr_ref[...], preferred_element_type=F32)
    c_b, c_c, c_u, c_v, c_g = conv_w, 2 * conv_w, 3 * conv_w, 3 * conv_w + sg_w, 3 * conv_w + 2 * sg_w
    zx, zb, zc = zr[:, :c_b], zr[:, c_b:c_c], zr[:, c_c:c_u]

    u = zc * zx
    ext = jnp.concatenate([carry_ref[...], u], axis=0)
    u1 = pltpu.roll(ext, 1, 0)[SUBLANES:]
    u2 = pltpu.roll(ext, 2, 0)[SUBLANES:]
    carry_ref[...] = u[tm - SUBLANES:]
    cw = cw_ref[...]
    yb = zb * (cw[0:1] * u2 + cw[1:2] * u1 + cw[2:3] * u)

    us = _gelu(zr[:, c_u:c_v])
    vs = _gelu(zr[:, c_v:c_g])
    mu = jnp.mean(vs, axis=-1, keepdims=True)
    xc = vs - mu
    var = jnp.mean(xc * xc, axis=-1, keepdims=True)
    vln = (xc * lax.rsqrt(var + LN_EPS) * lng_ref[...] + lnb_ref[...]).astype(BF16)
    ti = lax.broadcasted_iota(jnp.int32, (SG_CHUNK, SG_CHUNK), 0)
    si = lax.broadcasted_iota(jnp.int32, (SG_CHUNK, SG_CHUNK), 1)
    gch = sg_w // SG_GROUPS
    for g in range(SG_GROUPS):
        wg = jnp.where(ti >= si, sgw_ref[g], 0.0).astype(BF16)
        bias = sgb_ref[g]
        for n in range(tm // SG_CHUNK):
            rs = slice(n * SG_CHUNK, (n + 1) * SG_CHUNK)
            cs = slice(g * gch, (g + 1) * gch)
            sv = jnp.dot(wg, vln[rs, cs], preferred_element_type=F32) + bias
            yc_ref[rs, cs] = (us[rs, cs] * sv).astype(BF16)

    l0, l1, l2 = l0_ref[...], l1_ref[...], l2_ref[...]
    mx = jnp.maximum(jnp.maximum(l0, l1), l2)
    e0, e1, e2 = jnp.exp(l0 - mx), jnp.exp(l1 - mx), jnp.exp(l2 - mx)
    inv = 1.0 / (e0 + e1 + e2)
    ya = jnp.concatenate([o0_ref[...] * (e0 * inv), o1_ref[...] * (e1 * inv),
                          o2_ref[...] * (e2 * inv)], axis=1).astype(BF16)

    d_model = h.shape[1]
    pa = jnp.dot(ya, wa_ref[...], preferred_element_type=F32)
    pb = jnp.dot(yb.astype(BF16), wb_ref[...], preferred_element_type=F32)
    pc = jnp.dot(yc_ref[...], wc_ref[...], preferred_element_type=F32)
    m = (_sigmoid(zr[:, c_g:c_g + d_model]) * pa
         + _sigmoid(zr[:, c_g + d_model:c_g + 2 * d_model]) * pb
         + _sigmoid(zr[:, c_g + 2 * d_model:]) * pc)
    out_ref[...] = h + jnp.dot(m.astype(BF16), wo_ref[...], preferred_element_type=F32)


def _mixer_merge(h2d, g, att, w_rest, conv_w, ln_g, ln_b, sg_w, sg_b, w_a, w_b, w_c, w_o, seq):
    n_tok, d_model = h2d.shape
    tm = MIX_ROWS
    conv_width = conv_w.shape[1]
    sg_width = ln_g.shape[1]
    row_tile = lambda width: pl.BlockSpec((tm, width), lambda i: (i, 0))
    (o0, l0), (o1, l1), (o2, l2) = att
    kern = functools.partial(_mix_kernel, tiles_per_seq=seq // tm, conv_w=conv_width, sg_w=sg_width)
    return pl.pallas_call(
        kern,
        grid=(n_tok // tm,),
        in_specs=[row_tile(d_model), _resident(g.shape)]
        + [row_tile(GROUP_WIDTH)] * 6
        + [_resident(x.shape) for x in (w_rest, conv_w, ln_g, ln_b, sg_w, sg_b, w_a, w_b, w_c, w_o)],
        out_specs=row_tile(d_model),
        out_shape=jax.ShapeDtypeStruct((n_tok, d_model), F32),
        scratch_shapes=[pltpu.VMEM((SUBLANES, conv_width), F32),
                        pltpu.VMEM((tm, sg_width), BF16)],
        compiler_params=pltpu.CompilerParams(
            dimension_semantics=("arbitrary",), vmem_limit_bytes=VMEM_LIMIT_BYTES),
        name="mixer_merge",
    )(h2d, g, o0, o1, o2, l0, l1, l2, w_rest, conv_w, ln_g, ln_b, sg_w, sg_b, w_a, w_b, w_c, w_o)


def _mlp_kernel(h_ref, gm_ref, wu_ref, wd_ref, p_ref, gp_ref, wg_ref, wp_ref, out_ref,
                c_ref, acc_ref):
    j = pl.program_id(1)

    @pl.when(j == 0)
    def _():
        c_ref[...] = _rms_norm(h_ref[...], gm_ref[...]).astype(BF16)
        acc_ref[...] = jnp.zeros_like(acc_ref)

    t = jnp.dot(c_ref[...], wu_ref[...], preferred_element_type=F32)
    t = jnp.square(jnp.maximum(t, 0.0)).astype(BF16)
    acc_ref[...] += jnp.dot(t, wd_ref[...], preferred_element_type=F32)

    @pl.when(j == pl.num_programs(1) - 1)
    def _():
        h2 = h_ref[...] + acc_ref[...]
        e = _rms_norm(h2, gp_ref[...]).astype(BF16)
        gate = _sigmoid(jnp.dot(e, wg_ref[...], preferred_element_type=F32))
        emb = jnp.dot(p_ref[...].astype(BF16), wp_ref[...], preferred_element_type=F32)
        out_ref[...] = h2 + gate * emb


def _mlp_ple(h2d, g_mlp, w_up, w_down, p2d, g_ple, w_pg, w_pe):
    n_tok, d_model = h2d.shape
    d_ff = w_up.shape[1]
    tm, tf = MLP_ROWS, MLP_FF_TILE
    return pl.pallas_call(
        _mlp_kernel,
        grid=(n_tok // tm, d_ff // tf),
        in_specs=[
            pl.BlockSpec((tm, d_model), lambda i, j: (i, 0)),
            _resident(g_mlp.shape),
            pl.BlockSpec((d_model, tf), lambda i, j: (0, j)),
            pl.BlockSpec((tf, d_model), lambda i, j: (j, 0)),
            pl.BlockSpec((tm, p2d.shape[1]), lambda i, j: (i, 0)),
            _resident(g_ple.shape),
            _resident(w_pg.shape),
            _resident(w_pe.shape),
        ],
        out_specs=pl.BlockSpec((tm, d_model), lambda i, j: (i, 0)),
        out_shape=jax.ShapeDtypeStruct((n_tok, d_model), F32),
        scratch_shapes=[pltpu.VMEM((tm, d_model), BF16), pltpu.VMEM((tm, d_model), F32)],
        compiler_params=pltpu.CompilerParams(
            dimension_semantics=("arbitrary", "arbitrary"), vmem_limit_bytes=VMEM_LIMIT_BYTES),
        name="mlp_ple",
    )(h2d, g_mlp, w_up, w_down, p2d, g_ple, w_pg, w_pe)


def _norm_kernel(h_ref, g_ref, out_ref):
    out_ref[...] = _rms_norm(h_ref[...], g_ref[...])


def _final_norm(h2d, g):
    n_tok, d_model = h2d.shape
    tm = NORM_ROWS
    return pl.pallas_call(
        _norm_kernel,
        grid=(n_tok // tm,),
        in_specs=[pl.BlockSpec((tm, d_model), lambda i: (i, 0)), _resident(g.shape)],
        out_specs=pl.BlockSpec((tm, d_model), lambda i: (i, 0)),
        out_shape=jax.ShapeDtypeStruct((n_tok, d_model), F32),
        compiler_params=pltpu.CompilerParams(dimension_semantics=("arbitrary",)),
        name="final_norm",
    )(h2d, g)


def _rotary_tables(positions):
    inv_freq = ROPE_THETA ** (-(jnp.arange(0, 2 * ROT_HALF, 2, dtype=F32) / (2 * ROT_HALF)))
    ang = positions.astype(F32).reshape(-1, 1) * inv_freq
    cos, sin = jnp.cos(ang), jnp.sin(ang)
    rest = HEAD_DIM - 2 * ROT_HALF
    cos_h = jnp.concatenate([cos, cos, jnp.ones((ang.shape[0], rest), F32)], axis=1)
    sin_h = jnp.concatenate([-sin, sin, jnp.zeros((ang.shape[0], rest), F32)], axis=1)
    reps = LANES // HEAD_DIM
    return jnp.tile(cos_h, (1, reps)), jnp.tile(sin_h, (1, reps))


def kernel(x, p, positions, norm_mix_g, w_in, conv_w, sg_ln_g, sg_ln_b, sg_w, sg_b,
           w_branch_a, w_branch_b, w_branch_c, w_out, norm_mlp_g, w_up, w_down,
           norm_ple_g, w_ple_gate, w_ple_proj, norm_final_g):
    batch, seq, d_model = x.shape
    depth = w_in.shape[0]
    n_tok = batch * seq
    qkv_width = 3 * len(DILATIONS) * GROUP_WIDTH
    assert seq % (max(DILATIONS) * ATTN_BLOCK) == 0 and seq % ATTN_ROWS_MAX == 0
    assert n_tok % NORM_ROWS == 0 and seq % MIX_ROWS == 0

    cos_t, sin_t = _rotary_tables(positions)
    row = lambda v: v.reshape(1, -1)
    h = x.reshape(n_tok, d_model)
    for i in range(depth):
        w_in_b = w_in[i].astype(BF16)
        qkv = _qkv_projection(h, row(norm_mix_g[i]), w_in_b[:, :qkv_width], cos_t, sin_t)
        att = [_attention_group(qkv, g, d, batch, seq) for g, d in enumerate(DILATIONS)]
        sg_bias = jnp.broadcast_to(sg_b[i][:, :, None], sg_w[i].shape)
        h = _mixer_merge(h, row(norm_mix_g[i]), att, w_in_b[:, qkv_width:], conv_w[i],
                         row(sg_ln_g[i]), row(sg_ln_b[i]), sg_w[i], sg_bias,
                         w_branch_a[i].astype(BF16), w_branch_b[i].astype(BF16),
                         w_branch_c[i].astype(BF16), w_out[i].astype(BF16), seq)
        h = _mlp_ple(h, row(norm_mlp_g[i]), w_up[i].astype(BF16), w_down[i].astype(BF16),
                     p[i].reshape(n_tok, -1), row(norm_ple_g[i]),
                     w_ple_gate[i].astype(BF16), w_ple_proj[i].astype(BF16))
    return _final_norm(h, row(norm_final_g)).reshape(batch, seq, d_model)
```

```python
import functools
import math

import jax
import jax.numpy as jnp
from jax import lax
from jax.experimental import pallas as pl
from jax.experimental.pallas import tpu as pltpu

F32 = jnp.float32
BF16 = jnp.bfloat16

HEAD_DIM = 64
HEADS_PER_GROUP = 4
GROUP_WIDTH = HEADS_PER_GROUP * HEAD_DIM
DILATIONS = (1, 4, 16)
ATTN_BLOCK = 128
ROT_HALF = 8
ROPE_THETA = 500000.0
CONV_K = 3
SG_CHUNK = 128
SG_GROUPS = 4
RMS_EPS = 1e-6
LN_EPS = 1e-5

LANES = 128
SUBLANES = 8
VMEM_LIMIT_BYTES = 56 * 1024 * 1024

QKV_ROWS = 512
ATTN_ROWS_MAX = 1024
MIX_ROWS = 256
MLP_ROWS = 512
MLP_FF_TILE = 1024
NORM_ROWS = 1024


def _rms_norm(h, g):
    ms = jnp.mean(h * h, axis=-1, keepdims=True)
    return h * lax.rsqrt(ms + RMS_EPS) * g


def _sigmoid(x):
    return 1.0 / (1.0 + jnp.exp(-x))


def _gelu(x):
    return 0.5 * x * (1.0 + lax.erf(x * (1.0 / math.sqrt(2.0))))


def _resident(stacked, layer):
    shape = stacked.shape[1:]
    index = (layer,) + (0,) * len(shape)
    return pl.BlockSpec((None,) + shape, lambda *_: index, pipeline_mode=pl.Buffered(1))


def _qkv_kernel(h_ref, g_ref, w_ref, cos_ref, sin_ref, *refs):
    outs, slab_ref = refs[:-1], refs[-1]
    n_groups = len(outs)
    tm = h_ref.shape[0]
    a = _rms_norm(h_ref[...], g_ref[...]).astype(BF16)
    z = jnp.dot(a, w_ref[...], preferred_element_type=F32)
    cos_t = cos_ref[...]
    sin_t = sin_ref[...]
    lane = lax.broadcasted_iota(jnp.int32, cos_t.shape, 1)
    low_half = (lane % HEAD_DIM) < ROT_HALF
    for which in range(3):
        for g, d in enumerate(DILATIONS):
            for half in range(GROUP_WIDTH // LANES):
                c0 = (which * n_groups + g) * GROUP_WIDTH + half * LANES
                t = z[:, c0:c0 + LANES]
                if which < 2:
                    partner = jnp.where(low_half,
                                        pltpu.roll(t, LANES - ROT_HALF, 1),
                                        pltpu.roll(t, ROT_HALF, 1))
                    t = t * cos_t + partner * sin_t
                if d == 1:
                    outs[g][which, :, half * LANES:(half + 1) * LANES] = t
                else:
                    slab_ref[...] = t
                    for r in range(d):
                        l0 = r * GROUP_WIDTH + half * LANES
                        outs[g][which, :, l0:l0 + LANES] = slab_ref[pl.ds(r, tm // d, stride=d), :]


def _qkv_projection(layer, h2d, g, w_qkv, cos_t, sin_t):
    n_tok, d_model = h2d.shape
    tm = QKV_ROWS
    return pl.pallas_call(
        _qkv_kernel,
        grid=(n_tok // tm,),
        in_specs=[
            pl.BlockSpec((tm, d_model), lambda i: (i, 0)),
            _resident(g, layer),
            _resident(w_qkv, layer),
            pl.BlockSpec((tm, LANES), lambda i: (i, 0)),
            pl.BlockSpec((tm, LANES), lambda i: (i, 0)),
        ],
        out_specs=[pl.BlockSpec((3, tm // d, d * GROUP_WIDTH), lambda i: (0, i, 0))
                   for d in DILATIONS],
        out_shape=[jax.ShapeDtypeStruct((3, n_tok // d, d * GROUP_WIDTH), F32)
                   for d in DILATIONS],
        scratch_shapes=[pltpu.VMEM((tm, LANES), F32)],
        compiler_params=pltpu.CompilerParams(
            dimension_semantics=("arbitrary",), vmem_limit_bytes=VMEM_LIMIT_BYTES),
        name="qkv_proj",
    )(h2d, g, w_qkv, cos_t, sin_t)


def _attn_kernel(*refs, nblk, has_prev):
    if has_prev:
        q_ref, k_ref, kp_ref, v_ref, vp_ref, o_ref, l_ref, ks_ref, vs_ref = refs
        ks_ref[0:ATTN_BLOCK] = kp_ref[...].astype(BF16)
        vs_ref[0:ATTN_BLOCK] = vp_ref[...].astype(BF16)
        first_step = pl.program_id(2) == 0
    else:
        q_ref, k_ref, v_ref, o_ref, l_ref, ks_ref, vs_ref = refs
        ks_ref[0:ATTN_BLOCK] = jnp.zeros((ATTN_BLOCK, ks_ref.shape[1]), BF16)
        vs_ref[0:ATTN_BLOCK] = jnp.zeros((ATTN_BLOCK, vs_ref.shape[1]), BF16)
        first_step = True
    ks_ref[ATTN_BLOCK:] = k_ref[...].astype(BF16)
    vs_ref[ATTN_BLOCK:] = v_ref[...].astype(BF16)

    two = 2 * ATTN_BLOCK
    qi = lax.broadcasted_iota(jnp.int32, (two, two), 0) % ATTN_BLOCK
    kj = lax.broadcasted_iota(jnp.int32, (two, two), 1)
    dist = qi + ATTN_BLOCK - kj
    band = (dist >= 0) & (dist <= ATTN_BLOCK)
    band_first = band & ((kj >= ATTN_BLOCK) | jnp.logical_not(first_step))
    lane = lax.broadcasted_iota(jnp.int32, (ATTN_BLOCK, LANES), 1)
    head0 = lane < HEAD_DIM

    for i in range(nblk):
        valid = band_first if i == 0 else band
        r0 = i * ATTN_BLOCK
        for p in range(q_ref.shape[1] // LANES):
            c0 = p * LANES
            q2 = q_ref[r0:r0 + ATTN_BLOCK, c0:c0 + LANES] * (HEAD_DIM ** -0.5)
            qs = jnp.concatenate([jnp.where(head0, q2, 0.0), jnp.where(head0, 0.0, q2)],
                                 axis=0).astype(BF16)
            kk = ks_ref[r0:r0 + two, c0:c0 + LANES]
            s = lax.dot_general(qs, kk, (((1,), (1,)), ((), ())),
                                preferred_element_type=F32)
            s = jnp.where(valid, s, -jnp.inf)
            m = jnp.max(s, axis=-1, keepdims=True)
            e = jnp.exp(s - m)
            den = jnp.sum(e, axis=-1, keepdims=True)
            vv = vs_ref[r0:r0 + two, c0:c0 + LANES]
            pv = jnp.dot(e.astype(BF16), vv, preferred_element_type=F32)
            pv = pv * (1.0 / den)
            lse = m + jnp.log(den)
            o_ref[r0:r0 + ATTN_BLOCK, c0:c0 + LANES] = jnp.where(
                head0, pv[:ATTN_BLOCK], pv[ATTN_BLOCK:])
            l_ref[r0:r0 + ATTN_BLOCK, c0:c0 + LANES] = jnp.where(
                head0, lse[:ATTN_BLOCK], lse[ATTN_BLOCK:])


def _attention_group(qkv, dilation, batch, seq):
    m_len = seq // dilation
    mb = min(m_len, ATTN_ROWS_MAX)
    nblk = mb // ATTN_BLOCK
    n_res = min(dilation, ATTN_ROWS_MAX // mb)
    width = n_res * GROUP_WIDTH
    has_prev = mb < m_len
    view = qkv.reshape(3, batch, m_len, dilation * GROUP_WIDTH)

    def rows(which):
        return pl.BlockSpec((None, None, mb, width), lambda b, r, n: (which, b, n, r))

    def prev(which):
        return pl.BlockSpec((None, None, ATTN_BLOCK, width),
                            lambda b, r, n: (which, b, jnp.maximum(n * nblk - 1, 0), r))

    if has_prev:
        in_specs = [rows(0), rows(1), prev(1), rows(2), prev(2)]
    else:
        in_specs = [rows(0), rows(1), rows(2)]
    out_spec = pl.BlockSpec((None, mb, width), lambda b, r, n: (b, n, r))
    out_sds = jax.ShapeDtypeStruct((batch, m_len, dilation * GROUP_WIDTH), F32)
    o, l = pl.pallas_call(
        functools.partial(_attn_kernel, nblk=nblk, has_prev=has_prev),
        grid=(batch, dilation // n_res, m_len // mb),
        in_specs=in_specs,
        out_specs=[out_spec, out_spec],
        out_shape=[out_sds, out_sds],
        scratch_shapes=[pltpu.VMEM((mb + ATTN_BLOCK, width), BF16),
                        pltpu.VMEM((mb + ATTN_BLOCK, width), BF16)],
        compiler_params=pltpu.CompilerParams(
            dimension_semantics=("arbitrary", "arbitrary", "arbitrary"),
            vmem_limit_bytes=VMEM_LIMIT_BYTES),
        name=f"attn_d{dilation}",
    )(*([view] * len(in_specs)))
    rows2d = batch * m_len
    return (o.reshape(rows2d, dilation * GROUP_WIDTH), l.reshape(rows2d, dilation * GROUP_WIDTH))


def _mix_kernel(h_ref, g_ref, o0_ref, o1_ref, o2_ref, l0_ref, l1_ref, l2_ref,
                wr_ref, cw_ref, lng_ref, lnb_ref, sgw_ref, sgb_ref,
                wa_ref, wb_ref, wc_ref, wo_ref, out_ref, carry_ref, yc_ref, nat_ref,
                *, tiles_per_seq, conv_w, sg_w):
    tm = h_ref.shape[0]

    @pl.when(pl.program_id(0) % tiles_per_seq == 0)
    def _():
        carry_ref[...] = jnp.zeros_like(carry_ref)

    h = h_ref[...]
    a = _rms_norm(h, g_ref[...]).astype(BF16)
    zr = jnp.dot(a, wr_ref[...], preferred_element_type=F32)
    c_b, c_c, c_u, c_v, c_g = conv_w, 2 * conv_w, 3 * conv_w, 3 * conv_w + sg_w, 3 * conv_w + 2 * sg_w
    zx, zb, zc = zr[:, :c_b], zr[:, c_b:c_c], zr[:, c_c:c_u]

    u = zc * zx
    ext = jnp.concatenate([carry_ref[...], u], axis=0)
    u1 = pltpu.roll(ext, 1, 0)[SUBLANES:]
    u2 = pltpu.roll(ext, 2, 0)[SUBLANES:]
    carry_ref[...] = u[tm - SUBLANES:]
    cw = cw_ref[...]
    yb = zb * (cw[0:1] * u2 + cw[1:2] * u1 + cw[2:3] * u)

    us = _gelu(zr[:, c_u:c_v])
    vs = _gelu(zr[:, c_v:c_g])
    mu = jnp.mean(vs, axis=-1, keepdims=True)
    xc = vs - mu
    var = jnp.mean(xc * xc, axis=-1, keepdims=True)
    vln = (xc * lax.rsqrt(var + LN_EPS) * lng_ref[...] + lnb_ref[...]).astype(BF16)
    ti = lax.broadcasted_iota(jnp.int32, (SG_CHUNK, SG_CHUNK), 0)
    si = lax.broadcasted_iota(jnp.int32, (SG_CHUNK, SG_CHUNK), 1)
    gch = sg_w // SG_GROUPS
    for g in range(SG_GROUPS):
        wg = jnp.where(ti >= si, sgw_ref[g], 0.0).astype(BF16)
        bias = sgb_ref[g]
        for n in range(tm // SG_CHUNK):
            rs = slice(n * SG_CHUNK, (n + 1) * SG_CHUNK)
            cs = slice(g * gch, (g + 1) * gch)
            sv = jnp.dot(wg, vln[rs, cs], preferred_element_type=F32) + bias
            yc_ref[rs, cs] = (us[rs, cs] * sv).astype(BF16)

    halves = GROUP_WIDTH // LANES

    def natural(src_ref, slot, d):
        if d == 1:
            return src_ref[...]
        for r in range(d):
            for half in range(halves):
                l0 = r * GROUP_WIDTH + half * LANES
                nat_ref[slot * halves + half, pl.ds(r, tm // d, stride=d), :] = src_ref[:, l0:l0 + LANES]
        return jnp.concatenate([nat_ref[slot * halves + half] for half in range(halves)], axis=1)

    o_refs, l_refs = (o0_ref, o1_ref, o2_ref), (l0_ref, l1_ref, l2_ref)
    o0, o1, o2 = [natural(r, 2 * g, d) for g, (r, d) in enumerate(zip(o_refs, DILATIONS))]
    l0, l1, l2 = [natural(r, 2 * g + 1, d) for g, (r, d) in enumerate(zip(l_refs, DILATIONS))]
    mx = jnp.maximum(jnp.maximum(l0, l1), l2)
    e0, e1, e2 = jnp.exp(l0 - mx), jnp.exp(l1 - mx), jnp.exp(l2 - mx)
    inv = 1.0 / (e0 + e1 + e2)
    ya = jnp.concatenate([o0 * (e0 * inv), o1 * (e1 * inv), o2 * (e2 * inv)], axis=1).astype(BF16)

    d_model = h.shape[1]
    pa = jnp.dot(ya, wa_ref[...], preferred_element_type=F32)
    pb = jnp.dot(yb.astype(BF16), wb_ref[...], preferred_element_type=F32)
    pc = jnp.dot(yc_ref[...], wc_ref[...], preferred_element_type=F32)
    m = (_sigmoid(zr[:, c_g:c_g + d_model]) * pa
         + _sigmoid(zr[:, c_g + d_model:c_g + 2 * d_model]) * pb
         + _sigmoid(zr[:, c_g + 2 * d_model:]) * pc)
    out_ref[...] = h + jnp.dot(m.astype(BF16), wo_ref[...], preferred_element_type=F32)


def _mixer_merge(layer, h2d, g, att, w_rest, conv_w, ln_g, ln_b, sg_w, sg_b, w_a, w_b, w_c, w_o, seq):
    n_tok, d_model = h2d.shape
    tm = MIX_ROWS
    conv_width = conv_w.shape[-1]
    sg_width = ln_g.shape[-1]
    row_tile = lambda width: pl.BlockSpec((tm, width), lambda i: (i, 0))
    (o0, l0), (o1, l1), (o2, l2) = att
    att_tiles = [pl.BlockSpec((tm // d, d * GROUP_WIDTH), lambda i: (i, 0)) for d in DILATIONS]
    kern = functools.partial(_mix_kernel, tiles_per_seq=seq // tm, conv_w=conv_width, sg_w=sg_width)
    return pl.pallas_call(
        kern,
        grid=(n_tok // tm,),
        in_specs=[row_tile(d_model), _resident(g, layer)]
        + att_tiles * 2
        + [_resident(x, layer) for x in (w_rest, conv_w, ln_g, ln_b, sg_w, sg_b, w_a, w_b, w_c, w_o)],
        out_specs=row_tile(d_model),
        out_shape=jax.ShapeDtypeStruct((n_tok, d_model), F32),
        scratch_shapes=[pltpu.VMEM((SUBLANES, conv_width), F32),
                        pltpu.VMEM((tm, sg_width), BF16),
                        pltpu.VMEM((2 * len(DILATIONS) * (GROUP_WIDTH // LANES), tm, LANES), F32)],
        compiler_params=pltpu.CompilerParams(
            dimension_semantics=("arbitrary",), vmem_limit_bytes=VMEM_LIMIT_BYTES),
        name="mixer_merge",
    )(h2d, g, o0, o1, o2, l0, l1, l2, w_rest, conv_w, ln_g, ln_b, sg_w, sg_b, w_a, w_b, w_c, w_o)


def _mlp_kernel(h_ref, gm_ref, wu_ref, wd_ref, p_ref, gp_ref, wg_ref, wp_ref, out_ref,
                c_ref, acc_ref):
    j = pl.program_id(1)

    @pl.when(j == 0)
    def _():
        c_ref[...] = _rms_norm(h_ref[...], gm_ref[...]).astype(BF16)
        acc_ref[...] = jnp.zeros_like(acc_ref)

    t = jnp.dot(c_ref[...], wu_ref[...], preferred_element_type=F32)
    t = jnp.square(jnp.maximum(t, 0.0)).astype(BF16)
    acc_ref[...] += jnp.dot(t, wd_ref[...], preferred_element_type=F32)

    @pl.when(j == pl.num_programs(1) - 1)
    def _():
        h2 = h_ref[...] + acc_ref[...]
        e = _rms_norm(h2, gp_ref[...]).astype(BF16)
        gate = _sigmoid(jnp.dot(e, wg_ref[...], preferred_element_type=F32))
        emb = jnp.dot(p_ref[...].astype(BF16), wp_ref[...], preferred_element_type=F32)
        out_ref[...] = h2 + gate * emb


def _mlp_ple(layer, h2d, g_mlp, w_up, w_down, p3d, g_ple, w_pg, w_pe):
    n_tok, d_model = h2d.shape
    d_ff = w_up.shape[-1]
    tm, tf = MLP_ROWS, MLP_FF_TILE
    return pl.pallas_call(
        _mlp_kernel,
        grid=(n_tok // tm, d_ff // tf),
        in_specs=[
            pl.BlockSpec((tm, d_model), lambda i, j: (i, 0)),
            _resident(g_mlp, layer),
            pl.BlockSpec((None, d_model, tf), lambda i, j: (layer, 0, j)),
            pl.BlockSpec((None, tf, d_model), lambda i, j: (layer, j, 0)),
            pl.BlockSpec((None, tm, p3d.shape[-1]), lambda i, j: (layer, i, 0)),
            _resident(g_ple, layer),
            _resident(w_pg, layer),
            _resident(w_pe, layer),
        ],
        out_specs=pl.BlockSpec((tm, d_model), lambda i, j: (i, 0)),
        out_shape=jax.ShapeDtypeStruct((n_tok, d_model), F32),
        scratch_shapes=[pltpu.VMEM((tm, d_model), BF16), pltpu.VMEM((tm, d_model), F32)],
        compiler_params=pltpu.CompilerParams(
            dimension_semantics=("arbitrary", "arbitrary"), vmem_limit_bytes=VMEM_LIMIT_BYTES),
        name="mlp_ple",
    )(h2d, g_mlp, w_up, w_down, p3d, g_ple, w_pg, w_pe)


def _norm_kernel(h_ref, g_ref, out_ref):
    out_ref[...] = _rms_norm(h_ref[...], g_ref[...])


def _final_norm(h2d, g):
    n_tok, d_model = h2d.shape
    tm = NORM_ROWS
    return pl.pallas_call(
        _norm_kernel,
        grid=(n_tok // tm,),
        in_specs=[pl.BlockSpec((tm, d_model), lambda i: (i, 0)), _resident(g, 0)],
        out_specs=pl.BlockSpec((tm, d_model), lambda i: (i, 0)),
        out_shape=jax.ShapeDtypeStruct((n_tok, d_model), F32),
        compiler_params=pltpu.CompilerParams(dimension_semantics=("arbitrary",)),
        name="final_norm",
    )(h2d, g)


def _rotary_tables(positions):
    inv_freq = ROPE_THETA ** (-(jnp.arange(0, 2 * ROT_HALF, 2, dtype=F32) / (2 * ROT_HALF)))
    ang = positions.astype(F32).reshape(-1, 1) * inv_freq
    cos, sin = jnp.cos(ang), jnp.sin(ang)
    rest = HEAD_DIM - 2 * ROT_HALF
    cos_h = jnp.concatenate([cos, cos, jnp.ones((ang.shape[0], rest), F32)], axis=1)
    sin_h = jnp.concatenate([-sin, sin, jnp.zeros((ang.shape[0], rest), F32)], axis=1)
    reps = LANES // HEAD_DIM
    return jnp.tile(cos_h, (1, reps)), jnp.tile(sin_h, (1, reps))


def kernel(x, p, positions, norm_mix_g, w_in, conv_w, sg_ln_g, sg_ln_b, sg_w, sg_b,
           w_branch_a, w_branch_b, w_branch_c, w_out, norm_mlp_g, w_up, w_down,
           norm_ple_g, w_ple_gate, w_ple_proj, norm_final_g):
    batch, seq, d_model = x.shape
    depth = w_in.shape[0]
    n_tok = batch * seq
    qkv_width = 3 * len(DILATIONS) * GROUP_WIDTH
    assert seq % (max(DILATIONS) * ATTN_BLOCK) == 0 and seq % ATTN_ROWS_MAX == 0
    assert n_tok % NORM_ROWS == 0 and seq % MIX_ROWS == 0

    cos_t, sin_t = _rotary_tables(positions)
    rows = lambda v: v.reshape(v.shape[0], 1, v.shape[-1])
    bf = lambda w: w.astype(BF16)
    w_qkv = bf(w_in[:, :, :qkv_width])
    w_rest = bf(w_in[:, :, qkv_width:])
    sg_bias = jnp.broadcast_to(sg_b[:, :, :, None], sg_w.shape)
    w_a, w_b, w_c, w_o = bf(w_branch_a), bf(w_branch_b), bf(w_branch_c), bf(w_out)
    w_u, w_d, w_pg, w_pe = bf(w_up), bf(w_down), bf(w_ple_gate), bf(w_ple_proj)
    g_mix, g_mlp, g_ple = rows(norm_mix_g), rows(norm_mlp_g), rows(norm_ple_g)
    ln_g, ln_b = rows(sg_ln_g), rows(sg_ln_b)
    p3d = p.reshape(depth, n_tok, p.shape[-1])

    h = x.reshape(n_tok, d_model)
    for i in range(depth):
        qkv = _qkv_projection(i, h, g_mix, w_qkv, cos_t, sin_t)
        att = [_attention_group(qkv_g, d, batch, seq) for qkv_g, d in zip(qkv, DILATIONS)]
        h = _mixer_merge(i, h, g_mix, att, w_rest, conv_w, ln_g, ln_b, sg_w, sg_bias,
                         w_a, w_b, w_c, w_o, seq)
        h = _mlp_ple(i, h, g_mlp, w_u, w_d, p3d, g_ple, w_pg, w_pe)
    return _final_norm(h, norm_final_g.reshape(1, 1, -1)).reshape(batch, seq, d_model)
```

```python
import functools
import math

import jax
import jax.numpy as jnp
from jax import lax
from jax.experimental import pallas as pl
from jax.experimental.pallas import tpu as pltpu

F32 = jnp.float32
BF16 = jnp.bfloat16

HEAD_DIM = 64
HEADS_PER_GROUP = 4
GROUP_WIDTH = HEADS_PER_GROUP * HEAD_DIM
DILATIONS = (1, 4, 16)
ATTN_BLOCK = 128
ROT_HALF = 8
ROPE_THETA = 500000.0
CONV_K = 3
SG_CHUNK = 128
SG_GROUPS = 4
RMS_EPS = 1e-6
LN_EPS = 1e-5

LANES = 128
SUBLANES = 8
VMEM_LIMIT_BYTES = 56 * 1024 * 1024

QKV_ROWS = 512
ATTN_ROWS_MAX = 1024
MIX_ROWS = 512
MIX_SUB_ROWS = 256
MLP_ROWS = 512
MLP_SUB_ROWS = 256
MLP_FF_TILE = 1024
NORM_ROWS = 1024


def _rms_norm(h, g):
    ms = jnp.mean(h * h, axis=-1, keepdims=True)
    return h * lax.rsqrt(ms + RMS_EPS) * g


def _sigmoid(x):
    return 1.0 / (1.0 + jnp.exp(-x))


def _gelu(x):
    return 0.5 * x * (1.0 + lax.erf(x * (1.0 / math.sqrt(2.0))))


def _resident(stacked, layer):
    shape = stacked.shape[1:]
    index = (layer,) + (0,) * len(shape)
    return pl.BlockSpec((None,) + shape, lambda *_: index, pipeline_mode=pl.Buffered(1))


def _qkv_kernel(h_ref, g_ref, w_ref, cos_ref, sin_ref, *refs):
    outs, slab_ref = refs[:-1], refs[-1]
    n_groups = len(outs)
    tm = h_ref.shape[0]
    a = _rms_norm(h_ref[...], g_ref[...]).astype(BF16)
    z = jnp.dot(a, w_ref[...], preferred_element_type=F32)
    cos_t = cos_ref[...]
    sin_t = sin_ref[...]
    lane = lax.broadcasted_iota(jnp.int32, cos_t.shape, 1)
    low_half = (lane % HEAD_DIM) < ROT_HALF
    for which in range(3):
        for g, d in enumerate(DILATIONS):
            for half in range(GROUP_WIDTH // LANES):
                c0 = (which * n_groups + g) * GROUP_WIDTH + half * LANES
                t = z[:, c0:c0 + LANES]
                if which < 2:
                    partner = jnp.where(low_half,
                                        pltpu.roll(t, LANES - ROT_HALF, 1),
                                        pltpu.roll(t, ROT_HALF, 1))
                    t = t * cos_t + partner * sin_t
                if d == 1:
                    outs[g][which, :, half * LANES:(half + 1) * LANES] = t
                else:
                    slab_ref[...] = t
                    for r in range(d):
                        l0 = r * GROUP_WIDTH + half * LANES
                        outs[g][which, :, l0:l0 + LANES] = slab_ref[pl.ds(r, tm // d, stride=d), :]


def _qkv_projection(layer, h2d, g, w_qkv, cos_t, sin_t):
    n_tok, d_model = h2d.shape
    tm = QKV_ROWS
    return pl.pallas_call(
        _qkv_kernel,
        grid=(n_tok // tm,),
        in_specs=[
            pl.BlockSpec((tm, d_model), lambda i: (i, 0)),
            _resident(g, layer),
            _resident(w_qkv, layer),
            pl.BlockSpec((tm, LANES), lambda i: (i, 0)),
            pl.BlockSpec((tm, LANES), lambda i: (i, 0)),
        ],
        out_specs=[pl.BlockSpec((3, tm // d, d * GROUP_WIDTH), lambda i: (0, i, 0))
                   for d in DILATIONS],
        out_shape=[jax.ShapeDtypeStruct((3, n_tok // d, d * GROUP_WIDTH), F32)
                   for d in DILATIONS],
        scratch_shapes=[pltpu.VMEM((tm, LANES), F32)],
        compiler_params=pltpu.CompilerParams(
            dimension_semantics=("arbitrary",), vmem_limit_bytes=VMEM_LIMIT_BYTES),
        name="qkv_proj",
    )(h2d, g, w_qkv, cos_t, sin_t)


def _attn_kernel(*refs, nblk, has_prev):
    if has_prev:
        q_ref, k_ref, kp_ref, v_ref, vp_ref, o_ref, l_ref, ks_ref, vs_ref = refs
        ks_ref[0:ATTN_BLOCK] = kp_ref[...].astype(BF16)
        vs_ref[0:ATTN_BLOCK] = vp_ref[...].astype(BF16)
        first_step = pl.program_id(2) == 0
    else:
        q_ref, k_ref, v_ref, o_ref, l_ref, ks_ref, vs_ref = refs
        ks_ref[0:ATTN_BLOCK] = jnp.zeros((ATTN_BLOCK, ks_ref.shape[1]), BF16)
        vs_ref[0:ATTN_BLOCK] = jnp.zeros((ATTN_BLOCK, vs_ref.shape[1]), BF16)
        first_step = True
    ks_ref[ATTN_BLOCK:] = k_ref[...].astype(BF16)
    vs_ref[ATTN_BLOCK:] = v_ref[...].astype(BF16)

    two = 2 * ATTN_BLOCK
    qi = lax.broadcasted_iota(jnp.int32, (two, two), 0) % ATTN_BLOCK
    kj = lax.broadcasted_iota(jnp.int32, (two, two), 1)
    dist = qi + ATTN_BLOCK - kj
    band = (dist >= 0) & (dist <= ATTN_BLOCK)
    band_first = band & ((kj >= ATTN_BLOCK) | jnp.logical_not(first_step))
    lane = lax.broadcasted_iota(jnp.int32, (ATTN_BLOCK, LANES), 1)
    head0 = lane < HEAD_DIM

    for i in range(nblk):
        valid = band_first if i == 0 else band
        r0 = i * ATTN_BLOCK
        for p in range(q_ref.shape[1] // LANES):
            c0 = p * LANES
            q2 = q_ref[r0:r0 + ATTN_BLOCK, c0:c0 + LANES] * (HEAD_DIM ** -0.5)
            qs = jnp.concatenate([jnp.where(head0, q2, 0.0), jnp.where(head0, 0.0, q2)],
                                 axis=0).astype(BF16)
            kk = ks_ref[r0:r0 + two, c0:c0 + LANES]
            s = lax.dot_general(qs, kk, (((1,), (1,)), ((), ())),
                                preferred_element_type=F32)
            s = jnp.where(valid, s, -jnp.inf)
            m = jnp.max(s, axis=-1, keepdims=True)
            e = jnp.exp(s - m)
            den = jnp.sum(e, axis=-1, keepdims=True)
            vv = vs_ref[r0:r0 + two, c0:c0 + LANES]
            pv = jnp.dot(e.astype(BF16), vv, preferred_element_type=F32)
            pv = pv * (1.0 / den)
            lse = m + jnp.log(den)
            o_ref[r0:r0 + ATTN_BLOCK, c0:c0 + LANES] = jnp.where(
                head0, pv[:ATTN_BLOCK], pv[ATTN_BLOCK:])
            l_ref[r0:r0 + ATTN_BLOCK, c0:c0 + LANES] = jnp.where(
                head0, lse[:ATTN_BLOCK], lse[ATTN_BLOCK:])


def _attention_group(qkv, dilation, batch, seq):
    m_len = seq // dilation
    mb = min(m_len, ATTN_ROWS_MAX)
    nblk = mb // ATTN_BLOCK
    n_res = min(dilation, ATTN_ROWS_MAX // mb)
    width = n_res * GROUP_WIDTH
    has_prev = mb < m_len
    view = qkv.reshape(3, batch, m_len, dilation * GROUP_WIDTH)

    def rows(which):
        return pl.BlockSpec((None, None, mb, width), lambda b, r, n: (which, b, n, r))

    def prev(which):
        return pl.BlockSpec((None, None, ATTN_BLOCK, width),
                            lambda b, r, n: (which, b, jnp.maximum(n * nblk - 1, 0), r))

    if has_prev:
        in_specs = [rows(0), rows(1), prev(1), rows(2), prev(2)]
    else:
        in_specs = [rows(0), rows(1), rows(2)]
    out_spec = pl.BlockSpec((None, mb, width), lambda b, r, n: (b, n, r))
    out_sds = jax.ShapeDtypeStruct((batch, m_len, dilation * GROUP_WIDTH), F32)
    o, l = pl.pallas_call(
        functools.partial(_attn_kernel, nblk=nblk, has_prev=has_prev),
        grid=(batch, dilation // n_res, m_len // mb),
        in_specs=in_specs,
        out_specs=[out_spec, out_spec],
        out_shape=[out_sds, out_sds],
        scratch_shapes=[pltpu.VMEM((mb + ATTN_BLOCK, width), BF16),
                        pltpu.VMEM((mb + ATTN_BLOCK, width), BF16)],
        compiler_params=pltpu.CompilerParams(
            dimension_semantics=("arbitrary", "arbitrary", "arbitrary"),
            vmem_limit_bytes=VMEM_LIMIT_BYTES),
        name=f"attn_d{dilation}",
    )(*([view] * len(in_specs)))
    rows2d = batch * m_len
    return (o.reshape(rows2d, dilation * GROUP_WIDTH), l.reshape(rows2d, dilation * GROUP_WIDTH))


def _mix_kernel(h_ref, g_ref, o0_ref, o1_ref, o2_ref, l0_ref, l1_ref, l2_ref,
                wr_ref, cw_ref, lng_ref, lnb_ref, sgw_ref, sgb_ref,
                wa_ref, wb_ref, wc_ref, wo_ref, out_ref, carry_ref, yc_ref, nat_ref,
                *, tiles_per_seq, conv_w, sg_w):
    tm, d_model = h_ref.shape
    halves = GROUP_WIDTH // LANES

    @pl.when(pl.program_id(0) % tiles_per_seq == 0)
    def _():
        carry_ref[...] = jnp.zeros_like(carry_ref)

    att_refs = (o0_ref, l0_ref, o1_ref, l1_ref, o2_ref, l2_ref)
    for slot, src_ref in enumerate(att_refs):
        d = DILATIONS[slot // 2]
        for r in range(d if d > 1 else 0):
            for half in range(halves):
                l0 = r * GROUP_WIDTH + half * LANES
                nat_ref[slot * halves + half, pl.ds(r, tm // d, stride=d), :] = src_ref[:, l0:l0 + LANES]

    def natural(slot, rs):
        if DILATIONS[slot // 2] == 1:
            return att_refs[slot][rs, :]
        return jnp.concatenate([nat_ref[slot * halves + half, rs, :] for half in range(halves)], axis=1)

    c_b, c_c, c_u, c_v, c_g = conv_w, 2 * conv_w, 3 * conv_w, 3 * conv_w + sg_w, 3 * conv_w + 2 * sg_w
    ti = lax.broadcasted_iota(jnp.int32, (SG_CHUNK, SG_CHUNK), 0)
    si = lax.broadcasted_iota(jnp.int32, (SG_CHUNK, SG_CHUNK), 1)
    gch = sg_w // SG_GROUPS
    w_sg = [jnp.where(ti >= si, sgw_ref[g], 0.0).astype(BF16) for g in range(SG_GROUPS)]
    cw = cw_ref[...]
    carry = carry_ref[...]

    for s in range(tm // MIX_SUB_ROWS):
        r0 = s * MIX_SUB_ROWS
        rs = slice(r0, r0 + MIX_SUB_ROWS)
        h = h_ref[rs, :]
        a = _rms_norm(h, g_ref[...]).astype(BF16)
        zr = jnp.dot(a, wr_ref[...], preferred_element_type=F32)
        zx, zb, zc = zr[:, :c_b], zr[:, c_b:c_c], zr[:, c_c:c_u]

        u = zc * zx
        ext = jnp.concatenate([carry, u], axis=0)
        u1 = pltpu.roll(ext, 1, 0)[SUBLANES:]
        u2 = pltpu.roll(ext, 2, 0)[SUBLANES:]
        carry = u[MIX_SUB_ROWS - SUBLANES:]
        yb = zb * (cw[0:1] * u2 + cw[1:2] * u1 + cw[2:3] * u)

        us = _gelu(zr[:, c_u:c_v])
        vs = _gelu(zr[:, c_v:c_g])
        mu = jnp.mean(vs, axis=-1, keepdims=True)
        xc = vs - mu
        var = jnp.mean(xc * xc, axis=-1, keepdims=True)
        vln = (xc * lax.rsqrt(var + LN_EPS) * lng_ref[...] + lnb_ref[...]).astype(BF16)
        for g in range(SG_GROUPS):
            bias = sgb_ref[g]
            cs = slice(g * gch, (g + 1) * gch)
            for n in range(MIX_SUB_ROWS // SG_CHUNK):
                ns = slice(n * SG_CHUNK, (n + 1) * SG_CHUNK)
                sv = jnp.dot(w_sg[g], vln[ns, cs], preferred_element_type=F32) + bias
                yc_ref[r0 + n * SG_CHUNK:r0 + (n + 1) * SG_CHUNK, cs] = (us[ns, cs] * sv).astype(BF16)

        l0, l1, l2 = natural(1, rs), natural(3, rs), natural(5, rs)
        mx = jnp.maximum(jnp.maximum(l0, l1), l2)
        e0, e1, e2 = jnp.exp(l0 - mx), jnp.exp(l1 - mx), jnp.exp(l2 - mx)
        inv = 1.0 / (e0 + e1 + e2)
        ya = jnp.concatenate([natural(0, rs) * (e0 * inv), natural(2, rs) * (e1 * inv),
                              natural(4, rs) * (e2 * inv)], axis=1).astype(BF16)

        pa = jnp.dot(ya, wa_ref[...], preferred_element_type=F32)
        pb = jnp.dot(yb.astype(BF16), wb_ref[...], preferred_element_type=F32)
        pc = jnp.dot(yc_ref[rs, :], wc_ref[...], preferred_element_type=F32)
        m = (_sigmoid(zr[:, c_g:c_g + d_model]) * pa
             + _sigmoid(zr[:, c_g + d_model:c_g + 2 * d_model]) * pb
             + _sigmoid(zr[:, c_g + 2 * d_model:]) * pc)
        out_ref[rs, :] = h + jnp.dot(m.astype(BF16), wo_ref[...], preferred_element_type=F32)

    carry_ref[...] = carry


def _mixer_merge(layer, h2d, g, att, w_rest, conv_w, ln_g, ln_b, sg_w, sg_b, w_a, w_b, w_c, w_o, seq):
    n_tok, d_model = h2d.shape
    tm = MIX_ROWS
    conv_width = conv_w.shape[-1]
    sg_width = ln_g.shape[-1]
    row_tile = lambda width: pl.BlockSpec((tm, width), lambda i: (i, 0))
    (o0, l0), (o1, l1), (o2, l2) = att
    att_tiles = [pl.BlockSpec((tm // d, d * GROUP_WIDTH), lambda i: (i, 0)) for d in DILATIONS]
    kern = functools.partial(_mix_kernel, tiles_per_seq=seq // tm, conv_w=conv_width, sg_w=sg_width)
    return pl.pallas_call(
        kern,
        grid=(n_tok // tm,),
        in_specs=[row_tile(d_model), _resident(g, layer)]
        + att_tiles * 2
        + [_resident(x, layer) for x in (w_rest, conv_w, ln_g, ln_b, sg_w, sg_b, w_a, w_b, w_c, w_o)],
        out_specs=row_tile(d_model),
        out_shape=jax.ShapeDtypeStruct((n_tok, d_model), F32),
        scratch_shapes=[pltpu.VMEM((SUBLANES, conv_width), F32),
                        pltpu.VMEM((tm, sg_width), BF16),
                        pltpu.VMEM((2 * len(DILATIONS) * (GROUP_WIDTH // LANES), tm, LANES), F32)],
        compiler_params=pltpu.CompilerParams(
            dimension_semantics=("arbitrary",), vmem_limit_bytes=VMEM_LIMIT_BYTES),
        name="mixer_merge",
    )(h2d, g, o0, o1, o2, l0, l1, l2, w_rest, conv_w, ln_g, ln_b, sg_w, sg_b, w_a, w_b, w_c, w_o)


def _mlp_kernel(h_ref, gm_ref, wu_ref, wd_ref, p_ref, gp_ref, wg_ref, wp_ref, out_ref):
    d_ff = wu_ref.shape[1]
    for s in range(h_ref.shape[0] // MLP_SUB_ROWS):
        rs = slice(s * MLP_SUB_ROWS, (s + 1) * MLP_SUB_ROWS)
        h = h_ref[rs, :]
        c = _rms_norm(h, gm_ref[...]).astype(BF16)
        acc = h
        for f0 in range(0, d_ff, MLP_FF_TILE):
            t = jnp.dot(c, wu_ref[:, f0:f0 + MLP_FF_TILE], preferred_element_type=F32)
            t = jnp.square(jnp.maximum(t, 0.0)).astype(BF16)
            acc = acc + jnp.dot(t, wd_ref[f0:f0 + MLP_FF_TILE, :], preferred_element_type=F32)
        e = _rms_norm(acc, gp_ref[...]).astype(BF16)
        gate = _sigmoid(jnp.dot(e, wg_ref[...], preferred_element_type=F32))
        emb = jnp.dot(p_ref[rs, :].astype(BF16), wp_ref[...], preferred_element_type=F32)
        out_ref[rs, :] = acc + gate * emb


def _mlp_ple(layer, h2d, g_mlp, w_up, w_down, p3d, g_ple, w_pg, w_pe):
    n_tok, d_model = h2d.shape
    tm = MLP_ROWS
    return pl.pallas_call(
        _mlp_kernel,
        grid=(n_tok // tm,),
        in_specs=[
            pl.BlockSpec((tm, d_model), lambda i: (i, 0)),
            _resident(g_mlp, layer),
            _resident(w_up, layer),
            _resident(w_down, layer),
            pl.BlockSpec((None, tm, p3d.shape[-1]), lambda i: (layer, i, 0)),
            _resident(g_ple, layer),
            _resident(w_pg, layer),
            _resident(w_pe, layer),
        ],
        out_specs=pl.BlockSpec((tm, d_model), lambda i: (i, 0)),
        out_shape=jax.ShapeDtypeStruct((n_tok, d_model), F32),
        compiler_params=pltpu.CompilerParams(
            dimension_semantics=("arbitrary",), vmem_limit_bytes=VMEM_LIMIT_BYTES),
        name="mlp_ple",
    )(h2d, g_mlp, w_up, w_down, p3d, g_ple, w_pg, w_pe)


def _norm_kernel(h_ref, g_ref, out_ref):
    out_ref[...] = _rms_norm(h_ref[...], g_ref[...])


def _final_norm(h2d, g):
    n_tok, d_model = h2d.shape
    tm = NORM_ROWS
    return pl.pallas_call(
        _norm_kernel,
        grid=(n_tok // tm,),
        in_specs=[pl.BlockSpec((tm, d_model), lambda i: (i, 0)), _resident(g, 0)],
        out_specs=pl.BlockSpec((tm, d_model), lambda i: (i, 0)),
        out_shape=jax.ShapeDtypeStruct((n_tok, d_model), F32),
        compiler_params=pltpu.CompilerParams(dimension_semantics=("arbitrary",)),
        name="final_norm",
    )(h2d, g)


def _rotary_tables(positions):
    inv_freq = ROPE_THETA ** (-(jnp.arange(0, 2 * ROT_HALF, 2, dtype=F32) / (2 * ROT_HALF)))
    ang = positions.astype(F32).reshape(-1, 1) * inv_freq
    cos, sin = jnp.cos(ang), jnp.sin(ang)
    rest = HEAD_DIM - 2 * ROT_HALF
    cos_h = jnp.concatenate([cos, cos, jnp.ones((ang.shape[0], rest), F32)], axis=1)
    sin_h = jnp.concatenate([-sin, sin, jnp.zeros((ang.shape[0], rest), F32)], axis=1)
    reps = LANES // HEAD_DIM
    return jnp.tile(cos_h, (1, reps)), jnp.tile(sin_h, (1, reps))


def kernel(x, p, positions, norm_mix_g, w_in, conv_w, sg_ln_g, sg_ln_b, sg_w, sg_b,
           w_branch_a, w_branch_b, w_branch_c, w_out, norm_mlp_g, w_up, w_down,
           norm_ple_g, w_ple_gate, w_ple_proj, norm_final_g):
    batch, seq, d_model = x.shape
    depth = w_in.shape[0]
    n_tok = batch * seq
    qkv_width = 3 * len(DILATIONS) * GROUP_WIDTH
    assert seq % (max(DILATIONS) * ATTN_BLOCK) == 0 and seq % ATTN_ROWS_MAX == 0
    assert n_tok % NORM_ROWS == 0 and seq % MIX_ROWS == 0

    cos_t, sin_t = _rotary_tables(positions)
    rows = lambda v: v.reshape(v.shape[0], 1, v.shape[-1])
    bf = lambda w: w.astype(BF16)
    w_qkv = bf(w_in[:, :, :qkv_width])
    w_rest = bf(w_in[:, :, qkv_width:])
    sg_bias = jnp.broadcast_to(sg_b[:, :, :, None], sg_w.shape)
    w_a, w_b, w_c, w_o = bf(w_branch_a), bf(w_branch_b), bf(w_branch_c), bf(w_out)
    w_u, w_d, w_pg, w_pe = bf(w_up), bf(w_down), bf(w_ple_gate), bf(w_ple_proj)
    g_mix, g_mlp, g_ple = rows(norm_mix_g), rows(norm_mlp_g), rows(norm_ple_g)
    ln_g, ln_b = rows(sg_ln_g), rows(sg_ln_b)
    p3d = p.reshape(depth, n_tok, p.shape[-1])

    h = x.reshape(n_tok, d_model)
    for i in range(depth):
        qkv = _qkv_projection(i, h, g_mix, w_qkv, cos_t, sin_t)
        att = [_attention_group(qkv_g, d, batch, seq) for qkv_g, d in zip(qkv, DILATIONS)]
        h = _mixer_merge(i, h, g_mix, att, w_rest, conv_w, ln_g, ln_b, sg_w, sg_bias,
                         w_a, w_b, w_c, w_o, seq)
        h = _mlp_ple(i, h, g_mlp, w_u, w_d, p3d, g_ple, w_pg, w_pe)
    return _final_norm(h, norm_final_g.reshape(1, 1, -1)).reshape(batch, seq, d_model)
```

```python
import functools
import math

import jax
import jax.numpy as jnp
from jax import lax
from jax.experimental import pallas as pl
from jax.experimental.pallas import tpu as pltpu

F32 = jnp.float32
BF16 = jnp.bfloat16

HEAD_DIM = 64
HEADS_PER_GROUP = 4
GROUP_WIDTH = HEADS_PER_GROUP * HEAD_DIM
DILATIONS = (1, 4, 16)
ATTN_BLOCK = 128
ROT_HALF = 8
ROPE_THETA = 500000.0
CONV_K = 3
SG_CHUNK = 128
SG_GROUPS = 4
RMS_EPS = 1e-6
LN_EPS = 1e-5

LANES = 128
SUBLANES = 8
VMEM_LIMIT_BYTES = 56 * 1024 * 1024

QKV_ROWS = 512
ATTN_ROWS_MAX = 1024
MIX_ROWS = 512
MIX_SUB_ROWS = 256
MLP_ROWS = 512
MLP_SUB_ROWS = 256
MLP_FF_TILE = 1024


def _rms_norm(h, g):
    ms = jnp.mean(h * h, axis=-1, keepdims=True)
    return h * lax.rsqrt(ms + RMS_EPS) * g


def _sigmoid(x):
    return 1.0 / (1.0 + jnp.exp(-x))


def _gelu(x):
    return 0.5 * x * (1.0 + lax.erf(x * (1.0 / math.sqrt(2.0))))


def _resident(stacked, layer):
    shape = stacked.shape[1:]
    index = (layer,) + (0,) * len(shape)
    return pl.BlockSpec((None,) + shape, lambda *_: index, pipeline_mode=pl.Buffered(1))


def _qkv_kernel(h_ref, g_ref, w_ref, cos_ref, sin_ref, *refs):
    outs, slab_ref = refs[:-1], refs[-1]
    n_groups = len(outs)
    tm = h_ref.shape[0]
    a = _rms_norm(h_ref[...], g_ref[...]).astype(BF16)
    z = jnp.dot(a, w_ref[...], preferred_element_type=F32)
    cos_t = cos_ref[...]
    sin_t = sin_ref[...]
    lane = lax.broadcasted_iota(jnp.int32, cos_t.shape, 1)
    low_half = (lane % HEAD_DIM) < ROT_HALF
    for which in range(3):
        for g, d in enumerate(DILATIONS):
            for half in range(GROUP_WIDTH // LANES):
                c0 = (which * n_groups + g) * GROUP_WIDTH + half * LANES
                t = z[:, c0:c0 + LANES]
                if which < 2:
                    partner = jnp.where(low_half,
                                        pltpu.roll(t, LANES - ROT_HALF, 1),
                                        pltpu.roll(t, ROT_HALF, 1))
                    t = t * cos_t + partner * sin_t
                if d == 1:
                    outs[g][which, :, half * LANES:(half + 1) * LANES] = t
                else:
                    slab_ref[...] = t
                    for r in range(d):
                        l0 = r * GROUP_WIDTH + half * LANES
                        outs[g][which, :, l0:l0 + LANES] = slab_ref[pl.ds(r, tm // d, stride=d), :]


def _qkv_projection(layer, h2d, g, w_in, cos_t, sin_t):
    n_tok, d_model = h2d.shape
    tm = QKV_ROWS
    qkv_width = 3 * len(DILATIONS) * GROUP_WIDTH
    return pl.pallas_call(
        _qkv_kernel,
        grid=(n_tok // tm,),
        in_specs=[
            pl.BlockSpec((tm, d_model), lambda i: (i, 0)),
            _resident(g, layer),
            pl.BlockSpec((None, d_model, qkv_width), lambda i: (layer, 0, 0),
                         pipeline_mode=pl.Buffered(1)),
            pl.BlockSpec((tm, LANES), lambda i: (i, 0)),
            pl.BlockSpec((tm, LANES), lambda i: (i, 0)),
        ],
        out_specs=[pl.BlockSpec((3, tm // d, d * GROUP_WIDTH), lambda i: (0, i, 0))
                   for d in DILATIONS],
        out_shape=[jax.ShapeDtypeStruct((3, n_tok // d, d * GROUP_WIDTH), F32)
                   for d in DILATIONS],
        scratch_shapes=[pltpu.VMEM((tm, LANES), F32)],
        compiler_params=pltpu.CompilerParams(
            dimension_semantics=("arbitrary",), vmem_limit_bytes=VMEM_LIMIT_BYTES),
        name="qkv_proj",
    )(h2d, g, w_in, cos_t, sin_t)


def _attn_kernel(*refs, nblk, has_prev):
    if has_prev:
        q_ref, k_ref, kp_ref, v_ref, vp_ref, o_ref, l_ref, ks_ref, vs_ref = refs
        ks_ref[0:ATTN_BLOCK] = kp_ref[...].astype(BF16)
        vs_ref[0:ATTN_BLOCK] = vp_ref[...].astype(BF16)
        first_step = pl.program_id(2) == 0
    else:
        q_ref, k_ref, v_ref, o_ref, l_ref, ks_ref, vs_ref = refs
        ks_ref[0:ATTN_BLOCK] = jnp.zeros((ATTN_BLOCK, ks_ref.shape[1]), BF16)
        vs_ref[0:ATTN_BLOCK] = jnp.zeros((ATTN_BLOCK, vs_ref.shape[1]), BF16)
        first_step = True
    ks_ref[ATTN_BLOCK:] = k_ref[...].astype(BF16)
    vs_ref[ATTN_BLOCK:] = v_ref[...].astype(BF16)

    two = 2 * ATTN_BLOCK
    qi = lax.broadcasted_iota(jnp.int32, (two, two), 0) % ATTN_BLOCK
    kj = lax.broadcasted_iota(jnp.int32, (two, two), 1)
    dist = qi + ATTN_BLOCK - kj
    band = (dist >= 0) & (dist <= ATTN_BLOCK)
    band_first = band & ((kj >= ATTN_BLOCK) | jnp.logical_not(first_step))
    lane = lax.broadcasted_iota(jnp.int32, (ATTN_BLOCK, LANES), 1)
    head0 = lane < HEAD_DIM

    for i in range(nblk):
        valid = band_first if i == 0 else band
        r0 = i * ATTN_BLOCK
        for p in range(q_ref.shape[1] // LANES):
            c0 = p * LANES
            q2 = q_ref[r0:r0 + ATTN_BLOCK, c0:c0 + LANES] * (HEAD_DIM ** -0.5)
            qs = jnp.concatenate([jnp.where(head0, q2, 0.0), jnp.where(head0, 0.0, q2)],
                                 axis=0).astype(BF16)
            kk = ks_ref[r0:r0 + two, c0:c0 + LANES]
            s = lax.dot_general(qs, kk, (((1,), (1,)), ((), ())),
                                preferred_element_type=F32)
            s = jnp.where(valid, s, -jnp.inf)
            m = jnp.max(s, axis=-1, keepdims=True)
            e = jnp.exp(s - m)
            den = jnp.sum(e, axis=-1, keepdims=True)
            vv = vs_ref[r0:r0 + two, c0:c0 + LANES]
            pv = jnp.dot(e.astype(BF16), vv, preferred_element_type=F32)
            pv = pv * (1.0 / den)
            lse = m + jnp.log(den)
            o_ref[r0:r0 + ATTN_BLOCK, c0:c0 + LANES] = jnp.where(
                head0, pv[:ATTN_BLOCK], pv[ATTN_BLOCK:])
            l_ref[r0:r0 + ATTN_BLOCK, c0:c0 + LANES] = jnp.where(
                head0, lse[:ATTN_BLOCK], lse[ATTN_BLOCK:])


def _attention_group(qkv, dilation, batch, seq):
    m_len = seq // dilation
    mb = min(m_len, ATTN_ROWS_MAX)
    nblk = mb // ATTN_BLOCK
    n_res = min(dilation, ATTN_ROWS_MAX // mb)
    width = n_res * GROUP_WIDTH
    has_prev = mb < m_len
    view = qkv.reshape(3, batch, m_len, dilation * GROUP_WIDTH)

    def rows(which):
        return pl.BlockSpec((None, None, mb, width), lambda b, r, n: (which, b, n, r))

    def prev(which):
        return pl.BlockSpec((None, None, ATTN_BLOCK, width),
                            lambda b, r, n: (which, b, jnp.maximum(n * nblk - 1, 0), r))

    if has_prev:
        in_specs = [rows(0), rows(1), prev(1), rows(2), prev(2)]
    else:
        in_specs = [rows(0), rows(1), rows(2)]
    out_spec = pl.BlockSpec((None, mb, width), lambda b, r, n: (b, n, r))
    out_sds = jax.ShapeDtypeStruct((batch, m_len, dilation * GROUP_WIDTH), F32)
    o, l = pl.pallas_call(
        functools.partial(_attn_kernel, nblk=nblk, has_prev=has_prev),
        grid=(batch, dilation // n_res, m_len // mb),
        in_specs=in_specs,
        out_specs=[out_spec, out_spec],
        out_shape=[out_sds, out_sds],
        scratch_shapes=[pltpu.VMEM((mb + ATTN_BLOCK, width), BF16),
                        pltpu.VMEM((mb + ATTN_BLOCK, width), BF16)],
        compiler_params=pltpu.CompilerParams(
            dimension_semantics=("arbitrary", "arbitrary", "arbitrary"),
            vmem_limit_bytes=VMEM_LIMIT_BYTES),
        name=f"attn_d{dilation}",
    )(*([view] * len(in_specs)))
    rows2d = batch * m_len
    return (o.reshape(rows2d, dilation * GROUP_WIDTH), l.reshape(rows2d, dilation * GROUP_WIDTH))


def _mix_kernel(h_ref, g_ref, o0_ref, o1_ref, o2_ref, l0_ref, l1_ref, l2_ref,
                wr_ref, cw_ref, lng_ref, lnb_ref, sgw_ref, sgb_ref,
                wa_ref, wb_ref, wc_ref, wo_ref, out_ref, carry_ref, yc_ref, nat_ref,
                *, tiles_per_seq, conv_w, sg_w):
    tm, d_model = h_ref.shape
    halves = GROUP_WIDTH // LANES
    w_skip = 3 * len(DILATIONS) * GROUP_WIDTH

    @pl.when(pl.program_id(0) % tiles_per_seq == 0)
    def _():
        carry_ref[...] = jnp.zeros_like(carry_ref)

    att_refs = (o0_ref, l0_ref, o1_ref, l1_ref, o2_ref, l2_ref)
    for slot, src_ref in enumerate(att_refs):
        d = DILATIONS[slot // 2]
        for r in range(d if d > 1 else 0):
            for half in range(halves):
                l0 = r * GROUP_WIDTH + half * LANES
                nat_ref[slot * halves + half, pl.ds(r, tm // d, stride=d), :] = src_ref[:, l0:l0 + LANES]

    def natural(slot, rs):
        if DILATIONS[slot // 2] == 1:
            return att_refs[slot][rs, :]
        return jnp.concatenate([nat_ref[slot * halves + half, rs, :] for half in range(halves)], axis=1)

    c_b, c_c, c_u, c_v, c_g = conv_w, 2 * conv_w, 3 * conv_w, 3 * conv_w + sg_w, 3 * conv_w + 2 * sg_w
    ti = lax.broadcasted_iota(jnp.int32, (SG_CHUNK, SG_CHUNK), 0)
    si = lax.broadcasted_iota(jnp.int32, (SG_CHUNK, SG_CHUNK), 1)
    gch = sg_w // SG_GROUPS
    w_sg = [jnp.where(ti >= si, sgw_ref[g], 0.0).astype(BF16) for g in range(SG_GROUPS)]
    cw = cw_ref[...]
    carry = carry_ref[...]

    for s in range(tm // MIX_SUB_ROWS):
        r0 = s * MIX_SUB_ROWS
        rs = slice(r0, r0 + MIX_SUB_ROWS)
        h = h_ref[rs, :]
        a = _rms_norm(h, g_ref[...]).astype(BF16)
        zr = jnp.dot(a, wr_ref[:, w_skip:], preferred_element_type=F32)
        zx, zb, zc = zr[:, :c_b], zr[:, c_b:c_c], zr[:, c_c:c_u]

        u = zc * zx
        ext = jnp.concatenate([carry, u], axis=0)
        u1 = pltpu.roll(ext, 1, 0)[SUBLANES:]
        u2 = pltpu.roll(ext, 2, 0)[SUBLANES:]
        carry = u[MIX_SUB_ROWS - SUBLANES:]
        yb = zb * (cw[0:1] * u2 + cw[1:2] * u1 + cw[2:3] * u)

        us = _gelu(zr[:, c_u:c_v])
        vs = _gelu(zr[:, c_v:c_g])
        mu = jnp.mean(vs, axis=-1, keepdims=True)
        xc = vs - mu
        var = jnp.mean(xc * xc, axis=-1, keepdims=True)
        vln = (xc * lax.rsqrt(var + LN_EPS) * lng_ref[...] + lnb_ref[...]).astype(BF16)
        for g in range(SG_GROUPS):
            bias = sgb_ref[g]
            cs = slice(g * gch, (g + 1) * gch)
            for n in range(MIX_SUB_ROWS // SG_CHUNK):
                ns = slice(n * SG_CHUNK, (n + 1) * SG_CHUNK)
                sv = jnp.dot(w_sg[g], vln[ns, cs], preferred_element_type=F32) + bias
                yc_ref[r0 + n * SG_CHUNK:r0 + (n + 1) * SG_CHUNK, cs] = (us[ns, cs] * sv).astype(BF16)

        l0, l1, l2 = natural(1, rs), natural(3, rs), natural(5, rs)
        mx = jnp.maximum(jnp.maximum(l0, l1), l2)
        e0, e1, e2 = jnp.exp(l0 - mx), jnp.exp(l1 - mx), jnp.exp(l2 - mx)
        inv = 1.0 / (e0 + e1 + e2)
        ya = jnp.concatenate([natural(0, rs) * (e0 * inv), natural(2, rs) * (e1 * inv),
                              natural(4, rs) * (e2 * inv)], axis=1).astype(BF16)

        pa = jnp.dot(ya, wa_ref[...], preferred_element_type=F32)
        pb = jnp.dot(yb.astype(BF16), wb_ref[...], preferred_element_type=F32)
        pc = jnp.dot(yc_ref[rs, :], wc_ref[...], preferred_element_type=F32)
        m = (_sigmoid(zr[:, c_g:c_g + d_model]) * pa
             + _sigmoid(zr[:, c_g + d_model:c_g + 2 * d_model]) * pb
             + _sigmoid(zr[:, c_g + 2 * d_model:]) * pc)
        out_ref[rs, :] = h + jnp.dot(m.astype(BF16), wo_ref[...], preferred_element_type=F32)

    carry_ref[...] = carry


def _mixer_merge(layer, h2d, g, att, w_in, conv_w, ln_g, ln_b, sg_w, sg_b, w_a, w_b, w_c, w_o, seq):
    n_tok, d_model = h2d.shape
    tm = MIX_ROWS
    conv_width = conv_w.shape[-1]
    sg_width = ln_g.shape[-1]
    row_tile = lambda width: pl.BlockSpec((tm, width), lambda i: (i, 0))
    (o0, l0), (o1, l1), (o2, l2) = att
    att_tiles = [pl.BlockSpec((tm // d, d * GROUP_WIDTH), lambda i: (i, 0)) for d in DILATIONS]
    kern = functools.partial(_mix_kernel, tiles_per_seq=seq // tm, conv_w=conv_width, sg_w=sg_width)
    return pl.pallas_call(
        kern,
        grid=(n_tok // tm,),
        in_specs=[row_tile(d_model), _resident(g, layer)]
        + att_tiles * 2
        + [_resident(x, layer) for x in (w_in, conv_w, ln_g, ln_b, sg_w, sg_b, w_a, w_b, w_c, w_o)],
        out_specs=row_tile(d_model),
        out_shape=jax.ShapeDtypeStruct((n_tok, d_model), F32),
        scratch_shapes=[pltpu.VMEM((SUBLANES, conv_width), F32),
                        pltpu.VMEM((tm, sg_width), BF16),
                        pltpu.VMEM((2 * len(DILATIONS) * (GROUP_WIDTH // LANES), tm, LANES), F32)],
        compiler_params=pltpu.CompilerParams(
            dimension_semantics=("arbitrary",), vmem_limit_bytes=VMEM_LIMIT_BYTES),
        name="mixer_merge",
    )(h2d, g, o0, o1, o2, l0, l1, l2, w_in, conv_w, ln_g, ln_b, sg_w, sg_b, w_a, w_b, w_c, w_o)


def _mlp_kernel(h_ref, gm_ref, wu_ref, wd_ref, p_ref, gp_ref, wg_ref, wp_ref, gf_ref, out_ref,
                *, final_norm):
    d_ff = wu_ref.shape[1]
    for s in range(h_ref.shape[0] // MLP_SUB_ROWS):
        rs = slice(s * MLP_SUB_ROWS, (s + 1) * MLP_SUB_ROWS)
        h = h_ref[rs, :]
        c = _rms_norm(h, gm_ref[...]).astype(BF16)
        acc = h
        for f0 in range(0, d_ff, MLP_FF_TILE):
            t = jnp.dot(c, wu_ref[:, f0:f0 + MLP_FF_TILE], preferred_element_type=F32)
            t = jnp.square(jnp.maximum(t, 0.0)).astype(BF16)
            acc = acc + jnp.dot(t, wd_ref[f0:f0 + MLP_FF_TILE, :], preferred_element_type=F32)
        e = _rms_norm(acc, gp_ref[...]).astype(BF16)
        gate = _sigmoid(jnp.dot(e, wg_ref[...], preferred_element_type=F32))
        emb = jnp.dot(p_ref[rs, :].astype(BF16), wp_ref[...], preferred_element_type=F32)
        out = acc + gate * emb
        out_ref[rs, :] = _rms_norm(out, gf_ref[...]) if final_norm else out


def _mlp_ple(layer, h2d, g_mlp, w_up, w_down, p3d, g_ple, w_pg, w_pe, g_final, final_norm):
    n_tok, d_model = h2d.shape
    tm = MLP_ROWS
    return pl.pallas_call(
        functools.partial(_mlp_kernel, final_norm=final_norm),
        grid=(n_tok // tm,),
        in_specs=[
            pl.BlockSpec((tm, d_model), lambda i: (i, 0)),
            _resident(g_mlp, layer),
            _resident(w_up, layer),
            _resident(w_down, layer),
            pl.BlockSpec((None, tm, p3d.shape[-1]), lambda i: (layer, i, 0)),
            _resident(g_ple, layer),
            _resident(w_pg, layer),
            _resident(w_pe, layer),
            _resident(g_final, 0),
        ],
        out_specs=pl.BlockSpec((tm, d_model), lambda i: (i, 0)),
        out_shape=jax.ShapeDtypeStruct((n_tok, d_model), F32),
        compiler_params=pltpu.CompilerParams(
            dimension_semantics=("arbitrary",), vmem_limit_bytes=VMEM_LIMIT_BYTES),
        name="mlp_ple",
    )(h2d, g_mlp, w_up, w_down, p3d, g_ple, w_pg, w_pe, g_final)


def _rotary_tables(positions):
    inv_freq = ROPE_THETA ** (-(jnp.arange(0, 2 * ROT_HALF, 2, dtype=F32) / (2 * ROT_HALF)))
    in_head = jnp.arange(LANES) % HEAD_DIM
    rotated = in_head < 2 * ROT_HALF
    freq = jnp.where(rotated, inv_freq[in_head % ROT_HALF], 0.0)
    sign = jnp.where(in_head < ROT_HALF, -1.0, 1.0).astype(F32)
    ang = positions.astype(F32).reshape(-1, 1) * freq
    return jnp.cos(ang), sign * jnp.sin(ang)


def kernel(x, p, positions, norm_mix_g, w_in, conv_w, sg_ln_g, sg_ln_b, sg_w, sg_b,
           w_branch_a, w_branch_b, w_branch_c, w_out, norm_mlp_g, w_up, w_down,
           norm_ple_g, w_ple_gate, w_ple_proj, norm_final_g):
    batch, seq, d_model = x.shape
    depth = w_in.shape[0]
    n_tok = batch * seq
    qkv_width = 3 * len(DILATIONS) * GROUP_WIDTH
    assert seq % (max(DILATIONS) * ATTN_BLOCK) == 0 and seq % ATTN_ROWS_MAX == 0
    assert n_tok % MLP_ROWS == 0 and seq % MIX_ROWS == 0

    cos_t, sin_t = _rotary_tables(positions)
    rows = lambda v: v.reshape(v.shape[0], 1, v.shape[-1])
    bf = lambda w: w.astype(BF16)
    w_in_b = bf(w_in)
    sg_bias = jnp.broadcast_to(sg_b[:, :, :, None], sg_w.shape)
    w_a, w_b, w_c, w_o = bf(w_branch_a), bf(w_branch_b), bf(w_branch_c), bf(w_out)
    w_u, w_d, w_pg, w_pe = bf(w_up), bf(w_down), bf(w_ple_gate), bf(w_ple_proj)
    g_mix, g_mlp, g_ple = rows(norm_mix_g), rows(norm_mlp_g), rows(norm_ple_g)
    ln_g, ln_b = rows(sg_ln_g), rows(sg_ln_b)
    p3d = p.reshape(depth, n_tok, p.shape[-1])

    g_final = norm_final_g.reshape(1, 1, -1)

    h = x.reshape(n_tok, d_model)
    for i in range(depth):
        qkv = _qkv_projection(i, h, g_mix, w_in_b, cos_t, sin_t)
        att = [_attention_group(qkv_g, d, batch, seq) for qkv_g, d in zip(qkv, DILATIONS)]
        h = _mixer_merge(i, h, g_mix, att, w_in_b, conv_w, ln_g, ln_b, sg_w, sg_bias,
                         w_a, w_b, w_c, w_o, seq)
        h = _mlp_ple(i, h, g_mlp, w_u, w_d, p3d, g_ple, w_pg, w_pe, g_final,
                     final_norm=(i == depth - 1))
    return h.reshape(batch, seq, d_model)
```

```python
import functools
import math

import jax
import jax.numpy as jnp
from jax import lax
from jax.experimental import pallas as pl
from jax.experimental.pallas import tpu as pltpu

F32 = jnp.float32
BF16 = jnp.bfloat16

HEAD_DIM = 64
HEADS_PER_GROUP = 4
GROUP_WIDTH = HEADS_PER_GROUP * HEAD_DIM
DILATIONS = (1, 4, 16)
ATTN_BLOCK = 128
ROT_HALF = 8
ROPE_THETA = 500000.0
CONV_K = 3
SG_CHUNK = 128
SG_GROUPS = 4
RMS_EPS = 1e-6
LN_EPS = 1e-5

LANES = 128
SUBLANES = 8
VMEM_LIMIT_BYTES = 56 * 1024 * 1024

QKV_ROWS = 512
ATTN_ROWS_MAX = 1024
MIX_ROWS = 512
MIX_SUB_ROWS = 256
MLP_ROWS = 512
MLP_SUB_ROWS = 256
MLP_FF_TILE = 1024


def _rms_norm(h, g):
    ms = jnp.mean(h * h, axis=-1, keepdims=True)
    return h * lax.rsqrt(ms + RMS_EPS) * g


def _sigmoid(x):
    return 1.0 / (1.0 + jnp.exp(-x))


def _gelu(x):
    return 0.5 * x * (1.0 + lax.erf(x * (1.0 / math.sqrt(2.0))))


def _resident(stacked, layer):
    shape = stacked.shape[1:]
    index = (layer,) + (0,) * len(shape)
    return pl.BlockSpec((None,) + shape, lambda *_: index, pipeline_mode=pl.Buffered(1))


def _qkv_kernel(h_ref, g_ref, w_ref, cos_ref, sin_ref, *refs):
    outs, slab_ref = refs[:-1], refs[-1]
    n_groups = len(outs)
    tm = h_ref.shape[0]
    a = _rms_norm(h_ref[...], g_ref[...]).astype(BF16)
    cos_t = cos_ref[...]
    sin_t = sin_ref[...]
    lane = lax.broadcasted_iota(jnp.int32, cos_t.shape, 1)
    low_half = (lane % HEAD_DIM) < ROT_HALF
    slab = 0
    for which in range(3):
        for g, d in enumerate(DILATIONS):
            c0 = (which * n_groups + g) * GROUP_WIDTH
            z = jnp.dot(a, w_ref[:, c0:c0 + GROUP_WIDTH], preferred_element_type=F32)
            for half in range(GROUP_WIDTH // LANES):
                t = z[:, half * LANES:(half + 1) * LANES]
                if which < 2:
                    partner = jnp.where(low_half,
                                        pltpu.roll(t, LANES - ROT_HALF, 1),
                                        pltpu.roll(t, ROT_HALF, 1))
                    t = t * cos_t + partner * sin_t
                if d == 1:
                    outs[g][which, :, half * LANES:(half + 1) * LANES] = t.astype(BF16)
                else:
                    slab_ref[slab] = t
                    for r in range(d):
                        l0 = r * GROUP_WIDTH + half * LANES
                        rows = slab_ref[slab, pl.ds(r, tm // d, stride=d), :]
                        outs[g][which, :, l0:l0 + LANES] = rows.astype(BF16)
                    slab += 1


def _qkv_projection(layer, h2d, g, w_in, cos_t, sin_t):
    n_tok, d_model = h2d.shape
    tm = QKV_ROWS
    qkv_width = 3 * len(DILATIONS) * GROUP_WIDTH
    n_slabs = 3 * (GROUP_WIDTH // LANES) * sum(d > 1 for d in DILATIONS)
    return pl.pallas_call(
        _qkv_kernel,
        grid=(n_tok // tm,),
        in_specs=[
            pl.BlockSpec((tm, d_model), lambda i: (i, 0)),
            _resident(g, layer),
            pl.BlockSpec((None, d_model, qkv_width), lambda i: (layer, 0, 0),
                         pipeline_mode=pl.Buffered(1)),
            pl.BlockSpec((tm, LANES), lambda i: (i, 0)),
            pl.BlockSpec((tm, LANES), lambda i: (i, 0)),
        ],
        out_specs=[pl.BlockSpec((3, tm // d, d * GROUP_WIDTH), lambda i: (0, i, 0))
                   for d in DILATIONS],
        out_shape=[jax.ShapeDtypeStruct((3, n_tok // d, d * GROUP_WIDTH), BF16)
                   for d in DILATIONS],
        scratch_shapes=[pltpu.VMEM((n_slabs, tm, LANES), F32)],
        compiler_params=pltpu.CompilerParams(
            dimension_semantics=("arbitrary",), vmem_limit_bytes=VMEM_LIMIT_BYTES),
        name="qkv_proj",
    )(h2d, g, w_in, cos_t, sin_t)


def _attn_kernel(*refs, nblk, has_prev):
    if has_prev:
        q_ref, k_ref, kp_ref, v_ref, vp_ref, o_ref, l_ref = refs
        first_step = pl.program_id(2) == 0
    else:
        q_ref, k_ref, v_ref, o_ref, l_ref = refs
        kp_ref = vp_ref = None
        first_step = True

    def window(ref, prev_ref, r0, c0):
        if r0 > 0:
            return ref[r0 - ATTN_BLOCK:r0 + ATTN_BLOCK, c0:c0 + LANES]
        cur = ref[0:ATTN_BLOCK, c0:c0 + LANES]
        before = jnp.zeros_like(cur) if prev_ref is None else prev_ref[:, c0:c0 + LANES]
        return jnp.concatenate([before, cur], axis=0)

    two = 2 * ATTN_BLOCK
    qi = lax.broadcasted_iota(jnp.int32, (two, two), 0) % ATTN_BLOCK
    kj = lax.broadcasted_iota(jnp.int32, (two, two), 1)
    dist = qi + ATTN_BLOCK - kj
    band = (dist >= 0) & (dist <= ATTN_BLOCK)
    band_first = band & ((kj >= ATTN_BLOCK) | jnp.logical_not(first_step))
    mask_rest = jnp.where(band, 0.0, -jnp.inf)
    mask_first = jnp.where(band_first, 0.0, -jnp.inf)
    lane =lax.broadcasted_iota(jnp.int32, (ATTN_BLOCK, LANES), 1)
    head0 = lane < HEAD_DIM
    scale = HEAD_DIM ** -0.5
    keep0 = jnp.where(head0, scale, 0.0).astype(BF16)
    keep1 = jnp.where(head0, 0.0, scale).astype(BF16)

    for i in range(nblk):
        mask = mask_first if i == 0 else mask_rest
        r0 = i * ATTN_BLOCK
        for p in range(q_ref.shape[1] // LANES):
            c0 = p * LANES
            q2 = q_ref[r0:r0 + ATTN_BLOCK, c0:c0 + LANES]
            qs = jnp.concatenate([q2 * keep0, q2 * keep1], axis=0)
            kk = window(k_ref, kp_ref, r0, c0)
            s = lax.dot_general(qs, kk, (((1,), (1,)), ((), ())),
                                preferred_element_type=F32)
            s = s + mask
            m = jnp.max(s, axis=-1, keepdims=True)
            e = jnp.exp(s - m)
            den = jnp.sum(e, axis=-1, keepdims=True)
            vv = window(v_ref, vp_ref, r0, c0)
            pv = jnp.dot(e.astype(BF16), vv, preferred_element_type=F32)
            pv = pv * (1.0 / den)
            lse = m + jnp.log(den)
            o_ref[r0:r0 + ATTN_BLOCK, c0:c0 + LANES] = jnp.where(
                head0, pv[:ATTN_BLOCK], pv[ATTN_BLOCK:])
            l_ref[r0:r0 + ATTN_BLOCK, c0:c0 + LANES] = jnp.where(
                head0, lse[:ATTN_BLOCK], lse[ATTN_BLOCK:])


def _attention_group(qkv, dilation, batch, seq):
    m_len = seq // dilation
    mb = min(m_len, ATTN_ROWS_MAX)
    nblk = mb // ATTN_BLOCK
    n_res = min(dilation, ATTN_ROWS_MAX // mb)
    width = n_res * GROUP_WIDTH
    has_prev = mb < m_len
    view = qkv.reshape(3, batch, m_len, dilation * GROUP_WIDTH)

    def rows(which):
        return pl.BlockSpec((None, None, mb, width), lambda b, r, n: (which, b, n, r))

    def prev(which):
        return pl.BlockSpec((None, None, ATTN_BLOCK, width),
                            lambda b, r, n: (which, b, jnp.maximum(n * nblk - 1, 0), r))

    if has_prev:
        in_specs = [rows(0), rows(1), prev(1), rows(2), prev(2)]
    else:
        in_specs = [rows(0), rows(1), rows(2)]
    out_spec = pl.BlockSpec((None, mb, width), lambda b, r, n: (b, n, r))
    out_sds = jax.ShapeDtypeStruct((batch, m_len, dilation * GROUP_WIDTH), F32)
    o, l = pl.pallas_call(
        functools.partial(_attn_kernel, nblk=nblk, has_prev=has_prev),
        grid=(batch, dilation // n_res, m_len // mb),
        in_specs=in_specs,
        out_specs=[out_spec, out_spec],
        out_shape=[out_sds, out_sds],
        compiler_params=pltpu.CompilerParams(
            dimension_semantics=("arbitrary", "arbitrary", "arbitrary"),
            vmem_limit_bytes=VMEM_LIMIT_BYTES),
        name=f"attn_d{dilation}",
    )(*([view] * len(in_specs)))
    rows2d = batch * m_len
    return (o.reshape(rows2d, dilation * GROUP_WIDTH), l.reshape(rows2d, dilation * GROUP_WIDTH))


def _mix_kernel(h_ref, g_ref, o0_ref, o1_ref, o2_ref, l0_ref, l1_ref, l2_ref,
                wr_ref, cw_ref, lng_ref, lnb_ref, sgw_ref, sgb_ref,
                wa_ref, wb_ref, wc_ref, wo_ref, out_ref, carry_ref, yc_ref, nat_ref,
                *, tiles_per_seq, conv_w, sg_w):
    tm, d_model = h_ref.shape
    halves = GROUP_WIDTH // LANES
    w_skip = 3 * len(DILATIONS) * GROUP_WIDTH

    @pl.when(pl.program_id(0) % tiles_per_seq == 0)
    def _():
        carry_ref[...] = jnp.zeros_like(carry_ref)

    att_refs = (o0_ref, l0_ref, o1_ref, l1_ref, o2_ref, l2_ref)
    for slot, src_ref in enumerate(att_refs):
        d = DILATIONS[slot // 2]
        for r in range(d if d > 1 else 0):
            for half in range(halves):
                l0 = r * GROUP_WIDTH + half * LANES
                nat_ref[slot * halves + half, pl.ds(r, tm // d, stride=d), :] = src_ref[:, l0:l0 + LANES]

    def natural(slot, rs):
        if DILATIONS[slot // 2] == 1:
            return att_refs[slot][rs, :]
        return jnp.concatenate([nat_ref[slot * halves + half, rs, :] for half in range(halves)], axis=1)

    c_b, c_c, c_u, c_v, c_g = conv_w, 2 * conv_w, 3 * conv_w, 3 * conv_w + sg_w, 3 * conv_w + 2 * sg_w
    ti = lax.broadcasted_iota(jnp.int32, (SG_CHUNK, SG_CHUNK), 0)
    si = lax.broadcasted_iota(jnp.int32, (SG_CHUNK, SG_CHUNK), 1)
    gch = sg_w // SG_GROUPS
    w_sg = [jnp.where(ti >= si, sgw_ref[g], 0.0).astype(BF16) for g in range(SG_GROUPS)]
    cw = cw_ref[...]
    carry = carry_ref[...]

    for s in range(tm // MIX_SUB_ROWS):
        r0 = s * MIX_SUB_ROWS
        rs = slice(r0, r0 + MIX_SUB_ROWS)
        h = h_ref[rs, :]
        a = _rms_norm(h, g_ref[...]).astype(BF16)

        def project(c0, c1):
            return jnp.dot(a, wr_ref[:, w_skip + c0:w_skip + c1], preferred_element_type=F32)

        z_conv = project(0, c_u)
        z_sg = project(c_u, c_g)

        zx, zb, zc = z_conv[:, :c_b], z_conv[:, c_b:c_c], z_conv[:, c_c:c_u]
        u = zc * zx
        ext = jnp.concatenate([carry, u], axis=0)
        u1 = pltpu.roll(ext, 1, 0)[SUBLANES:]
        u2 = pltpu.roll(ext, 2, 0)[SUBLANES:]
        carry = u[MIX_SUB_ROWS - SUBLANES:]
        yb = (zb * (cw[0:1] * u2 + cw[1:2] * u1 + cw[2:3] * u)).astype(BF16)

        z_g0 = project(c_g, c_g + d_model)

        l0, l1, l2 = natural(1, rs), natural(3, rs), natural(5, rs)
        mx = jnp.maximum(jnp.maximum(l0, l1), l2)
        e0, e1, e2 = jnp.exp(l0 - mx), jnp.exp(l1 - mx), jnp.exp(l2 - mx)
        inv = 1.0 / (e0 + e1 + e2)
        ya = jnp.concatenate([natural(0, rs) * (e0 * inv), natural(2, rs) * (e1 * inv),
                              natural(4, rs) * (e2 * inv)], axis=1).astype(BF16)
        pa = jnp.dot(ya, wa_ref[...], preferred_element_type=F32)

        us = _gelu(z_sg[:, :sg_w])
        vs = _gelu(z_sg[:, sg_w:])
        mu = jnp.mean(vs, axis=-1, keepdims=True)
        xc = vs - mu
        var = jnp.mean(xc * xc, axis=-1, keepdims=True)
        vln = (xc * lax.rsqrt(var + LN_EPS) * lng_ref[...] + lnb_ref[...]).astype(BF16)

        z_g1 = project(c_g + d_model, c_g + 2 * d_model)
        pb = jnp.dot(yb, wb_ref[...], preferred_element_type=F32)
        m = _sigmoid(z_g0) * pa + _sigmoid(z_g1) * pb

        for g in range(SG_GROUPS):
            bias = sgb_ref[g]
            cs = slice(g * gch, (g + 1) * gch)
            for n in range(MIX_SUB_ROWS // SG_CHUNK):
                ns = slice(n * SG_CHUNK, (n + 1) * SG_CHUNK)
                sv = jnp.dot(w_sg[g], vln[ns, cs], preferred_element_type=F32) + bias
                yc_ref[r0 + n * SG_CHUNK:r0 + (n + 1) * SG_CHUNK, cs] = (us[ns, cs] * sv).astype(BF16)

        z_g2 = project(c_g + 2 * d_model, c_g + 3 * d_model)
        pc = jnp.dot(yc_ref[rs, :], wc_ref[...], preferred_element_type=F32)
        m = m + _sigmoid(z_g2) * pc
        out_ref[rs, :] = h + jnp.dot(m.astype(BF16), wo_ref[...], preferred_element_type=F32)

    carry_ref[...] = carry


def _mixer_merge(layer, h2d, g, att, w_in, conv_w, ln_g, ln_b, sg_w, sg_b, w_a, w_b, w_c, w_o, seq):
    n_tok, d_model = h2d.shape
    tm = MIX_ROWS
    conv_width = conv_w.shape[-1]
    sg_width = ln_g.shape[-1]
    row_tile = lambda width: pl.BlockSpec((tm, width), lambda i: (i, 0))
    (o0, l0), (o1, l1), (o2, l2) = att
    att_tiles = [pl.BlockSpec((tm // d, d * GROUP_WIDTH), lambda i: (i, 0)) for d in DILATIONS]
    kern = functools.partial(_mix_kernel, tiles_per_seq=seq // tm, conv_w=conv_width, sg_w=sg_width)
    return pl.pallas_call(
        kern,
        grid=(n_tok // tm,),
        in_specs=[row_tile(d_model), _resident(g, layer)]
        + att_tiles * 2
        + [_resident(x, layer) for x in (w_in, conv_w, ln_g, ln_b, sg_w, sg_b, w_a, w_b, w_c, w_o)],
        out_specs=row_tile(d_model),
        out_shape=jax.ShapeDtypeStruct((n_tok, d_model), F32),
        scratch_shapes=[pltpu.VMEM((SUBLANES, conv_width), F32),
                        pltpu.VMEM((tm, sg_width), BF16),
                        pltpu.VMEM((2 * len(DILATIONS) * (GROUP_WIDTH // LANES), tm, LANES), F32)],
        compiler_params=pltpu.CompilerParams(
            dimension_semantics=("arbitrary",), vmem_limit_bytes=VMEM_LIMIT_BYTES),
        name="mixer_merge",
    )(h2d, g, o0, o1, o2, l0, l1, l2, w_in, conv_w, ln_g, ln_b, sg_w, sg_b, w_a, w_b, w_c, w_o)


def _mlp_kernel(h_ref, gm_ref, wu_ref, wd_ref, p_ref, gp_ref, wg_ref, wp_ref, gf_ref, out_ref,
                *, final_norm):
    d_ff = wu_ref.shape[1]
    for s in range(h_ref.shape[0] // MLP_SUB_ROWS):
        rs = slice(s * MLP_SUB_ROWS, (s + 1) * MLP_SUB_ROWS)
        h = h_ref[rs, :]
        c = _rms_norm(h, gm_ref[...]).astype(BF16)
        acc = h
        for f0 in range(0, d_ff, MLP_FF_TILE):
            t = jnp.dot(c, wu_ref[:, f0:f0 + MLP_FF_TILE], preferred_element_type=F32)
            t = jnp.square(jnp.maximum(t, 0.0)).astype(BF16)
            acc = acc + jnp.dot(t, wd_ref[f0:f0 + MLP_FF_TILE, :], preferred_element_type=F32)
        e = _rms_norm(acc, gp_ref[...]).astype(BF16)
        gate = _sigmoid(jnp.dot(e, wg_ref[...], preferred_element_type=F32))
        emb = jnp.dot(p_ref[rs, :].astype(BF16), wp_ref[...], preferred_element_type=F32)
        out = acc + gate * emb
        out_ref[rs, :] = _rms_norm(out, gf_ref[...]) if final_norm else out


def _mlp_ple(layer, h2d, g_mlp, w_up, w_down, p3d, g_ple, w_pg, w_pe, g_final, final_norm):
    n_tok, d_model = h2d.shape
    tm = MLP_ROWS
    return pl.pallas_call(
        functools.partial(_mlp_kernel, final_norm=final_norm),
        grid=(n_tok // tm,),
        in_specs=[
            pl.BlockSpec((tm, d_model), lambda i: (i, 0)),
            _resident(g_mlp, layer),
            _resident(w_up, layer),
            _resident(w_down, layer),
            pl.BlockSpec((None, tm, p3d.shape[-1]), lambda i: (layer, i, 0)),
            _resident(g_ple, layer),
            _resident(w_pg, layer),
            _resident(w_pe, layer),
            _resident(g_final, 0),
        ],
        out_specs=pl.BlockSpec((tm, d_model), lambda i: (i, 0)),
        out_shape=jax.ShapeDtypeStruct((n_tok, d_model), F32),
        compiler_params=pltpu.CompilerParams(
            dimension_semantics=("arbitrary",), vmem_limit_bytes=VMEM_LIMIT_BYTES),
        name="mlp_ple",
    )(h2d, g_mlp, w_up, w_down, p3d, g_ple, w_pg, w_pe, g_final)


def _rotary_tables(positions):
    inv_freq = ROPE_THETA ** (-(jnp.arange(0, 2 * ROT_HALF, 2, dtype=F32) / (2 * ROT_HALF)))
    in_head = jnp.arange(LANES) % HEAD_DIM
    rotated = in_head < 2 * ROT_HALF
    freq = jnp.where(rotated, inv_freq[in_head % ROT_HALF], 0.0)
    sign = jnp.where(in_head < ROT_HALF, -1.0, 1.0).astype(F32)
    ang = positions.astype(F32).reshape(-1, 1) * freq
    return jnp.cos(ang), sign * jnp.sin(ang)


def kernel(x, p, positions, norm_mix_g, w_in, conv_w, sg_ln_g, sg_ln_b, sg_w, sg_b,
           w_branch_a, w_branch_b, w_branch_c, w_out, norm_mlp_g, w_up, w_down,
           norm_ple_g, w_ple_gate, w_ple_proj, norm_final_g):
    batch, seq, d_model = x.shape
    depth = w_in.shape[0]
    n_tok = batch * seq
    qkv_width = 3 * len(DILATIONS) * GROUP_WIDTH
    assert seq % (max(DILATIONS) * ATTN_BLOCK) == 0 and seq % ATTN_ROWS_MAX == 0
    assert n_tok % MLP_ROWS == 0 and seq % MIX_ROWS == 0

    cos_t, sin_t = _rotary_tables(positions)
    rows = lambda v: v.reshape(v.shape[0], 1, v.shape[-1])
    bf = lambda w: w.astype(BF16)
    w_in_b = bf(w_in)
    sg_bias = jnp.broadcast_to(sg_b[:, :, :, None], sg_w.shape)
    w_a, w_b, w_c, w_o = bf(w_branch_a), bf(w_branch_b), bf(w_branch_c), bf(w_out)
    w_u, w_d, w_pg, w_pe = bf(w_up), bf(w_down), bf(w_ple_gate), bf(w_ple_proj)
    g_mix, g_mlp, g_ple = rows(norm_mix_g), rows(norm_mlp_g), rows(norm_ple_g)
    ln_g, ln_b = rows(sg_ln_g), rows(sg_ln_b)
    p3d = p.reshape(depth, n_tok, p.shape[-1])

    g_final = norm_final_g.reshape(1, 1, -1)

    h = x.reshape(n_tok, d_model)
    for i in range(depth):
        qkv = _qkv_projection(i, h, g_mix, w_in_b, cos_t, sin_t)
        att = [_attention_group(qkv_g, d, batch, seq) for qkv_g, d in zip(qkv, DILATIONS)]
        h = _mixer_merge(i, h, g_mix, att, w_in_b, conv_w, ln_g, ln_b, sg_w, sg_bias,
                         w_a, w_b, w_c, w_o, seq)
        h = _mlp_ple(i, h, g_mlp, w_u, w_d, p3d, g_ple, w_pg, w_pe, g_final,
                     final_norm=(i == depth - 1))
    return h.reshape(batch, seq, d_model)
```

```python
import functools
import math

import jax
import jax.numpy as jnp
from jax import lax
from jax.experimental import pallas as pl
from jax.experimental.pallas import tpu as pltpu

F32 = jnp.float32
BF16 = jnp.bfloat16

HEAD_DIM = 64
HEADS_PER_GROUP = 4
GROUP_WIDTH = HEADS_PER_GROUP * HEAD_DIM
DILATIONS = (1, 4, 16)
ATTN_BLOCK = 128
ROT_HALF = 8
ROPE_THETA = 500000.0
CONV_K = 3
SG_CHUNK = 128
SG_GROUPS = 4
RMS_EPS = 1e-6
LN_EPS = 1e-5

LANES = 128
SUBLANES = 8
VMEM_LIMIT_BYTES = 56 * 1024 * 1024

QKV_ROWS = 1024
ATTN_ROWS_MAX = 1024
MIX_ROWS = 512
MIX_SUB_ROWS = 256
MLP_ROWS = 1024
MLP_SUB_ROWS = 256
MLP_FF_TILE = 1024


def _rms_norm(h, g):
    ms = jnp.mean(h * h, axis=-1, keepdims=True)
    return h * lax.rsqrt(ms + RMS_EPS) * g


def _sigmoid(x):
    return 1.0 / (1.0 + jnp.exp(-x))


def _gelu(x):
    return 0.5 * x * (1.0 + lax.erf(x * (1.0 / math.sqrt(2.0))))


def _resident(stacked, layer):
    shape = stacked.shape[1:]
    index = (layer,) + (0,) * len(shape)
    return pl.BlockSpec((None,) + shape, lambda *_: index, pipeline_mode=pl.Buffered(1))


def _qkv_kernel(h_ref, g_ref, w_ref, cos_ref, sin_ref, *refs):
    outs, slab_ref = refs[:-1], refs[-1]
    n_groups = len(outs)
    tm = h_ref.shape[0]
    a = _rms_norm(h_ref[...], g_ref[...]).astype(BF16)
    cos_t = cos_ref[...]
    sin_t = sin_ref[...]
    lane = lax.broadcasted_iota(jnp.int32, cos_t.shape, 1)
    low_half = (lane % HEAD_DIM) < ROT_HALF
    slab = 0
    for which in range(3):
        for g, d in enumerate(DILATIONS):
            c0 = (which * n_groups + g) * GROUP_WIDTH
            z = jnp.dot(a, w_ref[:, c0:c0 + GROUP_WIDTH], preferred_element_type=F32)
            for half in range(GROUP_WIDTH // LANES):
                t = z[:, half * LANES:(half + 1) * LANES]
                if which < 2:
                    partner = jnp.where(low_half,
                                        pltpu.roll(t, LANES - ROT_HALF, 1),
                                        pltpu.roll(t, ROT_HALF, 1))
                    t = t * cos_t + partner * sin_t
                if d == 1:
                    outs[g][which, :, half * LANES:(half + 1) * LANES] = t.astype(BF16)
                else:
                    slab_ref[slab] = t
                    for r in range(d):
                        l0 = r * GROUP_WIDTH + half * LANES
                        rows = slab_ref[slab, pl.ds(r, tm // d, stride=d), :]
                        outs[g][which, :, l0:l0 + LANES] = rows.astype(BF16)
                    slab += 1


def _qkv_projection(layer, h2d, g, w_in, cos_t, sin_t):
    n_tok, d_model = h2d.shape
    tm = QKV_ROWS
    qkv_width = 3 * len(DILATIONS) * GROUP_WIDTH
    n_slabs = 3 * (GROUP_WIDTH // LANES) * sum(d > 1 for d in DILATIONS)
    return pl.pallas_call(
        _qkv_kernel,
        grid=(n_tok // tm,),
        in_specs=[
            pl.BlockSpec((tm, d_model), lambda i: (i, 0)),
            _resident(g, layer),
            pl.BlockSpec((None, d_model, qkv_width), lambda i: (layer, 0, 0),
                         pipeline_mode=pl.Buffered(1)),
            pl.BlockSpec((tm, LANES), lambda i: (i, 0)),
            pl.BlockSpec((tm, LANES), lambda i: (i, 0)),
        ],
        out_specs=[pl.BlockSpec((3, tm // d, d * GROUP_WIDTH), lambda i: (0, i, 0))
                   for d in DILATIONS],
        out_shape=[jax.ShapeDtypeStruct((3, n_tok // d, d * GROUP_WIDTH), BF16)
                   for d in DILATIONS],
        scratch_shapes=[pltpu.VMEM((n_slabs, tm, LANES), F32)],
        compiler_params=pltpu.CompilerParams(
            dimension_semantics=("arbitrary",), vmem_limit_bytes=VMEM_LIMIT_BYTES),
        name="qkv_proj",
    )(h2d, g, w_in, cos_t, sin_t)


def _attn_kernel(*refs, nblk, has_prev):
    if has_prev:
        q_ref, k_ref, kp_ref, v_ref, vp_ref, o_ref, l_ref = refs
        first_step = pl.program_id(2) == 0
    else:
        q_ref, k_ref, v_ref, o_ref, l_ref = refs
        kp_ref = vp_ref = None
        first_step = True

    def window(ref, prev_ref, r0, c0):
        if r0 > 0:
            return ref[r0 - ATTN_BLOCK:r0 + ATTN_BLOCK, c0:c0 + LANES]
        cur = ref[0:ATTN_BLOCK, c0:c0 + LANES]
        before = jnp.zeros_like(cur) if prev_ref is None else prev_ref[:, c0:c0 + LANES]
        return jnp.concatenate([before, cur], axis=0)

    two = 2 * ATTN_BLOCK
    qi = lax.broadcasted_iota(jnp.int32, (two, two), 0) % ATTN_BLOCK
    kj = lax.broadcasted_iota(jnp.int32, (two, two), 1)
    dist = qi + ATTN_BLOCK - kj
    band = (dist >= 0) & (dist <= ATTN_BLOCK)
    band_first = band & ((kj >= ATTN_BLOCK) | jnp.logical_not(first_step))
    mask_rest = jnp.where(band, 0.0, -jnp.inf)
    mask_first = jnp.where(band_first, 0.0, -jnp.inf)
    lane =lax.broadcasted_iota(jnp.int32, (ATTN_BLOCK, LANES), 1)
    head0 = lane < HEAD_DIM
    scale = HEAD_DIM ** -0.5
    keep0 = jnp.where(head0, scale, 0.0).astype(BF16)
    keep1 = jnp.where(head0, 0.0, scale).astype(BF16)

    for i in range(nblk):
        mask = mask_first if i == 0 else mask_rest
        r0 = i * ATTN_BLOCK
        for p in range(q_ref.shape[1] // LANES):
            c0 = p * LANES
            q2 = q_ref[r0:r0 + ATTN_BLOCK, c0:c0 + LANES]
            qs = jnp.concatenate([q2 * keep0, q2 * keep1], axis=0)
            kk = window(k_ref, kp_ref, r0, c0)
            s = lax.dot_general(qs, kk, (((1,), (1,)), ((), ())),
                                preferred_element_type=F32)
            s = s + mask
            m = jnp.max(s, axis=-1, keepdims=True)
            e = jnp.exp(s - m)
            den = jnp.sum(e, axis=-1, keepdims=True)
            vv = window(v_ref, vp_ref, r0, c0)
            pv = jnp.dot(e.astype(BF16), vv, preferred_element_type=F32)
            pv = pv * (1.0 / den)
            lse = m + jnp.log(den)
            o_ref[r0:r0 + ATTN_BLOCK, c0:c0 + LANES] = jnp.where(
                head0, pv[:ATTN_BLOCK], pv[ATTN_BLOCK:])
            l_ref[r0:r0 + ATTN_BLOCK, c0:c0 + LANES] = jnp.where(
                head0, lse[:ATTN_BLOCK], lse[ATTN_BLOCK:])


def _attention_group(qkv, dilation, batch, seq):
    m_len = seq // dilation
    mb = min(m_len, ATTN_ROWS_MAX)
    nblk = mb // ATTN_BLOCK
    n_res = min(dilation, ATTN_ROWS_MAX // mb)
    width = n_res * GROUP_WIDTH
    has_prev = mb < m_len
    view = qkv.reshape(3, batch, m_len, dilation * GROUP_WIDTH)

    def rows(which):
        return pl.BlockSpec((None, None, mb, width), lambda b, r, n: (which, b, n, r))

    def prev(which):
        return pl.BlockSpec((None, None, ATTN_BLOCK, width),
                            lambda b, r, n: (which, b, jnp.maximum(n * nblk - 1, 0), r))

    if has_prev:
        in_specs = [rows(0), rows(1), prev(1), rows(2), prev(2)]
    else:
        in_specs = [rows(0), rows(1), rows(2)]
    out_spec = pl.BlockSpec((None, mb, width), lambda b, r, n: (b, n, r))
    out_sds = jax.ShapeDtypeStruct((batch, m_len, dilation * GROUP_WIDTH), F32)
    o, l = pl.pallas_call(
        functools.partial(_attn_kernel, nblk=nblk, has_prev=has_prev),
        grid=(batch, dilation // n_res, m_len // mb),
        in_specs=in_specs,
        out_specs=[out_spec, out_spec],
        out_shape=[out_sds, out_sds],
        compiler_params=pltpu.CompilerParams(
            dimension_semantics=("arbitrary", "arbitrary", "arbitrary"),
            vmem_limit_bytes=VMEM_LIMIT_BYTES),
        name=f"attn_d{dilation}",
    )(*([view] * len(in_specs)))
    rows2d = batch * m_len
    return (o.reshape(rows2d, dilation * GROUP_WIDTH), l.reshape(rows2d, dilation * GROUP_WIDTH))


def _mix_kernel(h_ref, g_ref, o0_ref, o1_ref, o2_ref, l0_ref, l1_ref, l2_ref,
                wr_ref, cw_ref, lng_ref, lnb_ref, sgw_ref, sgb_ref,
                wa_ref, wb_ref, wc_ref, wo_ref, out_ref, carry_ref, yc_ref, nat_ref,
                *, tiles_per_seq, conv_w, sg_w):
    tm, d_model = h_ref.shape
    halves = GROUP_WIDTH // LANES
    w_skip = 3 * len(DILATIONS) * GROUP_WIDTH

    @pl.when(pl.program_id(0) % tiles_per_seq == 0)
    def _():
        carry_ref[...] = jnp.zeros_like(carry_ref)

    att_refs = (o0_ref, l0_ref, o1_ref, l1_ref, o2_ref, l2_ref)
    for slot, src_ref in enumerate(att_refs):
        d = DILATIONS[slot // 2]
        for r in range(d if d > 1 else 0):
            for half in range(halves):
                l0 = r * GROUP_WIDTH + half * LANES
                nat_ref[slot * halves + half, pl.ds(r, tm // d, stride=d), :] = src_ref[:, l0:l0 + LANES]

    def natural(slot, rs):
        if DILATIONS[slot // 2] == 1:
            return att_refs[slot][rs, :]
        return jnp.concatenate([nat_ref[slot * halves + half, rs, :] for half in range(halves)], axis=1)

    c_b, c_c, c_u, c_v, c_g = conv_w, 2 * conv_w, 3 * conv_w, 3 * conv_w + sg_w, 3 * conv_w + 2 * sg_w
    ti = lax.broadcasted_iota(jnp.int32, (SG_CHUNK, SG_CHUNK), 0)
    si = lax.broadcasted_iota(jnp.int32, (SG_CHUNK, SG_CHUNK), 1)
    gch = sg_w // SG_GROUPS
    w_sg = [jnp.where(ti >= si, sgw_ref[g], 0.0).astype(BF16) for g in range(SG_GROUPS)]
    cw = cw_ref[...]
    carry = carry_ref[...]

    for s in range(tm // MIX_SUB_ROWS):
        r0 = s * MIX_SUB_ROWS
        rs = slice(r0, r0 + MIX_SUB_ROWS)
        h = h_ref[rs, :]
        a = _rms_norm(h, g_ref[...]).astype(BF16)

        def project(c0, c1):
            return jnp.dot(a, wr_ref[:, w_skip + c0:w_skip + c1], preferred_element_type=F32)

        z_conv = project(0, c_u)
        z_sg = project(c_u, c_g)

        zx, zb, zc = z_conv[:, :c_b], z_conv[:, c_b:c_c], z_conv[:, c_c:c_u]
        u = zc * zx
        ext = jnp.concatenate([carry, u], axis=0)
        u1 = pltpu.roll(ext, 1, 0)[SUBLANES:]
        u2 = pltpu.roll(ext, 2, 0)[SUBLANES:]
        carry = u[MIX_SUB_ROWS - SUBLANES:]
        yb = (zb * (cw[0:1] * u2 + cw[1:2] * u1 + cw[2:3] * u)).astype(BF16)

        z_g0 = project(c_g, c_g + d_model)

        l0, l1, l2 = natural(1, rs), natural(3, rs), natural(5, rs)
        mx = jnp.maximum(jnp.maximum(l0, l1), l2)
        e0, e1, e2 = jnp.exp(l0 - mx), jnp.exp(l1 - mx), jnp.exp(l2 - mx)
        inv = 1.0 / (e0 + e1 + e2)
        ya = jnp.concatenate([natural(0, rs) * (e0 * inv), natural(2, rs) * (e1 * inv),
                              natural(4, rs) * (e2 * inv)], axis=1).astype(BF16)
        pa = jnp.dot(ya, wa_ref[...], preferred_element_type=F32)

        us = _gelu(z_sg[:, :sg_w])
        vs = _gelu(z_sg[:, sg_w:])
        mu = jnp.mean(vs, axis=-1, keepdims=True)
        xc = vs - mu
        var = jnp.mean(xc * xc, axis=-1, keepdims=True)
        vln = (xc * lax.rsqrt(var + LN_EPS) * lng_ref[...] + lnb_ref[...]).astype(BF16)

        z_g1 = project(c_g + d_model, c_g + 2 * d_model)
        pb = jnp.dot(yb, wb_ref[...], preferred_element_type=F32)
        m = _sigmoid(z_g0) * pa + _sigmoid(z_g1) * pb

        for g in range(SG_GROUPS):
            bias = sgb_ref[g]
            cs = slice(g * gch, (g + 1) * gch)
            for n in range(MIX_SUB_ROWS // SG_CHUNK):
                ns = slice(n * SG_CHUNK, (n + 1) * SG_CHUNK)
                sv = jnp.dot(w_sg[g], vln[ns, cs], preferred_element_type=F32) + bias
                yc_ref[r0 + n * SG_CHUNK:r0 + (n + 1) * SG_CHUNK, cs] = (us[ns, cs] * sv).astype(BF16)

        z_g2 = project(c_g + 2 * d_model, c_g + 3 * d_model)
        pc = jnp.dot(yc_ref[rs, :], wc_ref[...], preferred_element_type=F32)
        m = m + _sigmoid(z_g2) * pc
        out_ref[rs, :] = h + jnp.dot(m.astype(BF16), wo_ref[...], preferred_element_type=F32)

    carry_ref[...] = carry


def _mixer_merge(layer, h2d, g, att, w_in, conv_w, ln_g, ln_b, sg_w, sg_b, w_a, w_b, w_c, w_o, seq):
    n_tok, d_model = h2d.shape
    tm = MIX_ROWS
    conv_width = conv_w.shape[-1]
    sg_width = ln_g.shape[-1]
    row_tile = lambda width: pl.BlockSpec((tm, width), lambda i: (i, 0))
    (o0, l0), (o1, l1), (o2, l2) = att
    att_tiles = [pl.BlockSpec((tm // d, d * GROUP_WIDTH), lambda i: (i, 0)) for d in DILATIONS]
    kern = functools.partial(_mix_kernel, tiles_per_seq=seq // tm, conv_w=conv_width, sg_w=sg_width)
    return pl.pallas_call(
        kern,
        grid=(n_tok // tm,),
        in_specs=[row_tile(d_model), _resident(g, layer)]
        + att_tiles * 2
        + [_resident(x, layer) for x in (w_in, conv_w, ln_g, ln_b, sg_w, sg_b, w_a, w_b, w_c, w_o)],
        out_specs=row_tile(d_model),
        out_shape=jax.ShapeDtypeStruct((n_tok, d_model), F32),
        scratch_shapes=[pltpu.VMEM((SUBLANES, conv_width), F32),
                        pltpu.VMEM((tm, sg_width), BF16),
                        pltpu.VMEM((2 * len(DILATIONS) * (GROUP_WIDTH // LANES), tm, LANES), F32)],
        compiler_params=pltpu.CompilerParams(
            dimension_semantics=("arbitrary",), vmem_limit_bytes=VMEM_LIMIT_BYTES),
        name="mixer_merge",
    )(h2d, g, o0, o1, o2, l0, l1, l2, w_in, conv_w, ln_g, ln_b, sg_w, sg_b, w_a, w_b, w_c, w_o)


def _mlp_kernel(h_ref, gm_ref, wu_ref, wd_ref, p_ref, gp_ref, wg_ref, wp_ref, gf_ref, out_ref,
                *, final_norm):
    d_ff = wu_ref.shape[1]

    @pl.loop(0, h_ref.shape[0] // MLP_SUB_ROWS)
    def _(s):
        rs = pl.ds(pl.multiple_of(s * MLP_SUB_ROWS, MLP_SUB_ROWS), MLP_SUB_ROWS)
        h = h_ref[rs, :]
        c = _rms_norm(h, gm_ref[...]).astype(BF16)
        acc = h
        for f0 in range(0, d_ff, MLP_FF_TILE):
            t = jnp.dot(c, wu_ref[:, f0:f0 + MLP_FF_TILE], preferred_element_type=F32)
            t = jnp.square(jnp.maximum(t, 0.0)).astype(BF16)
            acc = acc + jnp.dot(t, wd_ref[f0:f0 + MLP_FF_TILE, :], preferred_element_type=F32)
        e = _rms_norm(acc, gp_ref[...]).astype(BF16)
        gate = _sigmoid(jnp.dot(e, wg_ref[...], preferred_element_type=F32))
        emb = jnp.dot(p_ref[rs, :].astype(BF16), wp_ref[...], preferred_element_type=F32)
        out = acc + gate * emb
        out_ref[rs, :] = _rms_norm(out, gf_ref[...]) if final_norm else out


def _mlp_ple(layer, h2d, g_mlp, w_up, w_down, p3d, g_ple, w_pg, w_pe, g_final, final_norm):
    n_tok, d_model = h2d.shape
    tm = MLP_ROWS
    return pl.pallas_call(
        functools.partial(_mlp_kernel, final_norm=final_norm),
        grid=(n_tok // tm,),
        in_specs=[
            pl.BlockSpec((tm, d_model), lambda i: (i, 0)),
            _resident(g_mlp, layer),
            _resident(w_up, layer),
            _resident(w_down, layer),
            pl.BlockSpec((None, tm, p3d.shape[-1]), lambda i: (layer, i, 0)),
            _resident(g_ple, layer),
            _resident(w_pg, layer),
            _resident(w_pe, layer),
            _resident(g_final, 0),
        ],
        out_specs=pl.BlockSpec((tm, d_model), lambda i: (i, 0)),
        out_shape=jax.ShapeDtypeStruct((n_tok, d_model), F32),
        compiler_params=pltpu.CompilerParams(
            dimension_semantics=("arbitrary",), vmem_limit_bytes=VMEM_LIMIT_BYTES),
        name="mlp_ple",
    )(h2d, g_mlp, w_up, w_down, p3d, g_ple, w_pg, w_pe, g_final)


def _rotary_tables(positions):
    inv_freq = ROPE_THETA ** (-(jnp.arange(0, 2 * ROT_HALF, 2, dtype=F32) / (2 * ROT_HALF)))
    in_head = jnp.arange(LANES) % HEAD_DIM
    rotated = in_head < 2 * ROT_HALF
    freq = jnp.where(rotated, inv_freq[in_head % ROT_HALF], 0.0)
    sign = jnp.where(in_head < ROT_HALF, -1.0, 1.0).astype(F32)
    ang = positions.astype(F32).reshape(-1, 1) * freq
    return jnp.cos(ang), sign * jnp.sin(ang)


def kernel(x, p, positions, norm_mix_g, w_in, conv_w, sg_ln_g, sg_ln_b, sg_w, sg_b,
           w_branch_a, w_branch_b, w_branch_c, w_out, norm_mlp_g, w_up, w_down,
           norm_ple_g, w_ple_gate, w_ple_proj, norm_final_g):
    batch, seq, d_model = x.shape
    depth = w_in.shape[0]
    n_tok = batch * seq
    qkv_width = 3 * len(DILATIONS) * GROUP_WIDTH
    assert seq % (max(DILATIONS) * ATTN_BLOCK) == 0 and seq % ATTN_ROWS_MAX == 0
    assert n_tok % MLP_ROWS == 0 and seq % MIX_ROWS == 0

    cos_t, sin_t = _rotary_tables(positions)
    rows = lambda v: v.reshape(v.shape[0], 1, v.shape[-1])
    bf = lambda w: w.astype(BF16)
    w_in_b = bf(w_in)
    sg_bias = jnp.broadcast_to(sg_b[:, :, :, None], sg_w.shape)
    w_a, w_b, w_c, w_o = bf(w_branch_a), bf(w_branch_b), bf(w_branch_c), bf(w_out)
    w_u, w_d, w_pg, w_pe = bf(w_up), bf(w_down), bf(w_ple_gate), bf(w_ple_proj)
    g_mix, g_mlp, g_ple = rows(norm_mix_g), rows(norm_mlp_g), rows(norm_ple_g)
    ln_g, ln_b = rows(sg_ln_g), rows(sg_ln_b)
    p3d = p.reshape(depth, n_tok, p.shape[-1])

    g_final = norm_final_g.reshape(1, 1, -1)

    h = x.reshape(n_tok, d_model)
    for i in range(depth):
        qkv = _qkv_projection(i, h, g_mix, w_in_b, cos_t, sin_t)
        att = [_attention_group(qkv_g, d, batch, seq) for qkv_g, d in zip(qkv, DILATIONS)]
        h = _mixer_merge(i, h, g_mix, att, w_in_b, conv_w, ln_g, ln_b, sg_w, sg_bias,
                         w_a, w_b, w_c, w_o, seq)
        h = _mlp_ple(i, h, g_mlp, w_u, w_d, p3d, g_ple, w_pg, w_pe, g_final,
                     final_norm=(i == depth - 1))
    return h.reshape(batch, seq, d_model)
```

```python
import functools
import math

import jax
import jax.numpy as jnp
from jax import lax
from jax.experimental import pallas as pl
from jax.experimental.pallas import tpu as pltpu

F32 = jnp.float32
BF16 = jnp.bfloat16

HEAD_DIM = 64
HEADS_PER_GROUP = 4
GROUP_WIDTH = HEADS_PER_GROUP * HEAD_DIM
DILATIONS = (1, 4, 16)
ATTN_BLOCK = 128
ROT_HALF = 8
ROPE_THETA = 500000.0
CONV_K = 3
SG_CHUNK = 128
SG_GROUPS = 4
RMS_EPS = 1e-6
LN_EPS = 1e-5
QUERY_SCALE = math.log2(math.e) * HEAD_DIM ** -0.5

LANES = 128
SUBLANES = 8
VMEM_LIMIT_BYTES = 56 * 1024 * 1024

QKV_ROWS = 1024
ATTN_ROWS_MAX = 1024
MIX_ROWS = 512
MIX_SUB_ROWS = 256
MLP_ROWS = 512
MLP_SUB_ROWS = 256
MLP_FF_TILE = 1024


def _rms_norm(h, g):
    ms = jnp.mean(h * h, axis=-1, keepdims=True)
    return h * lax.rsqrt(ms + RMS_EPS) * g


def _sigmoid(x):
    return 1.0 / (1.0 + jnp.exp(-x))


def _gelu(x):
    return 0.5 * x * (1.0 + lax.erf(x * (1.0 / math.sqrt(2.0))))


def _resident(stacked, layer):
    shape = stacked.shape[1:]
    index = (layer,) + (0,) * len(shape)
    return pl.BlockSpec((None,) + shape, lambda *_: index, pipeline_mode=pl.Buffered(1))


def _qkv_kernel(h_ref, g_ref, w_ref, cos_ref, sin_ref, *refs):
    outs, slab_ref = refs[:-1], refs[-1]
    n_groups = len(outs)
    tm = h_ref.shape[0]
    a = _rms_norm(h_ref[...], g_ref[...]).astype(BF16)
    cos_k, sin_k = cos_ref[...], sin_ref[...]
    cos_sin = ((cos_k * QUERY_SCALE, sin_k * QUERY_SCALE), (cos_k, sin_k))
    lane = lax.broadcasted_iota(jnp.int32, cos_k.shape, 1)
    low_half = (lane % HEAD_DIM) < ROT_HALF
    slab = 0
    for which in range(3):
        for g, d in enumerate(DILATIONS):
            c0 = (which * n_groups + g) * GROUP_WIDTH
            z = jnp.dot(a, w_ref[:, c0:c0 + GROUP_WIDTH], preferred_element_type=F32)
            for half in range(GROUP_WIDTH // LANES):
                t = z[:, half * LANES:(half + 1) * LANES]
                if which < 2:
                    partner = jnp.where(low_half,
                                        pltpu.roll(t, LANES - ROT_HALF, 1),
                                        pltpu.roll(t, ROT_HALF, 1))
                    cos_t, sin_t = cos_sin[which]
                    t = t * cos_t + partner * sin_t
                if d == 1:
                    outs[g][which, :, half * LANES:(half + 1) * LANES] = t.astype(BF16)
                else:
                    slab_ref[slab] = t
                    for r in range(d):
                        l0 = r * GROUP_WIDTH + half * LANES
                        rows = slab_ref[slab, pl.ds(r, tm // d, stride=d), :]
                        outs[g][which, :, l0:l0 + LANES] = rows.astype(BF16)
                    slab += 1


def _qkv_projection(layer, h2d, g, w_in, cos_t, sin_t):
    n_tok, d_model = h2d.shape
    tm = QKV_ROWS
    qkv_width = 3 * len(DILATIONS) * GROUP_WIDTH
    n_slabs = 3 * (GROUP_WIDTH // LANES) * sum(d > 1 for d in DILATIONS)
    return pl.pallas_call(
        _qkv_kernel,
        grid=(n_tok // tm,),
        in_specs=[
            pl.BlockSpec((tm, d_model), lambda i: (i, 0)),
            _resident(g, layer),
            pl.BlockSpec((None, d_model, qkv_width), lambda i: (layer, 0, 0),
                         pipeline_mode=pl.Buffered(1)),
            pl.BlockSpec((tm, LANES), lambda i: (i, 0)),
            pl.BlockSpec((tm, LANES), lambda i: (i, 0)),
        ],
        out_specs=[pl.BlockSpec((3, tm // d, d * GROUP_WIDTH), lambda i: (0, i, 0))
                   for d in DILATIONS],
        out_shape=[jax.ShapeDtypeStruct((3, n_tok // d, d * GROUP_WIDTH), BF16)
                   for d in DILATIONS],
        scratch_shapes=[pltpu.VMEM((n_slabs, tm, LANES), F32)],
        compiler_params=pltpu.CompilerParams(
            dimension_semantics=("arbitrary",), vmem_limit_bytes=VMEM_LIMIT_BYTES),
        name="qkv_proj",
    )(h2d, g, w_in, cos_t, sin_t)


def _attn_kernel(*refs, nblk, has_prev):
    if has_prev:
        q_ref, k_ref, kp_ref, v_ref, vp_ref, o_ref, l_ref = refs
        first_step = pl.program_id(2) == 0
    else:
        q_ref, k_ref, v_ref, o_ref, l_ref = refs
        kp_ref = vp_ref = None
        first_step = True

    def window(ref, prev_ref, r0, c0):
        if r0 > 0:
            return ref[r0 - ATTN_BLOCK:r0 + ATTN_BLOCK, c0:c0 + LANES]
        cur = ref[0:ATTN_BLOCK, c0:c0 + LANES]
        before = jnp.zeros_like(cur) if prev_ref is None else prev_ref[:, c0:c0 + LANES]
        return jnp.concatenate([before, cur], axis=0)

    two = 2 * ATTN_BLOCK
    qi = lax.broadcasted_iota(jnp.int32, (two, two), 0) % ATTN_BLOCK
    kj = lax.broadcasted_iota(jnp.int32, (two, two), 1)
    dist = qi + ATTN_BLOCK - kj
    band = (dist >= 0) & (dist <= ATTN_BLOCK)
    band_first = band & ((kj >= ATTN_BLOCK) | jnp.logical_not(first_step))
    mask_rest = jnp.where(band, 0.0, -jnp.inf)
    mask_first = jnp.where(band_first, 0.0, -jnp.inf)
    lane =lax.broadcasted_iota(jnp.int32, (ATTN_BLOCK, LANES), 1)
    head0 = lane < HEAD_DIM
    keep0 = jnp.where(head0, 1.0, 0.0).astype(BF16)
    keep1 = jnp.where(head0, 0.0, 1.0).astype(BF16)
    ones = jnp.ones((two, LANES), BF16)

    for i in range(nblk):
        mask = mask_first if i == 0 else mask_rest
        r0 = i * ATTN_BLOCK
        for p in range(q_ref.shape[1] // LANES):
            c0 = p * LANES
            q2 = q_ref[r0:r0 + ATTN_BLOCK, c0:c0 + LANES]
            qs = jnp.concatenate([q2 * keep0, q2 * keep1], axis=0)
            kk = window(k_ref, kp_ref, r0, c0)
            s = lax.dot_general(qs, kk, (((1,), (1,)), ((), ())),
                                preferred_element_type=F32)
            s = s + mask
            m = jnp.max(s, axis=-1, keepdims=True)
            e = jnp.exp2((s - m).astype(BF16))
            vv = jnp.concatenate([window(v_ref, vp_ref, r0, c0), ones], axis=1)
            pv = jnp.dot(e, vv, preferred_element_type=F32)
            den = pv[:, LANES:]
            pv = pv[:, :LANES] * (1.0 / den)
            lse = m + jnp.log2(den)
            o_ref[r0:r0 + ATTN_BLOCK, c0:c0 + LANES] = jnp.where(
                head0, pv[:ATTN_BLOCK], pv[ATTN_BLOCK:])
            l_ref[r0:r0 + ATTN_BLOCK, c0:c0 + LANES] = jnp.where(
                head0, lse[:ATTN_BLOCK], lse[ATTN_BLOCK:])


def _attention_group(qkv, dilation, batch, seq):
    m_len = seq // dilation
    mb = min(m_len, ATTN_ROWS_MAX)
    nblk = mb // ATTN_BLOCK
    n_res = min(dilation, ATTN_ROWS_MAX // mb)
    width = n_res * GROUP_WIDTH
    has_prev = mb < m_len
    view = qkv.reshape(3, batch, m_len, dilation * GROUP_WIDTH)

    def rows(which):
        return pl.BlockSpec((None, None, mb, width), lambda b, r, n: (which, b, n, r))

    def prev(which):
        return pl.BlockSpec((None, None, ATTN_BLOCK, width),
                            lambda b, r, n: (which, b, jnp.maximum(n * nblk - 1, 0), r))

    if has_prev:
        in_specs = [rows(0), rows(1), prev(1), rows(2), prev(2)]
    else:
        in_specs = [rows(0), rows(1), rows(2)]
    out_spec = pl.BlockSpec((None, mb, width), lambda b, r, n: (b, n, r))
    out_sds = jax.ShapeDtypeStruct((batch, m_len, dilation * GROUP_WIDTH), F32)
    o, l = pl.pallas_call(
        functools.partial(_attn_kernel, nblk=nblk, has_prev=has_prev),
        grid=(batch, dilation // n_res, m_len // mb),
        in_specs=in_specs,
        out_specs=[out_spec, out_spec],
        out_shape=[out_sds, out_sds],
        compiler_params=pltpu.CompilerParams(
            dimension_semantics=("arbitrary", "arbitrary", "arbitrary"),
            vmem_limit_bytes=VMEM_LIMIT_BYTES),
        name=f"attn_d{dilation}",
    )(*([view] * len(in_specs)))
    rows2d = batch * m_len
    return (o.reshape(rows2d, dilation * GROUP_WIDTH), l.reshape(rows2d, dilation * GROUP_WIDTH))


def _mix_kernel(h_ref, g_ref, o0_ref, o1_ref, o2_ref, l0_ref, l1_ref, l2_ref,
                wr_ref, cw_ref, lng_ref, lnb_ref, sgw_ref, sgb_ref,
                wa_ref, wb_ref, wc_ref, wo_ref, out_ref, carry_ref, yc_ref, nat_ref,
                *, tiles_per_seq, conv_w, sg_w):
    tm, d_model = h_ref.shape
    halves = GROUP_WIDTH // LANES
    w_skip = 3 * len(DILATIONS) * GROUP_WIDTH

    @pl.when(pl.program_id(0) % tiles_per_seq == 0)
    def _():
        carry_ref[...] = jnp.zeros_like(carry_ref)

    att_refs = (o0_ref, l0_ref, o1_ref, l1_ref, o2_ref, l2_ref)
    for slot, src_ref in enumerate(att_refs):
        d = DILATIONS[slot // 2]
        for r in range(d if d > 1 else 0):
            for half in range(halves):
                l0 = r * GROUP_WIDTH + half * LANES
                nat_ref[slot * halves + half, pl.ds(r, tm // d, stride=d), :] = src_ref[:, l0:l0 + LANES]

    def natural(slot, rs):
        if DILATIONS[slot // 2] == 1:
            return att_refs[slot][rs, :]
        return jnp.concatenate([nat_ref[slot * halves + half, rs, :] for half in range(halves)], axis=1)

    c_b, c_c, c_u, c_v, c_g = conv_w, 2 * conv_w, 3 * conv_w, 3 * conv_w + sg_w, 3 * conv_w + 2 * sg_w
    ti = lax.broadcasted_iota(jnp.int32, (SG_CHUNK, SG_CHUNK), 0)
    si = lax.broadcasted_iota(jnp.int32, (SG_CHUNK, SG_CHUNK), 1)
    gch = sg_w // SG_GROUPS
    w_sg = [jnp.where(ti >= si, sgw_ref[g], 0.0).astype(BF16) for g in range(SG_GROUPS)]
    cw = cw_ref[...]
    carry = carry_ref[...]

    for s in range(tm // MIX_SUB_ROWS):
        r0 = s * MIX_SUB_ROWS
        rs = slice(r0, r0 + MIX_SUB_ROWS)
        h = h_ref[rs, :]
        a = _rms_norm(h, g_ref[...]).astype(BF16)

        def project(c0, c1):
            return jnp.dot(a, wr_ref[:, w_skip + c0:w_skip + c1], preferred_element_type=F32)

        z_conv = project(0, c_u)
        z_sg = project(c_u, c_g)

        zx, zb, zc = z_conv[:, :c_b], z_conv[:, c_b:c_c], z_conv[:, c_c:c_u]
        u = zc * zx
        ext = jnp.concatenate([carry, u], axis=0)
        u1 = pltpu.roll(ext, 1, 0)[SUBLANES:]
        u2 = pltpu.roll(ext, 2, 0)[SUBLANES:]
        carry = u[MIX_SUB_ROWS - SUBLANES:]
        yb = (zb * (cw[0:1] * u2 + cw[1:2] * u1 + cw[2:3] * u)).astype(BF16)

        z_g0 = project(c_g, c_g + d_model)

        l0, l1, l2 = natural(1, rs), natural(3, rs), natural(5, rs)
        mx = jnp.maximum(jnp.maximum(l0, l1), l2)
        e0, e1, e2 = jnp.exp2(l0 - mx), jnp.exp2(l1 - mx), jnp.exp2(l2 - mx)
        inv = 1.0 / (e0 + e1 + e2)
        ya = jnp.concatenate([natural(0, rs) * (e0 * inv), natural(2, rs) * (e1 * inv),
                              natural(4, rs) * (e2 * inv)], axis=1).astype(BF16)
        pa = jnp.dot(ya, wa_ref[...], preferred_element_type=F32)

        us = _gelu(z_sg[:, :sg_w])
        vs = _gelu(z_sg[:, sg_w:])
        mu = jnp.mean(vs, axis=-1, keepdims=True)
        xc = vs - mu
        var = jnp.mean(xc * xc, axis=-1, keepdims=True)
        vln = (xc * lax.rsqrt(var + LN_EPS) * lng_ref[...] + lnb_ref[...]).astype(BF16)

        z_g1 = project(c_g + d_model, c_g + 2 * d_model)
        pb = jnp.dot(yb, wb_ref[...], preferred_element_type=F32)
        m = _sigmoid(z_g0) * pa + _sigmoid(z_g1) * pb

        for g in range(SG_GROUPS):
            bias = sgb_ref[g]
            cs = slice(g * gch, (g + 1) * gch)
            for n in range(MIX_SUB_ROWS // SG_CHUNK):
                ns = slice(n * SG_CHUNK, (n + 1) * SG_CHUNK)
                sv = jnp.dot(w_sg[g], vln[ns, cs], preferred_element_type=F32) + bias
                yc_ref[r0 + n * SG_CHUNK:r0 + (n + 1) * SG_CHUNK, cs] = (us[ns, cs] * sv).astype(BF16)

        z_g2 = project(c_g + 2 * d_model, c_g + 3 * d_model)
        pc = jnp.dot(yc_ref[rs, :], wc_ref[...], preferred_element_type=F32)
        m = m + _sigmoid(z_g2) * pc
        out_ref[rs, :] = h + jnp.dot(m.astype(BF16), wo_ref[...], preferred_element_type=F32)

    carry_ref[...] = carry


def _mixer_merge(layer, h2d, g, att, w_in, conv_w, ln_g, ln_b, sg_w, sg_b, w_a, w_b, w_c, w_o, seq):
    n_tok, d_model = h2d.shape
    tm = MIX_ROWS
    conv_width = conv_w.shape[-1]
    sg_width = ln_g.shape[-1]
    row_tile = lambda width: pl.BlockSpec((tm, width), lambda i: (i, 0))
    (o0, l0), (o1, l1), (o2, l2) = att
    att_tiles = [pl.BlockSpec((tm // d, d * GROUP_WIDTH), lambda i: (i, 0)) for d in DILATIONS]
    kern = functools.partial(_mix_kernel, tiles_per_seq=seq // tm, conv_w=conv_width, sg_w=sg_width)
    return pl.pallas_call(
        kern,
        grid=(n_tok // tm,),
        in_specs=[row_tile(d_model), _resident(g, layer)]
        + att_tiles * 2
        + [_resident(x, layer) for x in (w_in, conv_w, ln_g, ln_b, sg_w, sg_b, w_a, w_b, w_c, w_o)],
        out_specs=row_tile(d_model),
        out_shape=jax.ShapeDtypeStruct((n_tok, d_model), F32),
        scratch_shapes=[pltpu.VMEM((SUBLANES, conv_width), F32),
                        pltpu.VMEM((tm, sg_width), BF16),
                        pltpu.VMEM((2 * len(DILATIONS) * (GROUP_WIDTH // LANES), tm, LANES), F32)],
        compiler_params=pltpu.CompilerParams(
            dimension_semantics=("arbitrary",), vmem_limit_bytes=VMEM_LIMIT_BYTES),
        name="mixer_merge",
    )(h2d, g, o0, o1, o2, l0, l1, l2, w_in, conv_w, ln_g, ln_b, sg_w, sg_b, w_a, w_b, w_c, w_o)


def _mlp_kernel(h_ref, gm_ref, wu_ref, wd_ref, p_ref, gp_ref, wg_ref, wp_ref, gf_ref, out_ref,
                *, final_norm):
    d_ff = wu_ref.shape[1]

    for s in range(h_ref.shape[0] // MLP_SUB_ROWS):
        rs = slice(s * MLP_SUB_ROWS, (s + 1) * MLP_SUB_ROWS)
        h = h_ref[rs, :]
        c = _rms_norm(h, gm_ref[...]).astype(BF16)
        acc = h
        for f0 in range(0, d_ff, MLP_FF_TILE):
            t = jnp.dot(c, wu_ref[:, f0:f0 + MLP_FF_TILE], preferred_element_type=F32)
            t = jnp.square(jnp.maximum(t, 0.0)).astype(BF16)
            acc = acc + jnp.dot(t, wd_ref[f0:f0 + MLP_FF_TILE, :], preferred_element_type=F32)
        e = _rms_norm(acc, gp_ref[...]).astype(BF16)
        gate = _sigmoid(jnp.dot(e, wg_ref[...], preferred_element_type=F32))
        emb = jnp.dot(p_ref[rs, :].astype(BF16), wp_ref[...], preferred_element_type=F32)
        out = acc + gate * emb
        out_ref[rs, :] = _rms_norm(out, gf_ref[...]) if final_norm else out


def _mlp_ple(layer, h2d, g_mlp, w_up, w_down, p3d, g_ple, w_pg, w_pe, g_final, final_norm):
    n_tok, d_model = h2d.shape
    tm = MLP_ROWS
    return pl.pallas_call(
        functools.partial(_mlp_kernel, final_norm=final_norm),
        grid=(n_tok // tm,),
        in_specs=[
            pl.BlockSpec((tm, d_model), lambda i: (i, 0)),
            _resident(g_mlp, layer),
            _resident(w_up, layer),
            _resident(w_down, layer),
            pl.BlockSpec((None, tm, p3d.shape[-1]), lambda i: (layer, i, 0)),
            _resident(g_ple, layer),
            _resident(w_pg, layer),
            _resident(w_pe, layer),
            _resident(g_final, 0),
        ],
        out_specs=pl.BlockSpec((tm, d_model), lambda i: (i, 0)),
        out_shape=jax.ShapeDtypeStruct((n_tok, d_model), F32),
        compiler_params=pltpu.CompilerParams(
            dimension_semantics=("arbitrary",), vmem_limit_bytes=VMEM_LIMIT_BYTES),
        name="mlp_ple",
    )(h2d, g_mlp, w_up, w_down, p3d, g_ple, w_pg, w_pe, g_final)


def _rotary_tables(positions):
    inv_freq = ROPE_THETA ** (-(jnp.arange(0, 2 * ROT_HALF, 2, dtype=F32) / (2 * ROT_HALF)))
    in_head = jnp.arange(LANES) % HEAD_DIM
    rotated = in_head < 2 * ROT_HALF
    freq = jnp.where(rotated, inv_freq[in_head % ROT_HALF], 0.0)
    sign = jnp.where(in_head < ROT_HALF, -1.0, 1.0).astype(F32)
    ang = positions.astype(F32).reshape(-1, 1) * freq
    return jnp.cos(ang), sign * jnp.sin(ang)


def kernel(x, p, positions, norm_mix_g, w_in, conv_w, sg_ln_g, sg_ln_b, sg_w, sg_b,
           w_branch_a, w_branch_b, w_branch_c, w_out, norm_mlp_g, w_up, w_down,
           norm_ple_g, w_ple_gate, w_ple_proj, norm_final_g):
    batch, seq, d_model = x.shape
    depth = w_in.shape[0]
    n_tok = batch * seq
    qkv_width = 3 * len(DILATIONS) * GROUP_WIDTH
    assert seq % (max(DILATIONS) * ATTN_BLOCK) == 0 and seq % ATTN_ROWS_MAX == 0
    assert n_tok % MLP_ROWS == 0 and seq % MIX_ROWS == 0

    cos_t, sin_t = _rotary_tables(positions)
    rows = lambda v: v.reshape(v.shape[0], 1, v.shape[-1])
    bf = lambda w: w.astype(BF16)
    w_in_b = bf(w_in)
    sg_bias = jnp.broadcast_to(sg_b[:, :, :, None], sg_w.shape)
    w_a, w_b, w_c, w_o = bf(w_branch_a), bf(w_branch_b), bf(w_branch_c), bf(w_out)
    w_u, w_d, w_pg, w_pe = bf(w_up), bf(w_down), bf(w_ple_gate), bf(w_ple_proj)
    g_mix, g_mlp, g_ple = rows(norm_mix_g), rows(norm_mlp_g), rows(norm_ple_g)
    ln_g, ln_b = rows(sg_ln_g), rows(sg_ln_b)
    p3d = p.reshape(depth, n_tok, p.shape[-1])

    g_final = norm_final_g.reshape(1, 1, -1)

    h = x.reshape(n_tok, d_model)
    for i in range(depth):
        qkv = _qkv_projection(i, h, g_mix, w_in_b, cos_t, sin_t)
        att = [_attention_group(qkv_g, d, batch, seq) for qkv_g, d in zip(qkv, DILATIONS)]
        h = _mixer_merge(i, h, g_mix, att, w_in_b, conv_w, ln_g, ln_b, sg_w, sg_bias,
                         w_a, w_b, w_c, w_o, seq)
        h = _mlp_ple(i, h, g_mlp, w_u, w_d, p3d, g_ple, w_pg, w_pe, g_final,
                     final_norm=(i == depth - 1))
    return h.reshape(batch, seq, d_model)
```

```python
import functools
import math

import jax
import jax.numpy as jnp
from jax import lax
from jax.experimental import pallas as pl
from jax.experimental.pallas import tpu as pltpu

F32 = jnp.float32
BF16 = jnp.bfloat16

HEAD_DIM = 64
HEADS_PER_GROUP = 4
GROUP_WIDTH = HEADS_PER_GROUP * HEAD_DIM
DILATIONS = (1, 4, 16)
ATTN_BLOCK = 128
ROT_HALF = 8
ROPE_THETA = 500000.0
CONV_K = 3
SG_CHUNK = 128
SG_GROUPS = 4
RMS_EPS = 1e-6
LN_EPS = 1e-5
QUERY_SCALE = math.log2(math.e) * HEAD_DIM ** -0.5

LANES = 128
SUBLANES = 8
VMEM_LIMIT_BYTES = 56 * 1024 * 1024

QKV_ROWS = 1024
ATTN_ROWS_MAX = 1024
MIX_ROWS = 512
MIX_SUB_ROWS = 256
MLP_ROWS = 512
MLP_SUB_ROWS = 256
MLP_FF_TILE = 1024


def _rms_norm(h, g):
    ms = jnp.mean(h * h, axis=-1, keepdims=True)
    return h * lax.rsqrt(ms + RMS_EPS) * g


def _sigmoid(x):
    return 1.0 / (1.0 + jnp.exp(-x))


def _gelu(x):
    return 0.5 * x * (1.0 + lax.erf(x * (1.0 / math.sqrt(2.0))))


def _resident(stacked, layer):
    shape = stacked.shape[1:]
    index = (layer,) + (0,) * len(shape)
    return pl.BlockSpec((None,) + shape, lambda *_: index, pipeline_mode=pl.Buffered(1))


def _qkv_kernel(h_ref, g_ref, w_ref, cos_ref, sin_ref, *refs):
    outs, slab_ref = refs[:-1], refs[-1]
    n_groups = len(outs)
    tm = h_ref.shape[0]
    a = _rms_norm(h_ref[...], g_ref[...]).astype(BF16)
    cos_k, sin_k = cos_ref[...], sin_ref[...]
    cos_sin = ((cos_k * QUERY_SCALE, sin_k * QUERY_SCALE), (cos_k, sin_k))
    lane = lax.broadcasted_iota(jnp.int32, cos_k.shape, 1)
    low_half = (lane % HEAD_DIM) < ROT_HALF
    slab = 0
    for which in range(3):
        for g, d in enumerate(DILATIONS):
            c0 = (which * n_groups + g) * GROUP_WIDTH
            z = jnp.dot(a, w_ref[:, c0:c0 + GROUP_WIDTH], preferred_element_type=F32)
            for half in range(GROUP_WIDTH // LANES):
                t = z[:, half * LANES:(half + 1) * LANES]
                if which < 2:
                    partner = jnp.where(low_half,
                                        pltpu.roll(t, LANES - ROT_HALF, 1),
                                        pltpu.roll(t, ROT_HALF, 1))
                    cos_t, sin_t = cos_sin[which]
                    t = t * cos_t + partner * sin_t
                if d == 1:
                    outs[g][which, :, half * LANES:(half + 1) * LANES] = t.astype(BF16)
                else:
                    slab_ref[slab] = t
                    for r in range(d):
                        l0 = r * GROUP_WIDTH + half * LANES
                        rows = slab_ref[slab, pl.ds(r, tm // d, stride=d), :]
                        outs[g][which, :, l0:l0 + LANES] = rows.astype(BF16)
                    slab += 1


def _qkv_projection(layer, h2d, g, w_in, cos_t, sin_t):
    n_tok, d_model = h2d.shape
    tm = QKV_ROWS
    qkv_width = 3 * len(DILATIONS) * GROUP_WIDTH
    n_slabs = 3 * (GROUP_WIDTH // LANES) * sum(d > 1 for d in DILATIONS)
    return pl.pallas_call(
        _qkv_kernel,
        grid=(n_tok // tm,),
        in_specs=[
            pl.BlockSpec((tm, d_model), lambda i: (i, 0)),
            _resident(g, layer),
            pl.BlockSpec((None, d_model, qkv_width), lambda i: (layer, 0, 0),
                         pipeline_mode=pl.Buffered(1)),
            pl.BlockSpec((tm, LANES), lambda i: (i, 0)),
            pl.BlockSpec((tm, LANES), lambda i: (i, 0)),
        ],
        out_specs=[pl.BlockSpec((3, tm // d, d * GROUP_WIDTH), lambda i: (0, i, 0))
                   for d in DILATIONS],
        out_shape=[jax.ShapeDtypeStruct((3, n_tok // d, d * GROUP_WIDTH), BF16)
                   for d in DILATIONS],
        scratch_shapes=[pltpu.VMEM((n_slabs, tm, LANES), F32)],
        compiler_params=pltpu.CompilerParams(
            dimension_semantics=("arbitrary",), vmem_limit_bytes=VMEM_LIMIT_BYTES),
        name="qkv_proj",
    )(h2d, g, w_in, cos_t, sin_t)


def _attend(q_ref, k_ref, kp_ref, v_ref, vp_ref, o_ref, l_ref, first_step):
    def window(ref, prev_ref, r0, c0):
        if r0 > 0:
            return ref[r0 - ATTN_BLOCK:r0 + ATTN_BLOCK, c0:c0 + LANES]
        cur = ref[0:ATTN_BLOCK, c0:c0 + LANES]
        before = jnp.zeros_like(cur) if prev_ref is None else prev_ref[:, c0:c0 + LANES]
        return jnp.concatenate([before, cur], axis=0)

    two = 2 * ATTN_BLOCK
    qi = lax.broadcasted_iota(jnp.int32, (two, two), 0) % ATTN_BLOCK
    kj = lax.broadcasted_iota(jnp.int32, (two, two), 1)
    dist = qi + ATTN_BLOCK - kj
    band = (dist >= 0) & (dist <= ATTN_BLOCK)
    band_first = band & ((kj >= ATTN_BLOCK) | jnp.logical_not(first_step))
    mask_rest = jnp.where(band, 0.0, -jnp.inf)
    mask_first = jnp.where(band_first, 0.0, -jnp.inf)
    lane = lax.broadcasted_iota(jnp.int32, (ATTN_BLOCK, LANES), 1)
    head0 = lane < HEAD_DIM
    keep0 = jnp.where(head0, 1.0, 0.0).astype(BF16)
    keep1 = jnp.where(head0, 0.0, 1.0).astype(BF16)
    ones = jnp.ones((two, LANES), BF16)

    for i in range(q_ref.shape[0] // ATTN_BLOCK):
        mask = mask_first if i == 0 else mask_rest
        r0 = i * ATTN_BLOCK
        for p in range(q_ref.shape[1] // LANES):
            c0 = p * LANES
            q2 = q_ref[r0:r0 + ATTN_BLOCK, c0:c0 + LANES]
            qs = jnp.concatenate([q2 * keep0, q2 * keep1], axis=0)
            kk = window(k_ref, kp_ref, r0, c0)
            s = lax.dot_general(qs, kk, (((1,), (1,)), ((), ())),
                                preferred_element_type=F32)
            s = s + mask
            m = jnp.max(s, axis=-1, keepdims=True)
            e = jnp.exp2((s - m).astype(BF16))
            vv = jnp.concatenate([window(v_ref, vp_ref, r0, c0), ones], axis=1)
            pv = jnp.dot(e, vv, preferred_element_type=F32)
            pick = lambda x: jnp.where(head0, x[:ATTN_BLOCK], x[ATTN_BLOCK:])
            den = pick(pv[:, LANES:])
            o_ref[r0:r0 + ATTN_BLOCK, c0:c0 + LANES] = pick(pv[:, :LANES]) * (1.0 / den)
            l_ref[r0:r0 + ATTN_BLOCK, c0:c0 + LANES] = pick(m) + jnp.log2(den)


def _attn_kernel(*refs, plans):
    refs = iter(refs)
    inputs = []
    for has_prev, _ in plans:
        q_ref, k_ref = next(refs), next(refs)
        kp_ref = next(refs) if has_prev else None
        v_ref = next(refs)
        vp_ref = next(refs) if has_prev else None
        inputs.append((q_ref, k_ref, kp_ref, v_ref, vp_ref))
    for (has_prev, row_steps), group_inputs in zip(plans, inputs):
        first_step = (pl.program_id(1) % row_steps == 0) if has_prev else True
        _attend(*group_inputs, next(refs), next(refs), first_step)


def _attention(qkv, batch, seq):
    in_specs, operands, out_specs, out_shapes, plans, steps = [], [], [], [], [], set()
    for qkv_g, dilation in zip(qkv, DILATIONS):
        m_len = seq // dilation
        mb = min(m_len, ATTN_ROWS_MAX)
        nblk = mb // ATTN_BLOCK
        n_res = min(dilation, ATTN_ROWS_MAX // mb)
        width = n_res * GROUP_WIDTH
        row_steps = m_len // mb
        has_prev = row_steps > 1
        steps.add(row_steps * (dilation // n_res))
        view = qkv_g.reshape(3, batch, m_len, dilation * GROUP_WIDTH)

        def rows(which, row_steps=row_steps, mb=mb, width=width):
            return pl.BlockSpec((None, None, mb, width),
                                lambda b, t: (which, b, t % row_steps, t // row_steps))

        def prev(which, row_steps=row_steps, nblk=nblk, width=width):
            return pl.BlockSpec(
                (None, None, ATTN_BLOCK, width),
                lambda b, t: (which, b, jnp.maximum((t % row_steps) * nblk - 1, 0), t // row_steps))

        if has_prev:
            specs = [rows(0), rows(1), prev(1), rows(2), prev(2)]
        else:
            specs = [rows(0), rows(1), rows(2)]
        in_specs += specs
        operands += [view] * len(specs)
        out_spec = pl.BlockSpec((None, mb, width),
                                lambda b, t, row_steps=row_steps: (b, t % row_steps, t // row_steps))
        out_specs += [out_spec, out_spec]
        out_shapes += [jax.ShapeDtypeStruct((batch, m_len, dilation * GROUP_WIDTH), F32)] * 2
        plans.append((has_prev, row_steps))
    (n_steps,) = steps
    outs = pl.pallas_call(
        functools.partial(_attn_kernel, plans=tuple(plans)),
        grid=(batch, n_steps),
        in_specs=in_specs,
        out_specs=out_specs,
        out_shape=out_shapes,
        compiler_params=pltpu.CompilerParams(
            dimension_semantics=("arbitrary", "arbitrary"), vmem_limit_bytes=VMEM_LIMIT_BYTES),
        name="dilated_attn",
    )(*operands)
    flat = [x.reshape(-1, x.shape[-1]) for x in outs]
    return list(zip(flat[0::2], flat[1::2]))


def _mix_kernel(h_ref, g_ref, o0_ref, o1_ref, o2_ref, l0_ref, l1_ref, l2_ref,
                wr_ref, cw_ref, lng_ref, lnb_ref, sgw_ref, sgb_ref,
                wa_ref, wb_ref, wc_ref, wo_ref, out_ref, carry_ref, yc_ref, nat_ref,
                *, tiles_per_seq, conv_w, sg_w):
    tm, d_model = h_ref.shape
    halves = GROUP_WIDTH // LANES
    w_skip = 3 * len(DILATIONS) * GROUP_WIDTH

    @pl.when(pl.program_id(0) % tiles_per_seq == 0)
    def _():
        carry_ref[...] = jnp.zeros_like(carry_ref)

    att_refs = (o0_ref, l0_ref, o1_ref, l1_ref, o2_ref, l2_ref)
    for slot, src_ref in enumerate(att_refs):
        d = DILATIONS[slot // 2]
        for r in range(d if d > 1 else 0):
            for half in range(halves):
                l0 = r * GROUP_WIDTH + half * LANES
                nat_ref[slot * halves + half, pl.ds(r, tm // d, stride=d), :] = src_ref[:, l0:l0 + LANES]

    def natural(slot, rs):
        if DILATIONS[slot // 2] == 1:
            return att_refs[slot][rs, :]
        return jnp.concatenate([nat_ref[slot * halves + half, rs, :] for half in range(halves)], axis=1)

    c_b, c_c, c_u, c_v, c_g = conv_w, 2 * conv_w, 3 * conv_w, 3 * conv_w + sg_w, 3 * conv_w + 2 * sg_w
    ti = lax.broadcasted_iota(jnp.int32, (SG_CHUNK, SG_CHUNK), 0)
    si = lax.broadcasted_iota(jnp.int32, (SG_CHUNK, SG_CHUNK), 1)
    gch = sg_w // SG_GROUPS
    w_sg = [jnp.where(ti >= si, sgw_ref[g], 0.0).astype(BF16) for g in range(SG_GROUPS)]
    cw = cw_ref[...]
    carry = carry_ref[...]

    for s in range(tm // MIX_SUB_ROWS):
        r0 = s * MIX_SUB_ROWS
        rs = slice(r0, r0 + MIX_SUB_ROWS)
        h = h_ref[rs, :]
        a = _rms_norm(h, g_ref[...]).astype(BF16)

        def project(c0, c1):
            return jnp.dot(a, wr_ref[:, w_skip + c0:w_skip + c1], preferred_element_type=F32)

        z_conv = project(0, c_u)
        z_sg = project(c_u, c_g)

        zx, zb, zc = z_conv[:, :c_b], z_conv[:, c_b:c_c], z_conv[:, c_c:c_u]
        u = zc * zx
        ext = jnp.concatenate([carry, u], axis=0)
        u1 = pltpu.roll(ext, 1, 0)[SUBLANES:]
        u2 = pltpu.roll(ext, 2, 0)[SUBLANES:]
        carry = u[MIX_SUB_ROWS - SUBLANES:]
        yb = (zb * (cw[0:1] * u2 + cw[1:2] * u1 + cw[2:3] * u)).astype(BF16)

        z_g0 = project(c_g, c_g + d_model)

        l0, l1, l2 = natural(1, rs), natural(3, rs), natural(5, rs)
        mx = jnp.maximum(jnp.maximum(l0, l1), l2)
        e0, e1, e2 = jnp.exp2(l0 - mx), jnp.exp2(l1 - mx), jnp.exp2(l2 - mx)
        inv = 1.0 / (e0 + e1 + e2)
        ya = jnp.concatenate([natural(0, rs) * (e0 * inv), natural(2, rs) * (e1 * inv),
                              natural(4, rs) * (e2 * inv)], axis=1).astype(BF16)
        pa = jnp.dot(ya, wa_ref[...], preferred_element_type=F32)

        us = _gelu(z_sg[:, :sg_w])
        vs = _gelu(z_sg[:, sg_w:])
        mu = jnp.mean(vs, axis=-1, keepdims=True)
        xc = vs - mu
        var = jnp.mean(xc * xc, axis=-1, keepdims=True)
        vln = (xc * lax.rsqrt(var + LN_EPS) * lng_ref[...] + lnb_ref[...]).astype(BF16)

        z_g1 = project(c_g + d_model, c_g + 2 * d_model)
        pb = jnp.dot(yb, wb_ref[...], preferred_element_type=F32)
        m = _sigmoid(z_g0) * pa + _sigmoid(z_g1) * pb

        for g in range(SG_GROUPS):
            bias = sgb_ref[g]
            cs = slice(g * gch, (g + 1) * gch)
            for n in range(MIX_SUB_ROWS // SG_CHUNK):
                ns = slice(n * SG_CHUNK, (n + 1) * SG_CHUNK)
                sv = jnp.dot(w_sg[g], vln[ns, cs], preferred_element_type=F32) + bias
                yc_ref[r0 + n * SG_CHUNK:r0 + (n + 1) * SG_CHUNK, cs] = (us[ns, cs] * sv).astype(BF16)

        z_g2 = project(c_g + 2 * d_model, c_g + 3 * d_model)
        pc = jnp.dot(yc_ref[rs, :], wc_ref[...], preferred_element_type=F32)
        m = m + _sigmoid(z_g2) * pc
        out_ref[rs, :] = h + jnp.dot(m.astype(BF16), wo_ref[...], preferred_element_type=F32)

    carry_ref[...] = carry


def _mixer_merge(layer, h2d, g, att, w_in, conv_w, ln_g, ln_b, sg_w, sg_b, w_a, w_b, w_c, w_o, seq):
    n_tok, d_model = h2d.shape
    tm = MIX_ROWS
    conv_width = conv_w.shape[-1]
    sg_width = ln_g.shape[-1]
    row_tile = lambda width: pl.BlockSpec((tm, width), lambda i: (i, 0))
    (o0, l0), (o1, l1), (o2, l2) = att
    att_tiles = [pl.BlockSpec((tm // d, d * GROUP_WIDTH), lambda i: (i, 0)) for d in DILATIONS]
    kern = functools.partial(_mix_kernel, tiles_per_seq=seq // tm, conv_w=conv_width, sg_w=sg_width)
    return pl.pallas_call(
        kern,
        grid=(n_tok // tm,),
        in_specs=[row_tile(d_model), _resident(g, layer)]
        + att_tiles * 2
        + [_resident(x, layer) for x in (w_in, conv_w, ln_g, ln_b, sg_w, sg_b, w_a, w_b, w_c, w_o)],
        out_specs=row_tile(d_model),
        out_shape=jax.ShapeDtypeStruct((n_tok, d_model), F32),
        scratch_shapes=[pltpu.VMEM((SUBLANES, conv_width), F32),
                        pltpu.VMEM((tm, sg_width), BF16),
                        pltpu.VMEM((2 * len(DILATIONS) * (GROUP_WIDTH // LANES), tm, LANES), F32)],
        compiler_params=pltpu.CompilerParams(
            dimension_semantics=("arbitrary",), vmem_limit_bytes=VMEM_LIMIT_BYTES),
        name="mixer_merge",
    )(h2d, g, o0, o1, o2, l0, l1, l2, w_in, conv_w, ln_g, ln_b, sg_w, sg_b, w_a, w_b, w_c, w_o)


def _mlp_kernel(h_ref, gm_ref, wu_ref, wd_ref, p_ref, gp_ref, wg_ref, wp_ref, gf_ref, out_ref,
                *, final_norm):
    d_ff = wu_ref.shape[1]

    for s in range(h_ref.shape[0] // MLP_SUB_ROWS):
        rs = slice(s * MLP_SUB_ROWS, (s + 1) * MLP_SUB_ROWS)
        h = h_ref[rs, :]
        c = _rms_norm(h, gm_ref[...]).astype(BF16)
        acc = h
        for f0 in range(0, d_ff, MLP_FF_TILE):
            t = jnp.dot(c, wu_ref[:, f0:f0 + MLP_FF_TILE], preferred_element_type=F32)
            t = jnp.square(jnp.maximum(t, 0.0)).astype(BF16)
            acc = acc + jnp.dot(t, wd_ref[f0:f0 + MLP_FF_TILE, :], preferred_element_type=F32)
        e = _rms_norm(acc, gp_ref[...]).astype(BF16)
        gate = _sigmoid(jnp.dot(e, wg_ref[...], preferred_element_type=F32))
        emb = jnp.dot(p_ref[rs, :].astype(BF16), wp_ref[...], preferred_element_type=F32)
        out = acc + gate * emb
        out_ref[rs, :] = _rms_norm(out, gf_ref[...]) if final_norm else out


def _mlp_ple(layer, h2d, g_mlp, w_up, w_down, p3d, g_ple, w_pg, w_pe, g_final, final_norm):
    n_tok, d_model = h2d.shape
    tm = MLP_ROWS
    return pl.pallas_call(
        functools.partial(_mlp_kernel, final_norm=final_norm),
        grid=(n_tok // tm,),
        in_specs=[
            pl.BlockSpec((tm, d_model), lambda i: (i, 0)),
            _resident(g_mlp, layer),
            _resident(w_up, layer),
            _resident(w_down, layer),
            pl.BlockSpec((None, tm, p3d.shape[-1]), lambda i: (layer, i, 0)),
            _resident(g_ple, layer),
            _resident(w_pg, layer),
            _resident(w_pe, layer),
            _resident(g_final, 0),
        ],
        out_specs=pl.BlockSpec((tm, d_model), lambda i: (i, 0)),
        out_shape=jax.ShapeDtypeStruct((n_tok, d_model), F32),
        compiler_params=pltpu.CompilerParams(
            dimension_semantics=("arbitrary",), vmem_limit_bytes=VMEM_LIMIT_BYTES),
        name="mlp_ple",
    )(h2d, g_mlp, w_up, w_down, p3d, g_ple, w_pg, w_pe, g_final)


def _rotary_tables(positions):
    inv_freq = ROPE_THETA ** (-(jnp.arange(0, 2 * ROT_HALF, 2, dtype=F32) / (2 * ROT_HALF)))
    in_head = jnp.arange(LANES) % HEAD_DIM
    rotated = in_head < 2 * ROT_HALF
    freq = jnp.where(rotated, inv_freq[in_head % ROT_HALF], 0.0)
    sign = jnp.where(in_head < ROT_HALF, -1.0, 1.0).astype(F32)
    ang = positions.astype(F32).reshape(-1, 1) * freq
    return jnp.cos(ang), sign * jnp.sin(ang)


def kernel(x, p, positions, norm_mix_g, w_in, conv_w, sg_ln_g, sg_ln_b, sg_w, sg_b,
           w_branch_a, w_branch_b, w_branch_c, w_out, norm_mlp_g, w_up, w_down,
           norm_ple_g, w_ple_gate, w_ple_proj, norm_final_g):
    batch, seq, d_model = x.shape
    depth = w_in.shape[0]
    n_tok = batch * seq
    qkv_width = 3 * len(DILATIONS) * GROUP_WIDTH
    assert seq % (max(DILATIONS) * ATTN_BLOCK) == 0 and seq % ATTN_ROWS_MAX == 0
    assert n_tok % MLP_ROWS == 0 and seq % MIX_ROWS == 0

    cos_t, sin_t = _rotary_tables(positions)
    rows = lambda v: v.reshape(v.shape[0], 1, v.shape[-1])
    bf = lambda w: w.astype(BF16)
    w_in_b = bf(w_in)
    sg_bias = jnp.broadcast_to(sg_b[:, :, :, None], sg_w.shape)
    w_a, w_b, w_c, w_o = bf(w_branch_a), bf(w_branch_b), bf(w_branch_c), bf(w_out)
    w_u, w_d, w_pg, w_pe = bf(w_up), bf(w_down), bf(w_ple_gate), bf(w_ple_proj)
    g_mix, g_mlp, g_ple = rows(norm_mix_g), rows(norm_mlp_g), rows(norm_ple_g)
    ln_g, ln_b = rows(sg_ln_g), rows(sg_ln_b)
    p3d = p.reshape(depth, n_tok, p.shape[-1])

    g_final = norm_final_g.reshape(1, 1, -1)

    h = x.reshape(n_tok, d_model)
    for i in range(depth):
        qkv = _qkv_projection(i, h, g_mix, w_in_b, cos_t, sin_t)
        att = _attention(qkv, batch, seq)
        h = _mixer_merge(i, h, g_mix, att, w_in_b, conv_w, ln_g, ln_b, sg_w, sg_bias,
                         w_a, w_b, w_c, w_o, seq)
        h = _mlp_ple(i, h, g_mlp, w_u, w_d, p3d, g_ple, w_pg, w_pe, g_final,
                     final_norm=(i == depth - 1))
    return h.reshape(batch, seq, d_model)
```

```python
import functools
import math

import jax
import jax.numpy as jnp
from jax import lax
from jax.experimental import pallas as pl
from jax.experimental.pallas import tpu as pltpu

F32 = jnp.float32
BF16 = jnp.bfloat16

HEAD_DIM = 64
HEADS_PER_GROUP = 4
GROUP_WIDTH = HEADS_PER_GROUP * HEAD_DIM
DILATIONS = (1, 4, 16)
ATTN_BLOCK = 128
ROT_HALF = 8
ROPE_THETA = 500000.0
CONV_K = 3
SG_CHUNK = 128
SG_GROUPS = 4
RMS_EPS = 1e-6
LN_EPS = 1e-5
QUERY_SCALE = math.log2(math.e) * HEAD_DIM ** -0.5

LANES = 128
SUBLANES = 8
BF16_SUBLANES = 16
VMEM_LIMIT_BYTES = 56 * 1024 * 1024

QKV_ROWS = 1024
QKV_SUB_ROWS = 256
ATTN_ROWS_MAX = 1024
MIX_ROWS = 512
MIX_SUB_ROWS = 256
MLP_ROWS = 512
MLP_SUB_ROWS = 512
MLP_FF_TILE = 1024


def _rms_norm(h, g):
    ms = jnp.mean(h * h, axis=-1, keepdims=True)
    return h * lax.rsqrt(ms + RMS_EPS) * g


def _sigmoid(x):
    return 1.0 / (1.0 + jnp.exp(-x))


def _gelu(x):
    return 0.5 * x * (1.0 + lax.erf(x * (1.0 / math.sqrt(2.0))))


def _resident(stacked, layer):
    shape = stacked.shape[1:]
    index = (layer,) + (0,) * len(shape)
    return pl.BlockSpec((None,) + shape, lambda *_: index, pipeline_mode=pl.Buffered(1))


def _qkv_kernel(h_ref, g_ref, w_ref, cos_ref, sin_ref, *refs):
    outs, slab_ref = refs[:-1], refs[-1]
    n_groups = len(outs)
    sub = QKV_SUB_ROWS
    lane = lax.broadcasted_iota(jnp.int32, (sub, LANES), 1)
    low_half = (lane % HEAD_DIM) < ROT_HALF
    slab = 0
    for s in range(h_ref.shape[0] // sub):
        rs = slice(s * sub, (s + 1) * sub)
        a = _rms_norm(h_ref[rs, :], g_ref[...]).astype(BF16)
        cos_k, sin_k = cos_ref[rs, :], sin_ref[rs, :]
        cos_sin = ((cos_k * QUERY_SCALE, sin_k * QUERY_SCALE), (cos_k, sin_k))
        for which in range(3):
            for g, d in enumerate(DILATIONS):
                c0 = (which * n_groups + g) * GROUP_WIDTH
                z = jnp.dot(a, w_ref[:, c0:c0 + GROUP_WIDTH], preferred_element_type=F32)
                out_rows = slice(s * sub // d, (s + 1) * sub // d)
                for half in range(GROUP_WIDTH // LANES):
                    t = z[:, half * LANES:(half + 1) * LANES]
                    if which < 2:
                        partner = jnp.where(low_half,
                                            pltpu.roll(t, LANES - ROT_HALF, 1),
                                            pltpu.roll(t, ROT_HALF, 1))
                        cos_t, sin_t = cos_sin[which]
                        t = t * cos_t + partner * sin_t
                    if d == 1:
                        outs[g][which, out_rows, half * LANES:(half + 1) * LANES] = t.astype(BF16)
                    else:
                        slab_ref[slab] = t
                        for r in range(d):
                            l0 = r * GROUP_WIDTH + half * LANES
                            rows = slab_ref[slab, pl.ds(r, sub // d, stride=d), :]
                            outs[g][which, out_rows, l0:l0 + LANES] = rows.astype(BF16)
                        slab += 1


def _qkv_projection(layer, h2d, g, w_in, cos_t, sin_t):
    n_tok, d_model = h2d.shape
    tm = QKV_ROWS
    qkv_width = 3 * len(DILATIONS) * GROUP_WIDTH
    n_slabs = 3 * (GROUP_WIDTH // LANES) * sum(d > 1 for d in DILATIONS) * (tm // QKV_SUB_ROWS)
    row_tile = pl.BlockSpec((tm, d_model), lambda i: (i, 0))
    out_specs = [pl.BlockSpec((3, tm // d, d * GROUP_WIDTH), lambda i: (0, i, 0)) for d in DILATIONS]
    out_shape = [jax.ShapeDtypeStruct((3, n_tok // d, d * GROUP_WIDTH), BF16) for d in DILATIONS]
    return pl.pallas_call(
        _qkv_kernel,
        grid=(n_tok // tm,),
        in_specs=[row_tile, _resident(g, layer)] + [
            pl.BlockSpec((None, d_model, qkv_width), lambda i: (0, 0, 0),
                         pipeline_mode=pl.Buffered(1)),
            pl.BlockSpec((tm, LANES), lambda i: (i, 0)),
            pl.BlockSpec((tm, LANES), lambda i: (i, 0)),
        ],
        out_specs=out_specs,
        out_shape=out_shape,
        scratch_shapes=[pltpu.VMEM((n_slabs, QKV_SUB_ROWS, LANES), F32)],
        compiler_params=pltpu.CompilerParams(
            dimension_semantics=("arbitrary",), vmem_limit_bytes=VMEM_LIMIT_BYTES),
        name="qkv_proj",
    )(h2d, g, w_in, cos_t, sin_t)


def _attend(q_ref, k_ref, kp_ref, v_ref, vp_ref, o_ref, l_ref, first_step):
    def window(ref, prev_ref, r0, c0):
        if r0 > 0:
            return ref[r0 - ATTN_BLOCK:r0 + ATTN_BLOCK, c0:c0 + LANES]
        cur = ref[0:ATTN_BLOCK, c0:c0 + LANES]
        before = jnp.zeros_like(cur) if prev_ref is None else prev_ref[:, c0:c0 + LANES]
        return jnp.concatenate([before, cur], axis=0)

    two = 2 * ATTN_BLOCK
    qi = lax.broadcasted_iota(jnp.int32, (two, two), 0) % ATTN_BLOCK
    kj = lax.broadcasted_iota(jnp.int32, (two, two), 1)
    dist = qi + ATTN_BLOCK - kj
    band = (dist >= 0) & (dist <= ATTN_BLOCK)
    band_first = band & ((kj >= ATTN_BLOCK) | jnp.logical_not(first_step))
    mask_rest = jnp.where(band, 0.0, -jnp.inf)
    mask_first = jnp.where(band_first, 0.0, -jnp.inf)
    lane = lax.broadcasted_iota(jnp.int32, (ATTN_BLOCK, LANES), 1)
    head0 = lane < HEAD_DIM
    keep0 = jnp.where(head0, 1.0, 0.0).astype(BF16)
    keep1 = jnp.where(head0, 0.0, 1.0).astype(BF16)
    ones = jnp.ones((two, LANES), BF16)

    for i in range(q_ref.shape[0] // ATTN_BLOCK):
        mask = mask_first if i == 0 else mask_rest
        r0 = i * ATTN_BLOCK
        for p in range(q_ref.shape[1] // LANES):
            c0 = p * LANES
            q2 = q_ref[r0:r0 + ATTN_BLOCK, c0:c0 + LANES]
            qs = jnp.concatenate([q2 * keep0, q2 * keep1], axis=0)
            kk = window(k_ref, kp_ref, r0, c0)
            s = lax.dot_general(qs, kk, (((1,), (1,)), ((), ())),
                                preferred_element_type=F32)
            s = s + mask
            m = jnp.max(s, axis=-1, keepdims=True)
            e = jnp.exp2((s - m).astype(BF16))
            vv = jnp.concatenate([window(v_ref, vp_ref, r0, c0), ones], axis=1)
            pv = jnp.dot(e, vv, preferred_element_type=F32)
            pick = lambda x: jnp.where(head0, x[:ATTN_BLOCK], x[ATTN_BLOCK:])
            den = pick(pv[:, LANES:])
            o_ref[r0:r0 + ATTN_BLOCK, c0:c0 + LANES] = pick(pv[:, :LANES]) * (1.0 / den)
            l_ref[r0:r0 + ATTN_BLOCK, c0:c0 + LANES] = pick(m) + jnp.log2(den)


def _attn_kernel(*refs, plans):
    refs = iter(refs)
    inputs = []
    for has_prev, _ in plans:
        q_ref, k_ref = next(refs), next(refs)
        kp_ref = next(refs) if has_prev else None
        v_ref = next(refs)
        vp_ref = next(refs) if has_prev else None
        inputs.append((q_ref, k_ref, kp_ref, v_ref, vp_ref))
    for (has_prev, row_steps), group_inputs in zip(plans, inputs):
        first_step = (pl.program_id(1) % row_steps == 0) if has_prev else True
        _attend(*group_inputs, next(refs), next(refs), first_step)


def _attention(qkv, batch, seq):
    in_specs, operands, out_specs, out_shapes, plans, steps = [], [], [], [], [], set()
    for qkv_g, dilation in zip(qkv, DILATIONS):
        m_len = seq // dilation
        mb = min(m_len, ATTN_ROWS_MAX)
        nblk = mb // ATTN_BLOCK
        n_res = min(dilation, ATTN_ROWS_MAX // mb)
        width = n_res * GROUP_WIDTH
        row_steps = m_len // mb
        has_prev = row_steps > 1
        steps.add(row_steps * (dilation // n_res))
        view = qkv_g.reshape(3, batch, m_len, dilation * GROUP_WIDTH)

        def rows(which, row_steps=row_steps, mb=mb, width=width):
            return pl.BlockSpec((None, None, mb, width),
                                lambda b, t: (which, b, t % row_steps, t // row_steps))

        def prev(which, row_steps=row_steps, nblk=nblk, width=width):
            return pl.BlockSpec(
                (None, None, ATTN_BLOCK, width),
                lambda b, t: (which, b, jnp.maximum((t % row_steps) * nblk - 1, 0), t // row_steps))

        if has_prev:
            specs = [rows(0), rows(1), prev(1), rows(2), prev(2)]
        else:
            specs = [rows(0), rows(1), rows(2)]
        in_specs += specs
        operands += [view] * len(specs)
        out_spec = pl.BlockSpec((None, mb, width),
                                lambda b, t, row_steps=row_steps: (b, t % row_steps, t // row_steps))
        out_specs += [out_spec, out_spec]
        out_shapes += [jax.ShapeDtypeStruct((batch, m_len, dilation * GROUP_WIDTH), F32)] * 2
        plans.append((has_prev, row_steps))
    (n_steps,) = steps
    outs = pl.pallas_call(
        functools.partial(_attn_kernel, plans=tuple(plans)),
        grid=(batch, n_steps),
        in_specs=in_specs,
        out_specs=out_specs,
        out_shape=out_shapes,
        compiler_params=pltpu.CompilerParams(
            dimension_semantics=("arbitrary", "arbitrary"), vmem_limit_bytes=VMEM_LIMIT_BYTES),
        name="dilated_attn",
    )(*operands)
    flat = [x.reshape(-1, x.shape[-1]) for x in outs]
    return list(zip(flat[0::2], flat[1::2]))


def _mix_kernel(h_ref, g_ref, o0_ref, o1_ref, o2_ref, l0_ref, l1_ref, l2_ref,
                wr_ref, cw_ref, lng_ref, lnb_ref, sgw_ref, sgb_ref,
                wa_ref, wb_ref, wc_ref, wo_ref, out_ref, carry_ref, yc_ref, nat_ref,
                *, tiles_per_seq, conv_w, sg_w):
    tm, d_model = h_ref.shape
    halves = GROUP_WIDTH // LANES
    w_skip = 3 * len(DILATIONS) * GROUP_WIDTH

    @pl.when(pl.program_id(0) % tiles_per_seq == 0)
    def _():
        carry_ref[...] = jnp.zeros_like(carry_ref)

    att_refs = (o0_ref, l0_ref, o1_ref, l1_ref, o2_ref, l2_ref)
    for slot, src_ref in enumerate(att_refs):
        d = DILATIONS[slot // 2]
        for r in range(d if d > 1 else 0):
            for half in range(halves):
                l0 = r * GROUP_WIDTH + half * LANES
                nat_ref[slot * halves + half, pl.ds(r, tm // d, stride=d), :] = src_ref[:, l0:l0 + LANES]

    def natural(slot, rs):
        if DILATIONS[slot // 2] == 1:
            return att_refs[slot][rs, :]
        return jnp.concatenate([nat_ref[slot * halves + half, rs, :] for half in range(halves)], axis=1)

    c_b, c_c, c_u, c_v, c_g = conv_w, 2 * conv_w, 3 * conv_w, 3 * conv_w + sg_w, 3 * conv_w + 2 * sg_w
    ti = lax.broadcasted_iota(jnp.int32, (SG_CHUNK, SG_CHUNK), 0)
    si = lax.broadcasted_iota(jnp.int32, (SG_CHUNK, SG_CHUNK), 1)
    gch = sg_w // SG_GROUPS
    w_sg = [jnp.where(ti >= si, sgw_ref[g], 0.0).astype(BF16) for g in range(SG_GROUPS)]
    cw = cw_ref[...]
    carry = carry_ref[...]

    for s in range(tm // MIX_SUB_ROWS):
        r0 = s * MIX_SUB_ROWS
        rs = slice(r0, r0 + MIX_SUB_ROWS)
        h = h_ref[rs, :]
        a = _rms_norm(h, g_ref[...]).astype(BF16)

        def project(c0, c1):
            return jnp.dot(a, wr_ref[:, w_skip + c0:w_skip + c1], preferred_element_type=F32)

        z_conv = project(0, c_u)
        z_sg = project(c_u, c_g)

        zx, zb, zc = z_conv[:, :c_b], z_conv[:, c_b:c_c], z_conv[:, c_c:c_u]
        u = zc * zx
        ext = jnp.concatenate([carry, u], axis=0)
        u1 = pltpu.roll(ext, 1, 0)[SUBLANES:]
        u2 = pltpu.roll(ext, 2, 0)[SUBLANES:]
        carry = u[MIX_SUB_ROWS - SUBLANES:]
        yb = (zb * (cw[0:1] * u2 + cw[1:2] * u1 + cw[2:3] * u)).astype(BF16)

        z_g0 = project(c_g, c_g + d_model)

        l0, l1, l2 = natural(1, rs), natural(3, rs), natural(5, rs)
        mx = jnp.maximum(jnp.maximum(l0, l1), l2)
        e0, e1, e2 = jnp.exp2(l0 - mx), jnp.exp2(l1 - mx), jnp.exp2(l2 - mx)
        inv = 1.0 / (e0 + e1 + e2)
        ya = jnp.concatenate([natural(0, rs) * (e0 * inv), natural(2, rs) * (e1 * inv),
                              natural(4, rs) * (e2 * inv)], axis=1).astype(BF16)
        pa = jnp.dot(ya, wa_ref[...], preferred_element_type=F32)

        us = _gelu(z_sg[:, :sg_w])
        vs = _gelu(z_sg[:, sg_w:])
        mu = jnp.mean(vs, axis=-1, keepdims=True)
        xc = vs - mu
        var = jnp.mean(xc * xc, axis=-1, keepdims=True)
        vln = (xc * lax.rsqrt(var + LN_EPS) * lng_ref[...] + lnb_ref[...]).astype(BF16)

        z_g1 = project(c_g + d_model, c_g + 2 * d_model)
        pb = jnp.dot(yb, wb_ref[...], preferred_element_type=F32)
        m = _sigmoid(z_g0) * pa + _sigmoid(z_g1) * pb

        for g in range(SG_GROUPS):
            bias = sgb_ref[g]
            cs = slice(g * gch, (g + 1) * gch)
            for n in range(MIX_SUB_ROWS // SG_CHUNK):
                ns = slice(n * SG_CHUNK, (n + 1) * SG_CHUNK)
                sv = jnp.dot(w_sg[g], vln[ns, cs], preferred_element_type=F32) + bias
                yc_ref[r0 + n * SG_CHUNK:r0 + (n + 1) * SG_CHUNK, cs] = (us[ns, cs] * sv).astype(BF16)

        z_g2 = project(c_g + 2 * d_model, c_g + 3 * d_model)
        pc = jnp.dot(yc_ref[rs, :], wc_ref[...], preferred_element_type=F32)
        m = m + _sigmoid(z_g2) * pc
        out_ref[rs, :] = h + jnp.dot(m.astype(BF16), wo_ref[...], preferred_element_type=F32)

    carry_ref[...] = carry


def _mixer_merge(layer, h2d, g, att, w_in, conv_w, ln_g, ln_b, sg_w, sg_b, w_a, w_b, w_c, w_o, seq):
    n_tok, d_model = h2d.shape
    tm = MIX_ROWS
    conv_width = conv_w.shape[-1]
    sg_width = ln_g.shape[-1]
    row_tile = lambda width: pl.BlockSpec((tm, width), lambda i: (i, 0))
    (o0, l0), (o1, l1), (o2, l2) = att
    att_tiles = [pl.BlockSpec((tm // d, d * GROUP_WIDTH), lambda i: (i, 0)) for d in DILATIONS]
    kern = functools.partial(_mix_kernel, tiles_per_seq=seq // tm, conv_w=conv_width, sg_w=sg_width)
    return pl.pallas_call(
        kern,
        grid=(n_tok // tm,),
        in_specs=[row_tile(d_model), _resident(g, layer)]
        + att_tiles * 2
        + [_resident(w_in, 0)]
        + [_resident(x, layer) for x in (conv_w, ln_g, ln_b, sg_w, sg_b, w_a, w_b, w_c, w_o)],
        out_specs=row_tile(d_model),
        out_shape=jax.ShapeDtypeStruct((n_tok, d_model), F32),
        scratch_shapes=[pltpu.VMEM((SUBLANES, conv_width), F32),
                        pltpu.VMEM((tm, sg_width), BF16),
                        pltpu.VMEM((2 * len(DILATIONS) * (GROUP_WIDTH // LANES), tm, LANES), F32)],
        compiler_params=pltpu.CompilerParams(
            dimension_semantics=("arbitrary",), vmem_limit_bytes=VMEM_LIMIT_BYTES),
        name="mixer_merge",
    )(h2d, g, o0, o1, o2, l0, l1, l2, w_in, conv_w, ln_g, ln_b, sg_w, sg_b, w_a, w_b, w_c, w_o)


def _mlp_kernel(h_ref, gm_ref, wu_ref, wd_ref, p_ref, gp_ref, wg_ref, wp_ref, gf_ref, *refs,
                final_norm):
    n_cast = len(refs) // 2
    out_ref = refs[n_cast]
    for src_ref, dst_ref in zip(refs[:n_cast], refs[n_cast + 1:]):
        dst_ref[...] = src_ref[...].astype(BF16)
    d_ff = wu_ref.shape[1]

    for s in range(h_ref.shape[0] // MLP_SUB_ROWS):
        rs = slice(s * MLP_SUB_ROWS, (s + 1) * MLP_SUB_ROWS)
        h = h_ref[rs, :]
        c = _rms_norm(h, gm_ref[...]).astype(BF16)
        acc = h
        for f0 in range(0, d_ff, MLP_FF_TILE):
            t = jnp.dot(c, wu_ref[:, f0:f0 + MLP_FF_TILE], preferred_element_type=F32)
            t = jnp.square(jnp.maximum(t, 0.0)).astype(BF16)
            acc = acc + jnp.dot(t, wd_ref[f0:f0 + MLP_FF_TILE, :], preferred_element_type=F32)
        e = _rms_norm(acc, gp_ref[...]).astype(BF16)
        gate = _sigmoid(jnp.dot(e, wg_ref[...], preferred_element_type=F32))
        emb = jnp.dot(p_ref[rs, :].astype(BF16), wp_ref[...], preferred_element_type=F32)
        out = acc + gate * emb
        out_ref[rs, :] = _rms_norm(out, gf_ref[...]) if final_norm else out


def _mlp_ple(layer, h2d, g_mlp, w_up, w_down, p3d, g_ple, w_pg, w_pe, g_final, next_weights):
    n_tok, d_model = h2d.shape
    tm = MLP_ROWS
    steps = n_tok // tm
    final_norm = not next_weights
    chunk = lambda w: (None, w.shape[1] // steps, w.shape[2])
    assert all(w.shape[1] % (steps * BF16_SUBLANES) == 0 for w in next_weights)
    row_tile = pl.BlockSpec((tm, d_model), lambda i: (i, 0))
    outs = pl.pallas_call(
        functools.partial(_mlp_kernel, final_norm=final_norm),
        grid=(steps,),
        in_specs=[
            row_tile,
            _resident(g_mlp, layer),
            _resident(w_up, 0),
            _resident(w_down, 0),
            pl.BlockSpec((None, tm, p3d.shape[-1]), lambda i: (layer, i, 0)),
            _resident(g_ple, layer),
            _resident(w_pg, layer),
            _resident(w_pe, layer),
            _resident(g_final, 0),
        ] + [pl.BlockSpec(chunk(w), lambda i: (layer + 1, i, 0)) for w in next_weights],
        out_specs=[row_tile] + [pl.BlockSpec(chunk(w), lambda i: (0, i, 0)) for w in next_weights],
        out_shape=[jax.ShapeDtypeStruct((n_tok, d_model), F32)]
        + [jax.ShapeDtypeStruct((1,) + w.shape[1:], BF16) for w in next_weights],
        compiler_params=pltpu.CompilerParams(
            dimension_semantics=("arbitrary",), vmem_limit_bytes=VMEM_LIMIT_BYTES),
        name="mlp_ple",
    )(h2d, g_mlp, w_up, w_down, p3d, g_ple, w_pg, w_pe, g_final, *next_weights)
    return outs[0], outs[1:]


def _rotary_tables(positions):
    inv_freq = ROPE_THETA ** (-(jnp.arange(0, 2 * ROT_HALF, 2, dtype=F32) / (2 * ROT_HALF)))
    n_tok = positions.size
    pack = LANES // (2 * ROT_HALF)
    ang = (positions.astype(F32).reshape(n_tok // pack, pack, 1) * inv_freq).reshape(n_tok // pack, -1)
    cos8 = jnp.cos(ang).reshape(n_tok, ROT_HALF)
    sin8 = jnp.sin(ang).reshape(n_tok, ROT_HALF)
    in_head = jnp.arange(LANES) % HEAD_DIM
    rotated = in_head < 2 * ROT_HALF
    sign = jnp.where(in_head < ROT_HALF, -1.0, 1.0).astype(F32)
    spread = lambda t: jnp.tile(t, (1, LANES // ROT_HALF))
    return (jnp.where(rotated, spread(cos8), 1.0), jnp.where(rotated, sign * spread(sin8), 0.0))


def kernel(x, p, positions, norm_mix_g, w_in, conv_w, sg_ln_g, sg_ln_b, sg_w, sg_b,
           w_branch_a, w_branch_b, w_branch_c, w_out, norm_mlp_g, w_up, w_down,
           norm_ple_g, w_ple_gate, w_ple_proj, norm_final_g):
    batch, seq, d_model = x.shape
    depth = w_in.shape[0]
    n_tok = batch * seq
    qkv_width = 3 * len(DILATIONS) * GROUP_WIDTH
    assert seq % (max(DILATIONS) * ATTN_BLOCK) == 0 and seq % ATTN_ROWS_MAX == 0
    assert n_tok % MLP_ROWS == 0 and seq % MIX_ROWS == 0

    cos_t, sin_t = _rotary_tables(positions)
    rows = lambda v: v.reshape(v.shape[0], 1, v.shape[-1])
    bf = lambda w: w.astype(BF16)
    sg_bias = jnp.broadcast_to(sg_b[:, :, :, None], sg_w.shape)
    w_a, w_b, w_c, w_o = bf(w_branch_a), bf(w_branch_b), bf(w_branch_c), bf(w_out)
    w_pg, w_pe = bf(w_ple_gate), bf(w_ple_proj)
    big_f32 = (w_in, w_up, w_down)
    w_in_l, w_up_l, w_down_l = (bf(w[:1]) for w in big_f32)
    g_mix, g_mlp, g_ple = rows(norm_mix_g), rows(norm_mlp_g), rows(norm_ple_g)
    ln_g, ln_b = rows(sg_ln_g), rows(sg_ln_b)
    p3d = p.reshape(depth, n_tok, p.shape[-1])

    g_final = norm_final_g.reshape(1, 1, -1)

    h = x.reshape(n_tok, d_model)
    for i in range(depth):
        qkv = _qkv_projection(i, h, g_mix, w_in_l, cos_t, sin_t)
        att = _attention(qkv, batch, seq)
        h = _mixer_merge(i, h, g_mix, att, w_in_l, conv_w, ln_g, ln_b, sg_w, sg_bias,
                         w_a, w_b, w_c, w_o, seq)
        h, nxt = _mlp_ple(i, h, g_mlp, w_up_l, w_down_l, p3d, g_ple, w_pg, w_pe, g_final,
                          big_f32 if i + 1 < depth else ())
        if nxt:
            w_in_l, w_up_l, w_down_l = nxt
    return h.reshape(batch, seq, d_model)
```

```python
import functools
import math

import jax
import jax.numpy as jnp
from jax import lax
from jax.experimental import pallas as pl
from jax.experimental.pallas import tpu as pltpu

F32 = jnp.float32
BF16 = jnp.bfloat16

HEAD_DIM = 64
HEADS_PER_GROUP = 4
GROUP_WIDTH = HEADS_PER_GROUP * HEAD_DIM
DILATIONS = (1, 4, 16)
ATTN_BLOCK = 128
ROT_HALF = 8
ROPE_THETA = 500000.0
CONV_K = 3
SG_CHUNK = 128
SG_GROUPS = 4
RMS_EPS = 1e-6
LN_EPS = 1e-5
QUERY_SCALE = math.log2(math.e) * HEAD_DIM ** -0.5

LANES = 128
SUBLANES = 8
BF16_SUBLANES = 16
VMEM_LIMIT_BYTES = 56 * 1024 * 1024

QKV_ROWS = 1024
QKV_SUB_ROWS = 256
ATTN_ROWS_MAX = 1024
MIX_ROWS = 512
MIX_SUB_ROWS = 256
MLP_ROWS = 512
MLP_SUB_ROWS = 512
MLP_FF_TILE = 1024
CAST_STEPS = 32


def _rms_norm(h, g):
    ms = jnp.mean(h * h, axis=-1, keepdims=True)
    return h * lax.rsqrt(ms + RMS_EPS) * g


def _sigmoid(x):
    return 1.0 / (1.0 + jnp.exp(-x))


def _gelu(x):
    return 0.5 * x * (1.0 + lax.erf(x * (1.0 / math.sqrt(2.0))))


def _resident(stacked, layer):
    shape = stacked.shape[1:]
    index = (layer,) + (0,) * len(shape)
    return pl.BlockSpec((None,) + shape, lambda *_: index, pipeline_mode=pl.Buffered(1))


def _qkv_kernel(h_ref, g_ref, w_ref, cos_ref, sin_ref, *refs):
    outs, slab_ref = refs[:-1], refs[-1]
    n_groups = len(outs)
    sub = QKV_SUB_ROWS
    lane = lax.broadcasted_iota(jnp.int32, (sub, LANES), 1)
    low_half = (lane % HEAD_DIM) < ROT_HALF
    slab = 0
    for s in range(h_ref.shape[0] // sub):
        rs = slice(s * sub, (s + 1) * sub)
        a = _rms_norm(h_ref[rs, :], g_ref[...]).astype(BF16)
        cos_k, sin_k = cos_ref[rs, :], sin_ref[rs, :]
        cos_sin = ((cos_k * QUERY_SCALE, sin_k * QUERY_SCALE), (cos_k, sin_k))
        for which in range(3):
            for g, d in enumerate(DILATIONS):
                c0 = (which * n_groups + g) * GROUP_WIDTH
                z = jnp.dot(a, w_ref[:, c0:c0 + GROUP_WIDTH], preferred_element_type=F32)
                out_rows = slice(s * sub // d, (s + 1) * sub // d)
                for half in range(GROUP_WIDTH // LANES):
                    t = z[:, half * LANES:(half + 1) * LANES]
                    if which < 2:
                        partner = jnp.where(low_half,
                                            pltpu.roll(t, LANES - ROT_HALF, 1),
                                            pltpu.roll(t, ROT_HALF, 1))
                        cos_t, sin_t = cos_sin[which]
                        t = t * cos_t + partner * sin_t
                    if d == 1:
                        outs[g][which, out_rows, half * LANES:(half + 1) * LANES] = t.astype(BF16)
                    else:
                        slab_ref[slab] = t
                        for r in range(d):
                            l0 = r * GROUP_WIDTH + half * LANES
                            rows = slab_ref[slab, pl.ds(r, sub // d, stride=d), :]
                            outs[g][which, out_rows, l0:l0 + LANES] = rows.astype(BF16)
                        slab += 1


def _qkv_projection(layer, h2d, g, w_in, cos_t, sin_t):
    n_tok, d_model = h2d.shape
    tm = QKV_ROWS
    qkv_width = 3 * len(DILATIONS) * GROUP_WIDTH
    n_slabs = 3 * (GROUP_WIDTH // LANES) * sum(d > 1 for d in DILATIONS) * (tm // QKV_SUB_ROWS)
    row_tile = pl.BlockSpec((tm, d_model), lambda i: (i, 0))
    out_specs = [pl.BlockSpec((3, tm // d, d * GROUP_WIDTH), lambda i: (0, i, 0)) for d in DILATIONS]
    out_shape = [jax.ShapeDtypeStruct((3, n_tok // d, d * GROUP_WIDTH), BF16) for d in DILATIONS]
    return pl.pallas_call(
        _qkv_kernel,
        grid=(n_tok // tm,),
        in_specs=[row_tile, _resident(g, layer)] + [
            pl.BlockSpec((None, d_model, qkv_width), lambda i: (0, 0, 0),
                         pipeline_mode=pl.Buffered(1)),
            pl.BlockSpec((tm, LANES), lambda i: (i, 0)),
            pl.BlockSpec((tm, LANES), lambda i: (i, 0)),
        ],
        out_specs=out_specs,
        out_shape=out_shape,
        scratch_shapes=[pltpu.VMEM((n_slabs, QKV_SUB_ROWS, LANES), F32)],
        compiler_params=pltpu.CompilerParams(
            dimension_semantics=("arbitrary",), vmem_limit_bytes=VMEM_LIMIT_BYTES),
        name="qkv_proj",
    )(h2d, g, w_in, cos_t, sin_t)


def _attend(q_ref, k_ref, kp_ref, v_ref, vp_ref, o_ref, l_ref, first_step):
    def window(ref, prev_ref, r0, c0):
        if r0 > 0:
            return ref[r0 - ATTN_BLOCK:r0 + ATTN_BLOCK, c0:c0 + LANES]
        cur = ref[0:ATTN_BLOCK, c0:c0 + LANES]
        before = jnp.zeros_like(cur) if prev_ref is None else prev_ref[:, c0:c0 + LANES]
        return jnp.concatenate([before, cur], axis=0)

    two = 2 * ATTN_BLOCK
    qi = lax.broadcasted_iota(jnp.int32, (two, two), 0) % ATTN_BLOCK
    kj = lax.broadcasted_iota(jnp.int32, (two, two), 1)
    dist = qi + ATTN_BLOCK - kj
    band = (dist >= 0) & (dist <= ATTN_BLOCK)
    band_first = band & ((kj >= ATTN_BLOCK) | jnp.logical_not(first_step))
    mask_rest = jnp.where(band, 0.0, -jnp.inf)
    mask_first = jnp.where(band_first, 0.0, -jnp.inf)
    lane = lax.broadcasted_iota(jnp.int32, (ATTN_BLOCK, LANES), 1)
    head0 = lane < HEAD_DIM
    keep0 = jnp.where(head0, 1.0, 0.0).astype(BF16)
    keep1 = jnp.where(head0, 0.0, 1.0).astype(BF16)
    ones = jnp.ones((two, LANES), BF16)

    for i in range(q_ref.shape[0] // ATTN_BLOCK):
        mask = mask_first if i == 0 else mask_rest
        r0 = i * ATTN_BLOCK
        for p in range(q_ref.shape[1] // LANES):
            c0 = p * LANES
            q2 = q_ref[r0:r0 + ATTN_BLOCK, c0:c0 + LANES]
            qs = jnp.concatenate([q2 * keep0, q2 * keep1], axis=0)
            kk = window(k_ref, kp_ref, r0, c0)
            s = lax.dot_general(qs, kk, (((1,), (1,)), ((), ())),
                                preferred_element_type=F32)
            s = s + mask
            m = jnp.max(s, axis=-1, keepdims=True)
            e = jnp.exp2((s - m).astype(BF16))
            vv = jnp.concatenate([window(v_ref, vp_ref, r0, c0), ones], axis=1)
            pv = jnp.dot(e, vv, preferred_element_type=F32)
            pick = lambda x: jnp.where(head0, x[:ATTN_BLOCK], x[ATTN_BLOCK:])
            den = pick(pv[:, LANES:])
            o_ref[r0:r0 + ATTN_BLOCK, c0:c0 + LANES] = pick(pv[:, :LANES]) * (1.0 / den)
            l_ref[r0:r0 + ATTN_BLOCK, c0:c0 + LANES] = pick(m) + jnp.log2(den)


def _attn_kernel(*refs, plans):
    refs = iter(refs)
    inputs = []
    for has_prev, _ in plans:
        q_ref, k_ref = next(refs), next(refs)
        kp_ref = next(refs) if has_prev else None
        v_ref = next(refs)
        vp_ref = next(refs) if has_prev else None
        inputs.append((q_ref, k_ref, kp_ref, v_ref, vp_ref))
    for (has_prev, row_steps), group_inputs in zip(plans, inputs):
        first_step = (pl.program_id(1) % row_steps == 0) if has_prev else True
        _attend(*group_inputs, next(refs), next(refs), first_step)


def _attention(qkv, batch, seq):
    in_specs, operands, out_specs, out_shapes, plans, steps = [], [], [], [], [], set()
    for qkv_g, dilation in zip(qkv, DILATIONS):
        m_len = seq // dilation
        mb = min(m_len, ATTN_ROWS_MAX)
        nblk = mb // ATTN_BLOCK
        n_res = min(dilation, ATTN_ROWS_MAX // mb)
        width = n_res * GROUP_WIDTH
        row_steps = m_len // mb
        has_prev = row_steps > 1
        steps.add(row_steps * (dilation // n_res))
        view = qkv_g.reshape(3, batch, m_len, dilation * GROUP_WIDTH)

        def rows(which, row_steps=row_steps, mb=mb, width=width):
            return pl.BlockSpec((None, None, mb, width),
                                lambda b, t: (which, b, t % row_steps, t // row_steps))

        def prev(which, row_steps=row_steps, nblk=nblk, width=width):
            return pl.BlockSpec(
                (None, None, ATTN_BLOCK, width),
                lambda b, t: (which, b, jnp.maximum((t % row_steps) * nblk - 1, 0), t // row_steps))

        if has_prev:
            specs = [rows(0), rows(1), prev(1), rows(2), prev(2)]
        else:
            specs = [rows(0), rows(1), rows(2)]
        in_specs += specs
        operands += [view] * len(specs)
        out_spec = pl.BlockSpec((None, mb, width),
                                lambda b, t, row_steps=row_steps: (b, t % row_steps, t // row_steps))
        out_specs += [out_spec, out_spec]
        out_shapes += [jax.ShapeDtypeStruct((batch, m_len, dilation * GROUP_WIDTH), F32)] * 2
        plans.append((has_prev, row_steps))
    (n_steps,) = steps
    outs = pl.pallas_call(
        functools.partial(_attn_kernel, plans=tuple(plans)),
        grid=(batch, n_steps),
        in_specs=in_specs,
        out_specs=out_specs,
        out_shape=out_shapes,
        compiler_params=pltpu.CompilerParams(
            dimension_semantics=("arbitrary", "arbitrary"), vmem_limit_bytes=VMEM_LIMIT_BYTES),
        name="dilated_attn",
    )(*operands)
    flat = [x.reshape(-1, x.shape[-1]) for x in outs]
    return list(zip(flat[0::2], flat[1::2]))


def _mix_kernel(h_ref, g_ref, o0_ref, o1_ref, o2_ref, l0_ref, l1_ref, l2_ref,
                wr_ref, cw_ref, lng_ref, lnb_ref, sgw_ref, sgb_ref,
                wa_ref, wb_ref, wc_ref, wo_ref, out_ref, carry_ref, yc_ref, nat_ref,
                *, tiles_per_seq, conv_w, sg_w):
    tm, d_model = h_ref.shape
    halves = GROUP_WIDTH // LANES
    w_skip = 3 * len(DILATIONS) * GROUP_WIDTH

    @pl.when(pl.program_id(0) % tiles_per_seq == 0)
    def _():
        carry_ref[...] = jnp.zeros_like(carry_ref)

    att_refs = (o0_ref, l0_ref, o1_ref, l1_ref, o2_ref, l2_ref)
    for slot, src_ref in enumerate(att_refs):
        d = DILATIONS[slot // 2]
        for r in range(d if d > 1 else 0):
            for half in range(halves):
                l0 = r * GROUP_WIDTH + half * LANES
                nat_ref[slot * halves + half, pl.ds(r, tm // d, stride=d), :] = src_ref[:, l0:l0 + LANES]

    def natural(slot, rs):
        if DILATIONS[slot // 2] == 1:
            return att_refs[slot][rs, :]
        return jnp.concatenate([nat_ref[slot * halves + half, rs, :] for half in range(halves)], axis=1)

    c_b, c_c, c_u, c_v, c_g = conv_w, 2 * conv_w, 3 * conv_w, 3 * conv_w + sg_w, 3 * conv_w + 2 * sg_w
    ti = lax.broadcasted_iota(jnp.int32, (SG_CHUNK, SG_CHUNK), 0)
    si = lax.broadcasted_iota(jnp.int32, (SG_CHUNK, SG_CHUNK), 1)
    gch = sg_w // SG_GROUPS
    w_sg = [jnp.where(ti >= si, sgw_ref[g], 0.0).astype(BF16) for g in range(SG_GROUPS)]
    cw = cw_ref[...]
    carry = carry_ref[...]

    for s in range(tm // MIX_SUB_ROWS):
        r0 = s * MIX_SUB_ROWS
        rs = slice(r0, r0 + MIX_SUB_ROWS)
        h = h_ref[rs, :]
        a = _rms_norm(h, g_ref[...]).astype(BF16)

        def project(c0, c1):
            return jnp.dot(a, wr_ref[:, w_skip + c0:w_skip + c1], preferred_element_type=F32)

        z_conv = project(0, c_u)
        z_sg = project(c_u, c_g)

        zx, zb, zc = z_conv[:, :c_b], z_conv[:, c_b:c_c], z_conv[:, c_c:c_u]
        u = zc * zx
        ext = jnp.concatenate([carry, u], axis=0)
        u1 = pltpu.roll(ext, 1, 0)[SUBLANES:]
        u2 = pltpu.roll(ext, 2, 0)[SUBLANES:]
        carry = u[MIX_SUB_ROWS - SUBLANES:]
        yb = (zb * (cw[0:1] * u2 + cw[1:2] * u1 + cw[2:3] * u)).astype(BF16)

        z_g0 = project(c_g, c_g + d_model)

        l0, l1, l2 = natural(1, rs), natural(3, rs), natural(5, rs)
        mx = jnp.maximum(jnp.maximum(l0, l1), l2)
        e0, e1, e2 = jnp.exp2(l0 - mx), jnp.exp2(l1 - mx), jnp.exp2(l2 - mx)
        inv = 1.0 / (e0 + e1 + e2)
        ya = jnp.concatenate([natural(0, rs) * (e0 * inv), natural(2, rs) * (e1 * inv),
                              natural(4, rs) * (e2 * inv)], axis=1).astype(BF16)
        pa = jnp.dot(ya, wa_ref[...], preferred_element_type=F32)

        us = _gelu(z_sg[:, :sg_w])
        vs = _gelu(z_sg[:, sg_w:])
        mu = jnp.mean(vs, axis=-1, keepdims=True)
        xc = vs - mu
        var = jnp.mean(xc * xc, axis=-1, keepdims=True)
        vln = (xc * lax.rsqrt(var + LN_EPS) * lng_ref[...] + lnb_ref[...]).astype(BF16)

        z_g1 = project(c_g + d_model, c_g + 2 * d_model)
        pb = jnp.dot(yb, wb_ref[...], preferred_element_type=F32)
        m = _sigmoid(z_g0) * pa + _sigmoid(z_g1) * pb

        for g in range(SG_GROUPS):
            bias = sgb_ref[g]
            cs = slice(g * gch, (g + 1) * gch)
            for n in range(MIX_SUB_ROWS // SG_CHUNK):
                ns = slice(n * SG_CHUNK, (n + 1) * SG_CHUNK)
                sv = jnp.dot(w_sg[g], vln[ns, cs], preferred_element_type=F32) + bias
                yc_ref[r0 + n * SG_CHUNK:r0 + (n + 1) * SG_CHUNK, cs] = (us[ns, cs] * sv).astype(BF16)

        z_g2 = project(c_g + 2 * d_model, c_g + 3 * d_model)
        pc = jnp.dot(yc_ref[rs, :], wc_ref[...], preferred_element_type=F32)
        m = m + _sigmoid(z_g2) * pc
        out_ref[rs, :] = h + jnp.dot(m.astype(BF16), wo_ref[...], preferred_element_type=F32)

    carry_ref[...] = carry


def _mixer_merge(layer, h2d, g, att, w_in, conv_w, ln_g, ln_b, sg_w, sg_b, w_a, w_b, w_c, w_o, seq):
    n_tok, d_model = h2d.shape
    tm = MIX_ROWS
    conv_width = conv_w.shape[-1]
    sg_width = ln_g.shape[-1]
    row_tile = lambda width: pl.BlockSpec((tm, width), lambda i: (i, 0))
    (o0, l0), (o1, l1), (o2, l2) = att
    att_tiles = [pl.BlockSpec((tm // d, d * GROUP_WIDTH), lambda i: (i, 0)) for d in DILATIONS]
    kern = functools.partial(_mix_kernel, tiles_per_seq=seq // tm, conv_w=conv_width, sg_w=sg_width)
    return pl.pallas_call(
        kern,
        grid=(n_tok // tm,),
        in_specs=[row_tile(d_model), _resident(g, layer)]
        + att_tiles * 2
        + [_resident(w_in, 0)]
        + [_resident(x, layer) for x in (conv_w, ln_g, ln_b, sg_w, sg_b, w_a, w_b, w_c, w_o)],
        out_specs=row_tile(d_model),
        out_shape=jax.ShapeDtypeStruct((n_tok, d_model), F32),
        scratch_shapes=[pltpu.VMEM((SUBLANES, conv_width), F32),
                        pltpu.VMEM((tm, sg_width), BF16),
                        pltpu.VMEM((2 * len(DILATIONS) * (GROUP_WIDTH // LANES), tm, LANES), F32)],
        compiler_params=pltpu.CompilerParams(
            dimension_semantics=("arbitrary",), vmem_limit_bytes=VMEM_LIMIT_BYTES),
        name="mixer_merge",
    )(h2d, g, o0, o1, o2, l0, l1, l2, w_in, conv_w, ln_g, ln_b, sg_w, sg_b, w_a, w_b, w_c, w_o)


def _mlp_kernel(h_ref, gm_ref, wu_ref, wd_ref, p_ref, gp_ref, wg_ref, wp_ref, gf_ref, *refs,
                final_norm):
    n_cast = len(refs) // 2
    out_ref = refs[n_cast]
    for src_ref, dst_ref in zip(refs[:n_cast], refs[n_cast + 1:]):
        dst_ref[...] = src_ref[...].astype(BF16)
    d_ff = wu_ref.shape[1]

    for s in range(h_ref.shape[0] // MLP_SUB_ROWS):
        rs = slice(s * MLP_SUB_ROWS, (s + 1) * MLP_SUB_ROWS)
        h = h_ref[rs, :]
        c = _rms_norm(h, gm_ref[...]).astype(BF16)
        acc = h
        for f0 in range(0, d_ff, MLP_FF_TILE):
            t = jnp.dot(c, wu_ref[:, f0:f0 + MLP_FF_TILE], preferred_element_type=F32)
            t = jnp.square(jnp.maximum(t, 0.0)).astype(BF16)
            acc = acc + jnp.dot(t, wd_ref[f0:f0 + MLP_FF_TILE, :], preferred_element_type=F32)
        e = _rms_norm(acc, gp_ref[...]).astype(BF16)
        gate = _sigmoid(jnp.dot(e, wg_ref[...], preferred_element_type=F32))
        emb = jnp.dot(p_ref[rs, :].astype(BF16), wp_ref[...], preferred_element_type=F32)
        out = acc + gate * emb
        out_ref[rs, :] = _rms_norm(out, gf_ref[...]) if final_norm else out


def _mlp_ple(layer, h2d, g_mlp, w_up, w_down, p3d, g_ple, w_pg, w_pe, g_final, next_weights):
    n_tok, d_model = h2d.shape
    tm = MLP_ROWS
    steps = n_tok // tm
    final_norm = not next_weights
    chunk = lambda w: (None, w.shape[1] // steps, w.shape[2])
    assert all(w.shape[1] % (steps * BF16_SUBLANES) == 0 for w in next_weights)
    row_tile = pl.BlockSpec((tm, d_model), lambda i: (i, 0))
    outs = pl.pallas_call(
        functools.partial(_mlp_kernel, final_norm=final_norm),
        grid=(steps,),
        in_specs=[
            row_tile,
            _resident(g_mlp, layer),
            _resident(w_up, 0),
            _resident(w_down, 0),
            pl.BlockSpec((None, tm, p3d.shape[-1]), lambda i: (layer, i, 0)),
            _resident(g_ple, layer),
            _resident(w_pg, layer),
            _resident(w_pe, layer),
            _resident(g_final, 0),
        ] + [pl.BlockSpec(chunk(w), lambda i: (layer + 1, i, 0)) for w in next_weights],
        out_specs=[row_tile] + [pl.BlockSpec(chunk(w), lambda i: (0, i, 0)) for w in next_weights],
        out_shape=[jax.ShapeDtypeStruct((n_tok, d_model), F32)]
        + [jax.ShapeDtypeStruct((1,) + w.shape[1:], BF16) for w in next_weights],
        compiler_params=pltpu.CompilerParams(
            dimension_semantics=("arbitrary",), vmem_limit_bytes=VMEM_LIMIT_BYTES),
        name="mlp_ple",
    )(h2d, g_mlp, w_up, w_down, p3d, g_ple, w_pg, w_pe, g_final, *next_weights)
    return outs[0], outs[1:]


def _cast_kernel(*refs):
    n = len(refs) // 2
    for src_ref, dst_ref in zip(refs[:n], refs[n:]):
        dst_ref[...] = src_ref[...].astype(BF16)


def _cast_weights(first_layer_of, whole):
    steps = CAST_STEPS
    in_specs, out_specs, out_shape, operands = [], [], [], []
    for w in first_layer_of:
        block = (None, w.shape[1] // steps, w.shape[2])
        in_specs.append(pl.BlockSpec(block, lambda i: (0, i, 0)))
        out_specs.append(pl.BlockSpec(block, lambda i: (0, i, 0)))
        out_shape.append(jax.ShapeDtypeStruct((1,) + w.shape[1:], BF16))
        operands.append(w)
    for w in whole:
        flat = w.reshape(-1, w.shape[-1])
        block = (flat.shape[0] // steps, flat.shape[1])
        in_specs.append(pl.BlockSpec(block, lambda i: (i, 0)))
        out_specs.append(pl.BlockSpec(block, lambda i: (i, 0)))
        out_shape.append(jax.ShapeDtypeStruct(flat.shape, BF16))
        operands.append(flat)
    assert all(spec.block_shape[-2] % BF16_SUBLANES == 0 for spec in out_specs)
    outs = pl.pallas_call(
        _cast_kernel,
        grid=(steps,),
        in_specs=in_specs,
        out_specs=out_specs,
        out_shape=out_shape,
        compiler_params=pltpu.CompilerParams(
            dimension_semantics=("arbitrary",), vmem_limit_bytes=VMEM_LIMIT_BYTES),
        name="cast_weights",
    )(*operands)
    n = len(first_layer_of)
    return outs[:n], [o.reshape(w.shape) for o, w in zip(outs[n:], whole)]


def _rotary_tables(positions):
    inv_freq = ROPE_THETA ** (-(jnp.arange(0, 2 * ROT_HALF, 2, dtype=F32) / (2 * ROT_HALF)))
    n_tok = positions.size
    pack = LANES // (2 * ROT_HALF)
    ang = (positions.astype(F32).reshape(n_tok // pack, pack, 1) * inv_freq).reshape(n_tok // pack, -1)
    cos8 = jnp.cos(ang).reshape(n_tok, ROT_HALF)
    sin8 = jnp.sin(ang).reshape(n_tok, ROT_HALF)
    in_head = jnp.arange(LANES) % HEAD_DIM
    rotated = in_head < 2 * ROT_HALF
    sign = jnp.where(in_head < ROT_HALF, -1.0, 1.0).astype(F32)
    spread = lambda t: jnp.tile(t, (1, LANES // ROT_HALF))
    return (jnp.where(rotated, spread(cos8), 1.0), jnp.where(rotated, sign * spread(sin8), 0.0))


def kernel(x, p, positions, norm_mix_g, w_in, conv_w, sg_ln_g, sg_ln_b, sg_w, sg_b,
           w_branch_a, w_branch_b, w_branch_c, w_out, norm_mlp_g, w_up, w_down,
           norm_ple_g, w_ple_gate, w_ple_proj, norm_final_g):
    batch, seq, d_model = x.shape
    depth = w_in.shape[0]
    n_tok = batch * seq
    qkv_width = 3 * len(DILATIONS) * GROUP_WIDTH
    assert seq % (max(DILATIONS) * ATTN_BLOCK) == 0 and seq % ATTN_ROWS_MAX == 0
    assert n_tok % MLP_ROWS == 0 and seq % MIX_ROWS == 0

    cos_t, sin_t = _rotary_tables(positions)
    rows = lambda v: v.reshape(v.shape[0], 1, v.shape[-1])
    sg_bias = jnp.broadcast_to(sg_b[:, :, :, None], sg_w.shape)
    big_f32 = (w_in, w_up, w_down)
    (w_in_l, w_up_l, w_down_l), (w_a, w_b, w_c, w_o, w_pg, w_pe) = _cast_weights(
        big_f32, (w_branch_a, w_branch_b, w_branch_c, w_out, w_ple_gate, w_ple_proj))
    g_mix, g_mlp, g_ple = rows(norm_mix_g), rows(norm_mlp_g), rows(norm_ple_g)
    ln_g, ln_b = rows(sg_ln_g), rows(sg_ln_b)
    p3d = p.reshape(depth, n_tok, p.shape[-1])

    g_final = norm_final_g.reshape(1, 1, -1)

    h = x.reshape(n_tok, d_model)
    for i in range(depth):
        qkv = _qkv_projection(i, h, g_mix, w_in_l, cos_t, sin_t)
        att = _attention(qkv, batch, seq)
        h = _mixer_merge(i, h, g_mix, att, w_in_l, conv_w, ln_g, ln_b, sg_w, sg_bias,
                         w_a, w_b, w_c, w_o, seq)
        h, nxt = _mlp_ple(i, h, g_mlp, w_up_l, w_down_l, p3d, g_ple, w_pg, w_pe, g_final,
                          big_f32 if i + 1 < depth else ())
        if nxt:
            w_in_l, w_up_l, w_down_l = nxt
    return h.reshape(batch, seq, d_model)
```

```python
import functools
import math

import jax
import jax.numpy as jnp
from jax import lax
from jax.experimental import pallas as pl
from jax.experimental.pallas import tpu as pltpu

F32 = jnp.float32
BF16 = jnp.bfloat16

HEAD_DIM = 64
HEADS_PER_GROUP = 4
GROUP_WIDTH = HEADS_PER_GROUP * HEAD_DIM
DILATIONS = (1, 4, 16)
ATTN_BLOCK = 128
ROT_HALF = 8
ROPE_THETA = 500000.0
CONV_K = 3
SG_CHUNK = 128
SG_GROUPS = 4
RMS_EPS = 1e-6
LN_EPS = 1e-5
QUERY_SCALE = math.log2(math.e) * HEAD_DIM ** -0.5

LANES = 128
SUBLANES = 8
BF16_SUBLANES = 16
VMEM_LIMIT_BYTES = 56 * 1024 * 1024

QKV_ROWS = 1024
QKV_SUB_ROWS = 256
ATTN_ROWS_MAX = 2048
MIX_ROWS = 512
MIX_SUB_ROWS = 256
MLP_ROWS = 1024
MLP_SUB_ROWS = 512
MLP_FF_TILE = 1024
CAST_STEPS = 32


def _rms_norm(h, g):
    ms = jnp.mean(h * h, axis=-1, keepdims=True)
    return h * lax.rsqrt(ms + RMS_EPS) * g


def _sigmoid(x):
    return 1.0 / (1.0 + jnp.exp(-x))


def _gelu(x):
    return 0.5 * x * (1.0 + lax.erf(x * (1.0 / math.sqrt(2.0))))


def _resident(stacked, layer):
    shape = stacked.shape[1:]
    index = (layer,) + (0,) * len(shape)
    return pl.BlockSpec((None,) + shape, lambda *_: index, pipeline_mode=pl.Buffered(1))


def _qkv_kernel(h_ref, g_ref, w_ref, cos_ref, sin_ref, *refs):
    outs, slab_ref = refs[:-1], refs[-1]
    n_groups = len(outs)
    sub = QKV_SUB_ROWS
    lane = lax.broadcasted_iota(jnp.int32, (sub, LANES), 1)
    low_half = (lane % HEAD_DIM) < ROT_HALF
    slab = 0
    for s in range(h_ref.shape[0] // sub):
        rs = slice(s * sub, (s + 1) * sub)
        a = _rms_norm(h_ref[rs, :], g_ref[...]).astype(BF16)
        cos_k, sin_k = cos_ref[rs, :], sin_ref[rs, :]
        cos_sin = ((cos_k * QUERY_SCALE, sin_k * QUERY_SCALE), (cos_k, sin_k))
        for which in range(3):
            for g, d in enumerate(DILATIONS):
                c0 = (which * n_groups + g) * GROUP_WIDTH
                z = jnp.dot(a, w_ref[:, c0:c0 + GROUP_WIDTH], preferred_element_type=F32)
                out_rows = slice(s * sub // d, (s + 1) * sub // d)
                for half in range(GROUP_WIDTH // LANES):
                    t = z[:, half * LANES:(half + 1) * LANES]
                    if which < 2:
                        partner = jnp.where(low_half,
                                            pltpu.roll(t, LANES - ROT_HALF, 1),
                                            pltpu.roll(t, ROT_HALF, 1))
                        cos_t, sin_t = cos_sin[which]
                        t = t * cos_t + partner * sin_t
                    if d == 1:
                        outs[g][which, out_rows, half * LANES:(half + 1) * LANES] = t.astype(BF16)
                    else:
                        slab_ref[slab] = t
                        for r in range(d):
                            l0 = r * GROUP_WIDTH + half * LANES
                            rows = slab_ref[slab, pl.ds(r, sub // d, stride=d), :]
                            outs[g][which, out_rows, l0:l0 + LANES] = rows.astype(BF16)
                        slab += 1


def _qkv_projection(layer, h2d, g, w_in, cos_t, sin_t):
    n_tok, d_model = h2d.shape
    tm = QKV_ROWS
    qkv_width = 3 * len(DILATIONS) * GROUP_WIDTH
    n_slabs = 3 * (GROUP_WIDTH // LANES) * sum(d > 1 for d in DILATIONS) * (tm // QKV_SUB_ROWS)
    row_tile = pl.BlockSpec((tm, d_model), lambda i: (i, 0))
    out_specs = [pl.BlockSpec((3, tm // d, d * GROUP_WIDTH), lambda i: (0, i, 0)) for d in DILATIONS]
    out_shape = [jax.ShapeDtypeStruct((3, n_tok // d, d * GROUP_WIDTH), BF16) for d in DILATIONS]
    return pl.pallas_call(
        _qkv_kernel,
        grid=(n_tok // tm,),
        in_specs=[row_tile, _resident(g, layer)] + [
            pl.BlockSpec((None, d_model, qkv_width), lambda i: (0, 0, 0),
                         pipeline_mode=pl.Buffered(1)),
            pl.BlockSpec((tm, LANES), lambda i: (i, 0)),
            pl.BlockSpec((tm, LANES), lambda i: (i, 0)),
        ],
        out_specs=out_specs,
        out_shape=out_shape,
        scratch_shapes=[pltpu.VMEM((n_slabs, QKV_SUB_ROWS, LANES), F32)],
        compiler_params=pltpu.CompilerParams(
            dimension_semantics=("arbitrary",), vmem_limit_bytes=VMEM_LIMIT_BYTES),
        name="qkv_proj",
    )(h2d, g, w_in, cos_t, sin_t)


def _attend(q_ref, k_ref, kp_ref, v_ref, vp_ref, o_ref, l_ref, first_step):
    def window(ref, prev_ref, r0, c0):
        if r0 > 0:
            return ref[r0 - ATTN_BLOCK:r0 + ATTN_BLOCK, c0:c0 + LANES]
        cur = ref[0:ATTN_BLOCK, c0:c0 + LANES]
        before = jnp.zeros_like(cur) if prev_ref is None else prev_ref[:, c0:c0 + LANES]
        return jnp.concatenate([before, cur], axis=0)

    two = 2 * ATTN_BLOCK
    qi = lax.broadcasted_iota(jnp.int32, (two, two), 0) % ATTN_BLOCK
    kj = lax.broadcasted_iota(jnp.int32, (two, two), 1)
    dist = qi + ATTN_BLOCK - kj
    band = (dist >= 0) & (dist <= ATTN_BLOCK)
    band_first = band & ((kj >= ATTN_BLOCK) | jnp.logical_not(first_step))
    mask_rest = jnp.where(band, 0.0, -jnp.inf)
    mask_first = jnp.where(band_first, 0.0, -jnp.inf)
    lane = lax.broadcasted_iota(jnp.int32, (ATTN_BLOCK, LANES), 1)
    head0 = lane < HEAD_DIM
    keep0 = jnp.where(head0, 1.0, 0.0).astype(BF16)
    keep1 = jnp.where(head0, 0.0, 1.0).astype(BF16)
    ones = jnp.ones((two, LANES), BF16)

    for i in range(q_ref.shape[0] // ATTN_BLOCK):
        mask = mask_first if i == 0 else mask_rest
        r0 = i * ATTN_BLOCK
        for p in range(q_ref.shape[1] // LANES):
            c0 = p * LANES
            q2 = q_ref[r0:r0 + ATTN_BLOCK, c0:c0 + LANES]
            qs = jnp.concatenate([q2 * keep0, q2 * keep1], axis=0)
            kk = window(k_ref, kp_ref, r0, c0)
            s = lax.dot_general(qs, kk, (((1,), (1,)), ((), ())),
                                preferred_element_type=F32)
            s = s + mask
            m = jnp.max(s, axis=-1, keepdims=True)
            e = jnp.exp2((s - m).astype(BF16))
            vv = jnp.concatenate([window(v_ref, vp_ref, r0, c0), ones], axis=1)
            pv = jnp.dot(e, vv, preferred_element_type=F32)
            pick = lambda x: jnp.where(head0, x[:ATTN_BLOCK], x[ATTN_BLOCK:])
            den = pick(pv[:, LANES:])
            o_ref[r0:r0 + ATTN_BLOCK, c0:c0 + LANES] = pick(pv[:, :LANES]) * (1.0 / den)
            l_ref[r0:r0 + ATTN_BLOCK, c0:c0 + LANES] = pick(m) + jnp.log2(den)


def _attn_kernel(*refs, plans):
    refs = iter(refs)
    inputs = []
    for has_prev, _ in plans:
        q_ref, k_ref = next(refs), next(refs)
        kp_ref = next(refs) if has_prev else None
        v_ref = next(refs)
        vp_ref = next(refs) if has_prev else None
        inputs.append((q_ref, k_ref, kp_ref, v_ref, vp_ref))
    for (has_prev, row_steps), group_inputs in zip(plans, inputs):
        first_step = (pl.program_id(1) % row_steps == 0) if has_prev else True
        _attend(*group_inputs, next(refs), next(refs), first_step)


def _attention(qkv, batch, seq):
    in_specs, operands, out_specs, out_shapes, plans, steps = [], [], [], [], [], set()
    for qkv_g, dilation in zip(qkv, DILATIONS):
        m_len = seq // dilation
        mb = min(m_len, ATTN_ROWS_MAX)
        nblk = mb // ATTN_BLOCK
        n_res = min(dilation, ATTN_ROWS_MAX // mb)
        width = n_res * GROUP_WIDTH
        row_steps = m_len // mb
        has_prev = row_steps > 1
        steps.add(row_steps * (dilation // n_res))
        view = qkv_g.reshape(3, batch, m_len, dilation * GROUP_WIDTH)

        def rows(which, row_steps=row_steps, mb=mb, width=width):
            return pl.BlockSpec((None, None, mb, width),
                                lambda b, t: (which, b, t % row_steps, t // row_steps))

        def prev(which, row_steps=row_steps, nblk=nblk, width=width):
            return pl.BlockSpec(
                (None, None, ATTN_BLOCK, width),
                lambda b, t: (which, b, jnp.maximum((t % row_steps) * nblk - 1, 0), t // row_steps))

        if has_prev:
            specs = [rows(0), rows(1), prev(1), rows(2), prev(2)]
        else:
            specs = [rows(0), rows(1), rows(2)]
        in_specs += specs
        operands += [view] * len(specs)
        out_spec = pl.BlockSpec((None, mb, width),
                                lambda b, t, row_steps=row_steps: (b, t % row_steps, t // row_steps))
        out_specs += [out_spec, out_spec]
        out_shapes += [jax.ShapeDtypeStruct((batch, m_len, dilation * GROUP_WIDTH), F32)] * 2
        plans.append((has_prev, row_steps))
    (n_steps,) = steps
    outs = pl.pallas_call(
        functools.partial(_attn_kernel, plans=tuple(plans)),
        grid=(batch, n_steps),
        in_specs=in_specs,
        out_specs=out_specs,
        out_shape=out_shapes,
        compiler_params=pltpu.CompilerParams(
            dimension_semantics=("arbitrary", "arbitrary"), vmem_limit_bytes=VMEM_LIMIT_BYTES),
        name="dilated_attn",
    )(*operands)
    flat = [x.reshape(-1, x.shape[-1]) for x in outs]
    return list(zip(flat[0::2], flat[1::2]))


def _mix_kernel(h_ref, g_ref, o0_ref, o1_ref, o2_ref, l0_ref, l1_ref, l2_ref,
                wr_ref, cw_ref, lng_ref, lnb_ref, sgw_ref, sgb_ref,
                wa_ref, wb_ref, wc_ref, wo_ref, out_ref, carry_ref, yc_ref, nat_ref,
                *, tiles_per_seq, conv_w, sg_w):
    tm, d_model = h_ref.shape
    halves = GROUP_WIDTH // LANES
    w_skip = 3 * len(DILATIONS) * GROUP_WIDTH

    @pl.when(pl.program_id(0) % tiles_per_seq == 0)
    def _():
        carry_ref[...] = jnp.zeros_like(carry_ref)

    att_refs = (o0_ref, l0_ref, o1_ref, l1_ref, o2_ref, l2_ref)
    for slot, src_ref in enumerate(att_refs):
        d = DILATIONS[slot // 2]
        for r in range(d if d > 1 else 0):
            for half in range(halves):
                l0 = r * GROUP_WIDTH + half * LANES
                nat_ref[slot * halves + half, pl.ds(r, tm // d, stride=d), :] = src_ref[:, l0:l0 + LANES]

    def natural(slot, rs):
        if DILATIONS[slot // 2] == 1:
            return att_refs[slot][rs, :]
        return jnp.concatenate([nat_ref[slot * halves + half, rs, :] for half in range(halves)], axis=1)

    c_b, c_c, c_u, c_v, c_g = conv_w, 2 * conv_w, 3 * conv_w, 3 * conv_w + sg_w, 3 * conv_w + 2 * sg_w
    ti = lax.broadcasted_iota(jnp.int32, (SG_CHUNK, SG_CHUNK), 0)
    si = lax.broadcasted_iota(jnp.int32, (SG_CHUNK, SG_CHUNK), 1)
    gch = sg_w // SG_GROUPS
    w_sg = [jnp.where(ti >= si, sgw_ref[g], 0.0).astype(BF16) for g in range(SG_GROUPS)]
    cw = cw_ref[...]
    carry = carry_ref[...]

    for s in range(tm // MIX_SUB_ROWS):
        r0 = s * MIX_SUB_ROWS
        rs = slice(r0, r0 + MIX_SUB_ROWS)
        h = h_ref[rs, :]
        a = _rms_norm(h, g_ref[...]).astype(BF16)

        def project(c0, c1):
            return jnp.dot(a, wr_ref[:, w_skip + c0:w_skip + c1], preferred_element_type=F32)

        z_conv = project(0, c_u)
        z_sg = project(c_u, c_g)

        zx, zb, zc = z_conv[:, :c_b], z_conv[:, c_b:c_c], z_conv[:, c_c:c_u]
        u = zc * zx
        ext = jnp.concatenate([carry, u], axis=0)
        u1 = pltpu.roll(ext, 1, 0)[SUBLANES:]
        u2 = pltpu.roll(ext, 2, 0)[SUBLANES:]
        carry = u[MIX_SUB_ROWS - SUBLANES:]
        yb = (zb * (cw[0:1] * u2 + cw[1:2] * u1 + cw[2:3] * u)).astype(BF16)

        z_g0 = project(c_g, c_g + d_model)

        l0, l1, l2 = natural(1, rs), natural(3, rs), natural(5, rs)
        mx = jnp.maximum(jnp.maximum(l0, l1), l2)
        e0, e1, e2 = jnp.exp2(l0 - mx), jnp.exp2(l1 - mx), jnp.exp2(l2 - mx)
        inv = 1.0 / (e0 + e1 + e2)
        ya = jnp.concatenate([natural(0, rs) * (e0 * inv), natural(2, rs) * (e1 * inv),
                              natural(4, rs) * (e2 * inv)], axis=1).astype(BF16)
        pa = jnp.dot(ya, wa_ref[...], preferred_element_type=F32)

        us = _gelu(z_sg[:, :sg_w])
        vs = _gelu(z_sg[:, sg_w:])
        mu = jnp.mean(vs, axis=-1, keepdims=True)
        xc = vs - mu
        var = jnp.mean(xc * xc, axis=-1, keepdims=True)
        vln = (xc * lax.rsqrt(var + LN_EPS) * lng_ref[...] + lnb_ref[...]).astype(BF16)

        z_g1 = project(c_g + d_model, c_g + 2 * d_model)
        pb = jnp.dot(yb, wb_ref[...], preferred_element_type=F32)
        m = _sigmoid(z_g0) * pa + _sigmoid(z_g1) * pb

        for g in range(SG_GROUPS):
            bias = sgb_ref[g]
            cs = slice(g * gch, (g + 1) * gch)
            for n in range(MIX_SUB_ROWS // SG_CHUNK):
                ns = slice(n * SG_CHUNK, (n + 1) * SG_CHUNK)
                sv = jnp.dot(w_sg[g], vln[ns, cs], preferred_element_type=F32) + bias
                yc_ref[r0 + n * SG_CHUNK:r0 + (n + 1) * SG_CHUNK, cs] = (us[ns, cs] * sv).astype(BF16)

        z_g2 = project(c_g + 2 * d_model, c_g + 3 * d_model)
        pc = jnp.dot(yc_ref[rs, :], wc_ref[...], preferred_element_type=F32)
        m = m + _sigmoid(z_g2) * pc
        out_ref[rs, :] = h + jnp.dot(m.astype(BF16), wo_ref[...], preferred_element_type=F32)

    carry_ref[...] = carry


def _mixer_merge(layer, h2d, g, att, w_in, conv_w, ln_g, ln_b, sg_w, sg_b, w_a, w_b, w_c, w_o, seq):
    n_tok, d_model = h2d.shape
    tm = MIX_ROWS
    conv_width = conv_w.shape[-1]
    sg_width = ln_g.shape[-1]
    row_tile = lambda width: pl.BlockSpec((tm, width), lambda i: (i, 0))
    (o0, l0), (o1, l1), (o2, l2) = att
    att_tiles = [pl.BlockSpec((tm // d, d * GROUP_WIDTH), lambda i: (i, 0)) for d in DILATIONS]
    kern = functools.partial(_mix_kernel, tiles_per_seq=seq // tm, conv_w=conv_width, sg_w=sg_width)
    return pl.pallas_call(
        kern,
        grid=(n_tok // tm,),
        in_specs=[row_tile(d_model), _resident(g, layer)]
        + att_tiles * 2
        + [_resident(w_in, 0)]
        + [_resident(x, layer) for x in (conv_w, ln_g, ln_b, sg_w, sg_b, w_a, w_b, w_c, w_o)],
        out_specs=row_tile(d_model),
        out_shape=jax.ShapeDtypeStruct((n_tok, d_model), F32),
        scratch_shapes=[pltpu.VMEM((SUBLANES, conv_width), F32),
                        pltpu.VMEM((tm, sg_width), BF16),
                        pltpu.VMEM((2 * len(DILATIONS) * (GROUP_WIDTH // LANES), tm, LANES), F32)],
        compiler_params=pltpu.CompilerParams(
            dimension_semantics=("arbitrary",), vmem_limit_bytes=VMEM_LIMIT_BYTES),
        name="mixer_merge",
    )(h2d, g, o0, o1, o2, l0, l1, l2, w_in, conv_w, ln_g, ln_b, sg_w, sg_b, w_a, w_b, w_c, w_o)


def _mlp_kernel(h_ref, gm_ref, wu_ref, wd_ref, p_ref, gp_ref, wg_ref, wp_ref, gf_ref, *refs,
                final_norm):
    n_cast = len(refs) // 2
    out_ref = refs[n_cast]
    for src_ref, dst_ref in zip(refs[:n_cast], refs[n_cast + 1:]):
        dst_ref[...] = src_ref[...].astype(BF16)
    d_ff = wu_ref.shape[1]

    for s in range(h_ref.shape[0] // MLP_SUB_ROWS):
        rs = slice(s * MLP_SUB_ROWS, (s + 1) * MLP_SUB_ROWS)
        h = h_ref[rs, :]
        c = _rms_norm(h, gm_ref[...]).astype(BF16)
        acc = h
        for f0 in range(0, d_ff, MLP_FF_TILE):
            t = jnp.dot(c, wu_ref[:, f0:f0 + MLP_FF_TILE], preferred_element_type=F32)
            t = jnp.square(jnp.maximum(t, 0.0)).astype(BF16)
            acc = acc + jnp.dot(t, wd_ref[f0:f0 + MLP_FF_TILE, :], preferred_element_type=F32)
        e = _rms_norm(acc, gp_ref[...]).astype(BF16)
        gate = _sigmoid(jnp.dot(e, wg_ref[...], preferred_element_type=F32))
        emb = jnp.dot(p_ref[rs, :].astype(BF16), wp_ref[...], preferred_element_type=F32)
        out = acc + gate * emb
        out_ref[rs, :] = _rms_norm(out, gf_ref[...]) if final_norm else out


def _mlp_ple(layer, h2d, g_mlp, w_up, w_down, p3d, g_ple, w_pg, w_pe, g_final, next_weights):
    n_tok, d_model = h2d.shape
    tm = MLP_ROWS
    steps = n_tok // tm
    final_norm = not next_weights
    chunk = lambda w: (None, w.shape[1] // steps, w.shape[2])
    assert all(w.shape[1] % (steps * BF16_SUBLANES) == 0 for w in next_weights)
    row_tile = pl.BlockSpec((tm, d_model), lambda i: (i, 0))
    outs = pl.pallas_call(
        functools.partial(_mlp_kernel, final_norm=final_norm),
        grid=(steps,),
        in_specs=[
            row_tile,
            _resident(g_mlp, layer),
            _resident(w_up, 0),
            _resident(w_down, 0),
            pl.BlockSpec((None, tm, p3d.shape[-1]), lambda i: (layer, i, 0)),
            _resident(g_ple, layer),
            _resident(w_pg, layer),
            _resident(w_pe, layer),
            _resident(g_final, 0),
        ] + [pl.BlockSpec(chunk(w), lambda i: (layer + 1, i, 0)) for w in next_weights],
        out_specs=[row_tile] + [pl.BlockSpec(chunk(w), lambda i: (0, i, 0)) for w in next_weights],
        out_shape=[jax.ShapeDtypeStruct((n_tok, d_model), F32)]
        + [jax.ShapeDtypeStruct((1,) + w.shape[1:], BF16) for w in next_weights],
        compiler_params=pltpu.CompilerParams(
            dimension_semantics=("arbitrary",), vmem_limit_bytes=VMEM_LIMIT_BYTES),
        name="mlp_ple",
    )(h2d, g_mlp, w_up, w_down, p3d, g_ple, w_pg, w_pe, g_final, *next_weights)
    return outs[0], outs[1:]


def _cast_kernel(*refs):
    n = len(refs) // 2
    for src_ref, dst_ref in zip(refs[:n], refs[n:]):
        dst_ref[...] = src_ref[...].astype(BF16)


def _cast_weights(first_layer_of, whole):
    steps = CAST_STEPS
    in_specs, out_specs, out_shape, operands = [], [], [], []
    for w in first_layer_of:
        block = (None, w.shape[1] // steps, w.shape[2])
        in_specs.append(pl.BlockSpec(block, lambda i: (0, i, 0)))
        out_specs.append(pl.BlockSpec(block, lambda i: (0, i, 0)))
        out_shape.append(jax.ShapeDtypeStruct((1,) + w.shape[1:], BF16))
        operands.append(w)
    for w in whole:
        flat = w.reshape(-1, w.shape[-1])
        block = (flat.shape[0] // steps, flat.shape[1])
        in_specs.append(pl.BlockSpec(block, lambda i: (i, 0)))
        out_specs.append(pl.BlockSpec(block, lambda i: (i, 0)))
        out_shape.append(jax.ShapeDtypeStruct(flat.shape, BF16))
        operands.append(flat)
    assert all(spec.block_shape[-2] % BF16_SUBLANES == 0 for spec in out_specs)
    outs = pl.pallas_call(
        _cast_kernel,
        grid=(steps,),
        in_specs=in_specs,
        out_specs=out_specs,
        out_shape=out_shape,
        compiler_params=pltpu.CompilerParams(
            dimension_semantics=("arbitrary",), vmem_limit_bytes=VMEM_LIMIT_BYTES),
        name="cast_weights",
    )(*operands)
    n = len(first_layer_of)
    return outs[:n], [o.reshape(w.shape) for o, w in zip(outs[n:], whole)]


def _rotary_tables(positions):
    inv_freq = ROPE_THETA ** (-(jnp.arange(0, 2 * ROT_HALF, 2, dtype=F32) / (2 * ROT_HALF)))
    n_tok = positions.size
    pack = LANES // (2 * ROT_HALF)
    ang = (positions.astype(F32).reshape(n_tok // pack, pack, 1) * inv_freq).reshape(n_tok // pack, -1)
    cos8 = jnp.cos(ang).reshape(n_tok, ROT_HALF)
    sin8 = jnp.sin(ang).reshape(n_tok, ROT_HALF)
    in_head = jnp.arange(LANES) % HEAD_DIM
    rotated = in_head < 2 * ROT_HALF
    sign = jnp.where(in_head < ROT_HALF, -1.0, 1.0).astype(F32)
    spread = lambda t: jnp.tile(t, (1, LANES // ROT_HALF))
    return (jnp.where(rotated, spread(cos8), 1.0), jnp.where(rotated, sign * spread(sin8), 0.0))


def kernel(x, p, positions, norm_mix_g, w_in, conv_w, sg_ln_g, sg_ln_b, sg_w, sg_b,
           w_branch_a, w_branch_b, w_branch_c, w_out, norm_mlp_g, w_up, w_down,
           norm_ple_g, w_ple_gate, w_ple_proj, norm_final_g):
    batch, seq, d_model = x.shape
    depth = w_in.shape[0]
    n_tok = batch * seq
    qkv_width = 3 * len(DILATIONS) * GROUP_WIDTH
    assert seq % (max(DILATIONS) * ATTN_BLOCK) == 0 and seq % ATTN_ROWS_MAX == 0
    assert n_tok % MLP_ROWS == 0 and seq % MIX_ROWS == 0

    cos_t, sin_t = _rotary_tables(positions)
    rows = lambda v: v.reshape(v.shape[0], 1, v.shape[-1])
    sg_bias = jnp.broadcast_to(sg_b[:, :, :, None], sg_w.shape)
    big_f32 = (w_in, w_up, w_down)
    (w_in_l, w_up_l, w_down_l), (w_a, w_b, w_c, w_o, w_pg, w_pe) = _cast_weights(
        big_f32, (w_branch_a, w_branch_b, w_branch_c, w_out, w_ple_gate, w_ple_proj))
    g_mix, g_mlp, g_ple = rows(norm_mix_g), rows(norm_mlp_g), rows(norm_ple_g)
    ln_g, ln_b = rows(sg_ln_g), rows(sg_ln_b)
    p3d = p.reshape(depth, n_tok, p.shape[-1])

    g_final = norm_final_g.reshape(1, 1, -1)

    h = x.reshape(n_tok, d_model)
    for i in range(depth):
        qkv = _qkv_projection(i, h, g_mix, w_in_l, cos_t, sin_t)
        att = _attention(qkv, batch, seq)
        h = _mixer_merge(i, h, g_mix, att, w_in_l, conv_w, ln_g, ln_b, sg_w, sg_bias,
                         w_a, w_b, w_c, w_o, seq)
        h, nxt = _mlp_ple(i, h, g_mlp, w_up_l, w_down_l, p3d, g_ple, w_pg, w_pe, g_final,
                          big_f32 if i + 1 < depth else ())
        if nxt:
            w_in_l, w_up_l, w_down_l = nxt
    return h.reshape(batch, seq, d_model)
```

```python
import functools
import math

import jax
import jax.numpy as jnp
from jax import lax
from jax.experimental import pallas as pl
from jax.experimental.pallas import tpu as pltpu

F32 = jnp.float32
BF16 = jnp.bfloat16

HEAD_DIM = 64
HEADS_PER_GROUP = 4
GROUP_WIDTH = HEADS_PER_GROUP * HEAD_DIM
DILATIONS = (1, 4, 16)
ATTN_BLOCK = 128
ROT_HALF = 8
ROPE_THETA = 500000.0
CONV_K = 3
SG_CHUNK = 128
SG_GROUPS = 4
RMS_EPS = 1e-6
LN_EPS = 1e-5
QUERY_SCALE = math.log2(math.e) * HEAD_DIM ** -0.5

LANES = 128
SUBLANES = 8
BF16_SUBLANES = 16
VMEM_LIMIT_BYTES = 56 * 1024 * 1024

QKV_ROWS = 1024
QKV_SUB_ROWS = 256
ATTN_ROWS_MAX = 2048
MIX_ROWS = 512
MIX_SUB_ROWS = 256
MLP_ROWS = 512
MLP_SUB_ROWS = 512
MLP_FF_TILE = 1024
CAST_STEPS = 32


def _rms_norm(h, g):
    ms = jnp.mean(h * h, axis=-1, keepdims=True)
    return h * lax.rsqrt(ms + RMS_EPS) * g


def _sigmoid(x):
    return 1.0 / (1.0 + jnp.exp(-x))


def _gelu(x):
    return 0.5 * x * (1.0 + lax.erf(x * (1.0 / math.sqrt(2.0))))


def _resident(stacked, layer):
    shape = stacked.shape[1:]
    index = (layer,) + (0,) * len(shape)
    return pl.BlockSpec((None,) + shape, lambda *_: index, pipeline_mode=pl.Buffered(1))


def _qkv_kernel(h_ref, g_ref, w_ref, cos_ref, sin_ref, *refs):
    outs, slab_ref = refs[:-1], refs[-1]
    n_groups = len(outs)
    sub = QKV_SUB_ROWS
    lane = lax.broadcasted_iota(jnp.int32, (sub, LANES), 1)
    low_half = (lane % HEAD_DIM) < ROT_HALF
    slab = 0
    for s in range(h_ref.shape[0] // sub):
        rs = slice(s * sub, (s + 1) * sub)
        a = _rms_norm(h_ref[rs, :], g_ref[...]).astype(BF16)
        cos_k, sin_k = cos_ref[rs, :], sin_ref[rs, :]
        cos_sin = ((cos_k * QUERY_SCALE, sin_k * QUERY_SCALE), (cos_k, sin_k))
        for which in range(3):
            for g, d in enumerate(DILATIONS):
                c0 = (which * n_groups + g) * GROUP_WIDTH
                z = jnp.dot(a, w_ref[:, c0:c0 + GROUP_WIDTH], preferred_element_type=F32)
                out_rows = slice(s * sub // d, (s + 1) * sub // d)
                for half in range(GROUP_WIDTH // LANES):
                    t = z[:, half * LANES:(half + 1) * LANES]
                    if which < 2:
                        partner = jnp.where(low_half,
                                            pltpu.roll(t, LANES - ROT_HALF, 1),
                                            pltpu.roll(t, ROT_HALF, 1))
                        cos_t, sin_t = cos_sin[which]
                        t = t * cos_t + partner * sin_t
                    if d == 1:
                        outs[g][which, out_rows, half * LANES:(half + 1) * LANES] = t.astype(BF16)
                    else:
                        slab_ref[slab] = t
                        for r in range(d):
                            l0 = r * GROUP_WIDTH + half * LANES
                            rows = slab_ref[slab, pl.ds(r, sub // d, stride=d), :]
                            outs[g][which, out_rows, l0:l0 + LANES] = rows.astype(BF16)
                        slab += 1


def _qkv_projection(layer, h2d, g, w_in, cos_t, sin_t):
    n_tok, d_model = h2d.shape
    tm = QKV_ROWS
    qkv_width = 3 * len(DILATIONS) * GROUP_WIDTH
    n_slabs = 3 * (GROUP_WIDTH // LANES) * sum(d > 1 for d in DILATIONS) * (tm // QKV_SUB_ROWS)
    row_tile = pl.BlockSpec((tm, d_model), lambda i: (i, 0))
    out_specs = [pl.BlockSpec((3, tm // d, d * GROUP_WIDTH), lambda i: (0, i, 0)) for d in DILATIONS]
    out_shape = [jax.ShapeDtypeStruct((3, n_tok // d, d * GROUP_WIDTH), BF16) for d in DILATIONS]
    return pl.pallas_call(
        _qkv_kernel,
        grid=(n_tok // tm,),
        in_specs=[row_tile, _resident(g, layer)] + [
            pl.BlockSpec((None, d_model, qkv_width), lambda i: (0, 0, 0),
                         pipeline_mode=pl.Buffered(1)),
            pl.BlockSpec((tm, LANES), lambda i: (i, 0)),
            pl.BlockSpec((tm, LANES), lambda i: (i, 0)),
        ],
        out_specs=out_specs,
        out_shape=out_shape,
        scratch_shapes=[pltpu.VMEM((n_slabs, QKV_SUB_ROWS, LANES), F32)],
        compiler_params=pltpu.CompilerParams(
            dimension_semantics=("arbitrary",), vmem_limit_bytes=VMEM_LIMIT_BYTES),
        name="qkv_proj",
    )(h2d, g, w_in, cos_t, sin_t)


def _attend(q_ref, k_ref, kp_ref, v_ref, vp_ref, o_ref, l_ref, first_step):
    def window(ref, prev_ref, r0, c0):
        if r0 > 0:
            return ref[r0 - ATTN_BLOCK:r0 + ATTN_BLOCK, c0:c0 + LANES]
        cur = ref[0:ATTN_BLOCK, c0:c0 + LANES]
        before = jnp.zeros_like(cur) if prev_ref is None else prev_ref[:, c0:c0 + LANES]
        return jnp.concatenate([before, cur], axis=0)

    two = 2 * ATTN_BLOCK
    qi = lax.broadcasted_iota(jnp.int32, (two, two), 0) % ATTN_BLOCK
    kj = lax.broadcasted_iota(jnp.int32, (two, two), 1)
    dist = qi + ATTN_BLOCK - kj
    band = (dist >= 0) & (dist <= ATTN_BLOCK)
    band_first = band & ((kj >= ATTN_BLOCK) | jnp.logical_not(first_step))
    mask_rest = jnp.where(band, 0.0, -jnp.inf)
    mask_first = jnp.where(band_first, 0.0, -jnp.inf)
    lane = lax.broadcasted_iota(jnp.int32, (ATTN_BLOCK, LANES), 1)
    head0 = lane < HEAD_DIM
    keep0 = jnp.where(head0, 1.0, 0.0).astype(BF16)
    keep1 = jnp.where(head0, 0.0, 1.0).astype(BF16)
    ones = jnp.ones((two, LANES), BF16)

    for i in range(q_ref.shape[0] // ATTN_BLOCK):
        mask = mask_first if i == 0 else mask_rest
        r0 = i * ATTN_BLOCK
        for p in range(q_ref.shape[1] // LANES):
            c0 = p * LANES
            q2 = q_ref[r0:r0 + ATTN_BLOCK, c0:c0 + LANES]
            qs = jnp.concatenate([q2 * keep0, q2 * keep1], axis=0)
            kk = window(k_ref, kp_ref, r0, c0)
            s = lax.dot_general(qs, kk, (((1,), (1,)), ((), ())),
                                preferred_element_type=F32)
            s = s + mask
            m = jnp.max(s, axis=-1, keepdims=True)
            e = jnp.exp2((s - m).astype(BF16))
            vv = jnp.concatenate([window(v_ref, vp_ref, r0, c0), ones], axis=1)
            pv = jnp.dot(e, vv, preferred_element_type=F32)
            pick = lambda x: jnp.where(head0, x[:ATTN_BLOCK], x[ATTN_BLOCK:])
            den = pick(pv[:, LANES:])
            o_ref[r0:r0 + ATTN_BLOCK, c0:c0 + LANES] = pick(pv[:, :LANES]) * (1.0 / den)
            l_ref[r0:r0 + ATTN_BLOCK, c0:c0 + LANES] = pick(m) + jnp.log2(den)


def _attn_kernel(*refs, plans):
    refs = iter(refs)
    inputs = []
    for has_prev, _ in plans:
        q_ref, k_ref = next(refs), next(refs)
        kp_ref = next(refs) if has_prev else None
        v_ref = next(refs)
        vp_ref = next(refs) if has_prev else None
        inputs.append((q_ref, k_ref, kp_ref, v_ref, vp_ref))
    for (has_prev, row_steps), group_inputs in zip(plans, inputs):
        first_step = (pl.program_id(1) % row_steps == 0) if has_prev else True
        _attend(*group_inputs, next(refs), next(refs), first_step)


def _attention(qkv, batch, seq):
    in_specs, operands, out_specs, out_shapes, plans, steps = [], [], [], [], [], set()
    for qkv_g, dilation in zip(qkv, DILATIONS):
        m_len = seq // dilation
        mb = min(m_len, ATTN_ROWS_MAX)
        nblk = mb // ATTN_BLOCK
        n_res = min(dilation, ATTN_ROWS_MAX // mb)
        width = n_res * GROUP_WIDTH
        row_steps = m_len // mb
        has_prev = row_steps > 1
        steps.add(row_steps * (dilation // n_res))
        view = qkv_g.reshape(3, batch, m_len, dilation * GROUP_WIDTH)

        def rows(which, row_steps=row_steps, mb=mb, width=width):
            return pl.BlockSpec((None, None, mb, width),
                                lambda b, t: (which, b, t % row_steps, t // row_steps))

        def prev(which, row_steps=row_steps, nblk=nblk, width=width):
            return pl.BlockSpec(
                (None, None, ATTN_BLOCK, width),
                lambda b, t: (which, b, jnp.maximum((t % row_steps) * nblk - 1, 0), t // row_steps))

        if has_prev:
            specs = [rows(0), rows(1), prev(1), rows(2), prev(2)]
        else:
            specs = [rows(0), rows(1), rows(2)]
        in_specs += specs
        operands += [view] * len(specs)
        out_spec = pl.BlockSpec((None, mb, width),
                                lambda b, t, row_steps=row_steps: (b, t % row_steps, t // row_steps))
        out_specs += [out_spec, out_spec]
        out_shapes += [jax.ShapeDtypeStruct((batch, m_len, dilation * GROUP_WIDTH), F32)] * 2
        plans.append((has_prev, row_steps))
    (n_steps,) = steps
    outs = pl.pallas_call(
        functools.partial(_attn_kernel, plans=tuple(plans)),
        grid=(batch, n_steps),
        in_specs=in_specs,
        out_specs=out_specs,
        out_shape=out_shapes,
        compiler_params=pltpu.CompilerParams(
            dimension_semantics=("arbitrary", "arbitrary"), vmem_limit_bytes=VMEM_LIMIT_BYTES),
        name="dilated_attn",
    )(*operands)
    flat = [x.reshape(-1, x.shape[-1]) for x in outs]
    return list(zip(flat[0::2], flat[1::2]))


def _mix_kernel(h_ref, g_ref, o0_ref, o1_ref, o2_ref, l0_ref, l1_ref, l2_ref,
                wr_ref, cw_ref, lng_ref, lnb_ref, sgw_ref, sgb_ref,
                wa_ref, wb_ref, wc_ref, wo_ref, out_ref, carry_ref, yc_ref, nat_ref,
                *, tiles_per_seq, conv_w, sg_w):
    tm, d_model = h_ref.shape
    halves = GROUP_WIDTH // LANES
    w_skip = 3 * len(DILATIONS) * GROUP_WIDTH

    @pl.when(pl.program_id(0) % tiles_per_seq == 0)
    def _():
        carry_ref[...] = jnp.zeros_like(carry_ref)

    att_refs = (o0_ref, l0_ref, o1_ref, l1_ref, o2_ref, l2_ref)
    for slot, src_ref in enumerate(att_refs):
        d = DILATIONS[slot // 2]
        for r in range(d if d > 1 else 0):
            for half in range(halves):
                l0 = r * GROUP_WIDTH + half * LANES
                nat_ref[slot * halves + half, pl.ds(r, tm // d, stride=d), :] = src_ref[:, l0:l0 + LANES]

    def natural(slot, rs):
        if DILATIONS[slot // 2] == 1:
            return att_refs[slot][rs, :]
        return jnp.concatenate([nat_ref[slot * halves + half, rs, :] for half in range(halves)], axis=1)

    c_b, c_c, c_u, c_v, c_g = conv_w, 2 * conv_w, 3 * conv_w, 3 * conv_w + sg_w, 3 * conv_w + 2 * sg_w
    ti = lax.broadcasted_iota(jnp.int32, (SG_CHUNK, SG_CHUNK), 0)
    si = lax.broadcasted_iota(jnp.int32, (SG_CHUNK, SG_CHUNK), 1)
    gch = sg_w // SG_GROUPS
    w_sg = [jnp.where(ti >= si, sgw_ref[g], 0.0).astype(BF16) for g in range(SG_GROUPS)]
    cw = cw_ref[...]
    carry = carry_ref[...]

    for s in range(tm // MIX_SUB_ROWS):
        r0 = s * MIX_SUB_ROWS
        rs = slice(r0, r0 + MIX_SUB_ROWS)
        h = h_ref[rs, :]
        a = _rms_norm(h, g_ref[...]).astype(BF16)

        def project(c0, c1):
            return jnp.dot(a, wr_ref[:, w_skip + c0:w_skip + c1], preferred_element_type=F32)

        z_conv = project(0, c_u)
        z_sg = project(c_u, c_g)

        zx, zb, zc = z_conv[:, :c_b], z_conv[:, c_b:c_c], z_conv[:, c_c:c_u]
        u = zc * zx
        ext = jnp.concatenate([carry, u], axis=0)
        u1 = pltpu.roll(ext, 1, 0)[SUBLANES:]
        u2 = pltpu.roll(ext, 2, 0)[SUBLANES:]
        carry = u[MIX_SUB_ROWS - SUBLANES:]
        yb = (zb * (cw[0:1] * u2 + cw[1:2] * u1 + cw[2:3] * u)).astype(BF16)

        z_g0 = project(c_g, c_g + d_model)

        l0, l1, l2 = natural(1, rs), natural(3, rs), natural(5, rs)
        mx = jnp.maximum(jnp.maximum(l0, l1), l2)
        e0, e1, e2 = jnp.exp2(l0 - mx), jnp.exp2(l1 - mx), jnp.exp2(l2 - mx)
        inv = 1.0 / (e0 + e1 + e2)
        ya = jnp.concatenate([natural(0, rs) * (e0 * inv), natural(2, rs) * (e1 * inv),
                              natural(4, rs) * (e2 * inv)], axis=1).astype(BF16)
        pa = jnp.dot(ya, wa_ref[...], preferred_element_type=F32)

        us = _gelu(z_sg[:, :sg_w])
        vs = _gelu(z_sg[:, sg_w:])
        mu = jnp.mean(vs, axis=-1, keepdims=True)
        xc = vs - mu
        var = jnp.mean(xc * xc, axis=-1, keepdims=True)
        vln = (xc * lax.rsqrt(var + LN_EPS) * lng_ref[...] + lnb_ref[...]).astype(BF16)

        z_g1 = project(c_g + d_model, c_g + 2 * d_model)
        pb = jnp.dot(yb, wb_ref[...], preferred_element_type=F32)
        m = _sigmoid(z_g0) * pa + _sigmoid(z_g1) * pb

        for g in range(SG_GROUPS):
            bias = sgb_ref[g]
            cs = slice(g * gch, (g + 1) * gch)
            for n in range(MIX_SUB_ROWS // SG_CHUNK):
                ns = slice(n * SG_CHUNK, (n + 1) * SG_CHUNK)
                sv = jnp.dot(w_sg[g], vln[ns, cs], preferred_element_type=F32) + bias
                yc_ref[r0 + n * SG_CHUNK:r0 + (n + 1) * SG_CHUNK, cs] = (us[ns, cs] * sv).astype(BF16)

        z_g2 = project(c_g + 2 * d_model, c_g + 3 * d_model)
        pc = jnp.dot(yc_ref[rs, :], wc_ref[...], preferred_element_type=F32)
        m = m + _sigmoid(z_g2) * pc
        out_ref[rs, :] = h + jnp.dot(m.astype(BF16), wo_ref[...], preferred_element_type=F32)

    carry_ref[...] = carry


def _mixer_merge(layer, h2d, g, att, w_in, conv_w, ln_g, ln_b, sg_w, sg_b, w_a, w_b, w_c, w_o, seq):
    n_tok, d_model = h2d.shape
    tm = MIX_ROWS
    conv_width = conv_w.shape[-1]
    sg_width = ln_g.shape[-1]
    row_tile = lambda width: pl.BlockSpec((tm, width), lambda i: (i, 0))
    (o0, l0), (o1, l1), (o2, l2) = att
    att_tiles = [pl.BlockSpec((tm // d, d * GROUP_WIDTH), lambda i: (i, 0)) for d in DILATIONS]
    kern = functools.partial(_mix_kernel, tiles_per_seq=seq // tm, conv_w=conv_width, sg_w=sg_width)
    return pl.pallas_call(
        kern,
        grid=(n_tok // tm,),
        in_specs=[row_tile(d_model), _resident(g, layer)]
        + att_tiles * 2
        + [_resident(w_in, 0)]
        + [_resident(x, layer) for x in (conv_w, ln_g, ln_b, sg_w, sg_b, w_a, w_b, w_c, w_o)],
        out_specs=row_tile(d_model),
        out_shape=jax.ShapeDtypeStruct((n_tok, d_model), F32),
        scratch_shapes=[pltpu.VMEM((SUBLANES, conv_width), F32),
                        pltpu.VMEM((tm, sg_width), BF16),
                        pltpu.VMEM((2 * len(DILATIONS) * (GROUP_WIDTH // LANES), tm, LANES), F32)],
        compiler_params=pltpu.CompilerParams(
            dimension_semantics=("arbitrary",), vmem_limit_bytes=VMEM_LIMIT_BYTES),
        name="mixer_merge",
    )(h2d, g, o0, o1, o2, l0, l1, l2, w_in, conv_w, ln_g, ln_b, sg_w, sg_b, w_a, w_b, w_c, w_o)


def _mlp_kernel(h_ref, gm_ref, wu_ref, wd_ref, p_ref, gp_ref, wg_ref, wp_ref, gf_ref, *refs,
                final_norm):
    n_cast = len(refs) // 2
    out_ref = refs[n_cast]
    for src_ref, dst_ref in zip(refs[:n_cast], refs[n_cast + 1:]):
        dst_ref[...] = src_ref[...].astype(BF16)
    d_ff = wu_ref.shape[1]

    for s in range(h_ref.shape[0] // MLP_SUB_ROWS):
        rs = slice(s * MLP_SUB_ROWS, (s + 1) * MLP_SUB_ROWS)
        h = h_ref[rs, :]
        c = _rms_norm(h, gm_ref[...]).astype(BF16)
        acc = h
        for f0 in range(0, d_ff, MLP_FF_TILE):
            t = jnp.dot(c, wu_ref[:, f0:f0 + MLP_FF_TILE], preferred_element_type=F32)
            t = jnp.square(jnp.maximum(t, 0.0)).astype(BF16)
            acc = acc + jnp.dot(t, wd_ref[f0:f0 + MLP_FF_TILE, :], preferred_element_type=F32)
        e = _rms_norm(acc, gp_ref[...]).astype(BF16)
        gate = _sigmoid(jnp.dot(e, wg_ref[...], preferred_element_type=F32))
        emb = jnp.dot(p_ref[rs, :].astype(BF16), wp_ref[...], preferred_element_type=F32)
        out = acc + gate * emb
        out_ref[rs, :] = _rms_norm(out, gf_ref[...]) if final_norm else out


def _mlp_ple(layer, h2d, g_mlp, w_up, w_down, p3d, g_ple, w_pg, w_pe, g_final, next_weights):
    n_tok, d_model = h2d.shape
    tm = MLP_ROWS
    steps = n_tok // tm
    final_norm = not next_weights
    chunk = lambda w: (None, w.shape[1] // steps, w.shape[2])
    assert all(w.shape[1] % (steps * BF16_SUBLANES) == 0 for w in next_weights)
    row_tile = pl.BlockSpec((tm, d_model), lambda i: (i, 0))
    outs = pl.pallas_call(
        functools.partial(_mlp_kernel, final_norm=final_norm),
        grid=(steps,),
        in_specs=[
            row_tile,
            _resident(g_mlp, layer),
            _resident(w_up, 0),
            _resident(w_down, 0),
            pl.BlockSpec((None, tm, p3d.shape[-1]), lambda i: (layer, i, 0)),
            _resident(g_ple, layer),
            _resident(w_pg, layer),
            _resident(w_pe, layer),
            _resident(g_final, 0),
        ] + [pl.BlockSpec(chunk(w), lambda i: (layer + 1, i, 0)) for w in next_weights],
        out_specs=[row_tile] + [pl.BlockSpec(chunk(w), lambda i: (0, i, 0)) for w in next_weights],
        out_shape=[jax.ShapeDtypeStruct((n_tok, d_model), F32)]
        + [jax.ShapeDtypeStruct((1,) + w.shape[1:], BF16) for w in next_weights],
        compiler_params=pltpu.CompilerParams(
            dimension_semantics=("arbitrary",), vmem_limit_bytes=VMEM_LIMIT_BYTES),
        name="mlp_ple",
    )(h2d, g_mlp, w_up, w_down, p3d, g_ple, w_pg, w_pe, g_final, *next_weights)
    return outs[0], outs[1:]


def _cast_kernel(*refs):
    n = len(refs) // 2
    for src_ref, dst_ref in zip(refs[:n], refs[n:]):
        dst_ref[...] = src_ref[...].astype(BF16)


def _cast_weights(first_layer_of, whole):
    steps = CAST_STEPS
    in_specs, out_specs, out_shape, operands = [], [], [], []
    for w in first_layer_of:
        block = (None, w.shape[1] // steps, w.shape[2])
        in_specs.append(pl.BlockSpec(block, lambda i: (0, i, 0)))
        out_specs.append(pl.BlockSpec(block, lambda i: (0, i, 0)))
        out_shape.append(jax.ShapeDtypeStruct((1,) + w.shape[1:], BF16))
        operands.append(w)
    for w in whole:
        flat = w.reshape(-1, w.shape[-1])
        block = (flat.shape[0] // steps, flat.shape[1])
        in_specs.append(pl.BlockSpec(block, lambda i: (i, 0)))
        out_specs.append(pl.BlockSpec(block, lambda i: (i, 0)))
        out_shape.append(jax.ShapeDtypeStruct(flat.shape, BF16))
        operands.append(flat)
    assert all(spec.block_shape[-2] % BF16_SUBLANES == 0 for spec in out_specs)
    outs = pl.pallas_call(
        _cast_kernel,
        grid=(steps,),
        in_specs=in_specs,
        out_specs=out_specs,
        out_shape=out_shape,
        compiler_params=pltpu.CompilerParams(
            dimension_semantics=("arbitrary",), vmem_limit_bytes=VMEM_LIMIT_BYTES),
        name="cast_weights",
    )(*operands)
    n = len(first_layer_of)
    return outs[:n], [o.reshape(w.shape) for o, w in zip(outs[n:], whole)]


def _rotary_tables(positions):
    inv_freq = ROPE_THETA ** (-(jnp.arange(0, 2 * ROT_HALF, 2, dtype=F32) / (2 * ROT_HALF)))
    n_tok = positions.size
    pack = LANES // (2 * ROT_HALF)
    ang = (positions.astype(F32).reshape(n_tok // pack, pack, 1) * inv_freq).reshape(n_tok // pack, -1)
    cos8 = jnp.cos(ang).reshape(n_tok, ROT_HALF)
    sin8 = jnp.sin(ang).reshape(n_tok, ROT_HALF)
    in_head = jnp.arange(LANES) % HEAD_DIM
    rotated = in_head < 2 * ROT_HALF
    sign = jnp.where(in_head < ROT_HALF, -1.0, 1.0).astype(F32)
    spread = lambda t: jnp.tile(t, (1, LANES // ROT_HALF))
    return (jnp.where(rotated, spread(cos8), 1.0), jnp.where(rotated, sign * spread(sin8), 0.0))


def kernel(x, p, positions, norm_mix_g, w_in, conv_w, sg_ln_g, sg_ln_b, sg_w, sg_b,
           w_branch_a, w_branch_b, w_branch_c, w_out, norm_mlp_g, w_up, w_down,
           norm_ple_g, w_ple_gate, w_ple_proj, norm_final_g):
    batch, seq, d_model = x.shape
    depth = w_in.shape[0]
    n_tok = batch * seq
    qkv_width = 3 * len(DILATIONS) * GROUP_WIDTH
    assert seq % (max(DILATIONS) * ATTN_BLOCK) == 0 and seq % ATTN_ROWS_MAX == 0
    assert n_tok % MLP_ROWS == 0 and seq % MIX_ROWS == 0

    cos_t, sin_t = _rotary_tables(positions)
    rows = lambda v: v.reshape(v.shape[0], 1, v.shape[-1])
    sg_bias = jnp.broadcast_to(sg_b[:, :, :, None], sg_w.shape)
    big_f32 = (w_in, w_up, w_down)
    (w_in_l, w_up_l, w_down_l), (w_a, w_b, w_c, w_o, w_pg, w_pe) = _cast_weights(
        big_f32, (w_branch_a, w_branch_b, w_branch_c, w_out, w_ple_gate, w_ple_proj))
    g_mix, g_mlp, g_ple = rows(norm_mix_g), rows(norm_mlp_g), rows(norm_ple_g)
    ln_g, ln_b = rows(sg_ln_g), rows(sg_ln_b)
    p3d = p.reshape(depth, n_tok, p.shape[-1])

    g_final = norm_final_g.reshape(1, 1, -1)

    h = x.reshape(n_tok, d_model)
    for i in range(depth):
        qkv = _qkv_projection(i, h, g_mix, w_in_l, cos_t, sin_t)
        att = _attention(qkv, batch, seq)
        h = _mixer_merge(i, h, g_mix, att, w_in_l, conv_w, ln_g, ln_b, sg_w, sg_bias,
                         w_a, w_b, w_c, w_o, seq)
        h, nxt = _mlp_ple(i, h, g_mlp, w_up_l, w_down_l, p3d, g_ple, w_pg, w_pe, g_final,
                          big_f32 if i + 1 < depth else ())
        if nxt:
            w_in_l, w_up_l, w_down_l = nxt
    return h.reshape(batch, seq, d_model)
```

```python
import functools
import math

import jax
import jax.numpy as jnp
from jax import lax
from jax.experimental import pallas as pl
from jax.experimental.pallas import tpu as pltpu

F32 = jnp.float32
BF16 = jnp.bfloat16

HEAD_DIM = 64
HEADS_PER_GROUP = 4
GROUP_WIDTH = HEADS_PER_GROUP * HEAD_DIM
DILATIONS = (1, 4, 16)
ATTN_BLOCK = 128
ROT_HALF = 8
ROPE_THETA = 500000.0
CONV_K = 3
SG_CHUNK = 128
SG_GROUPS = 4
RMS_EPS = 1e-6
LN_EPS = 1e-5
QUERY_SCALE = math.log2(math.e) * HEAD_DIM ** -0.5

LANES = 128
SUBLANES = 8
BF16_SUBLANES = 16
FREE_STRIDE = 4
VMEM_LIMIT_BYTES = 56 * 1024 * 1024

QKV_ROWS = 1024
QKV_SUB_ROWS = 256
ATTN_ROWS_MAX = 2048
MIX_ROWS = 512
MIX_SUB_ROWS = 256
MLP_ROWS = 512
MLP_SUB_ROWS = 512
MLP_FF_TILE = 1024
CAST_STEPS = 32


def _rms_norm(h, g):
    ms = jnp.mean(h * h, axis=-1, keepdims=True)
    return h * lax.rsqrt(ms + RMS_EPS) * g


def _sigmoid(x):
    return 1.0 / (1.0 + jnp.exp(-x))


def _gelu(x):
    return 0.5 * x * (1.0 + lax.erf(x * (1.0 / math.sqrt(2.0))))


def _resident(stacked, layer):
    shape = stacked.shape[1:]
    index = (layer,) + (0,) * len(shape)
    return pl.BlockSpec((None,) + shape, lambda *_: index, pipeline_mode=pl.Buffered(1))


def _qkv_kernel(h_ref, g_ref, w_ref, cos_ref, sin_ref, *refs):
    outs, slab_ref, slab2_ref = refs[:-2], refs[-2], refs[-1]
    n_groups = len(outs)
    sub = QKV_SUB_ROWS
    lane = lax.broadcasted_iota(jnp.int32, (sub, LANES), 1)
    low_half = (lane % HEAD_DIM) < ROT_HALF
    slab = slab2 = 0
    for s in range(h_ref.shape[0] // sub):
        rs = slice(s * sub, (s + 1) * sub)
        a = _rms_norm(h_ref[rs, :], g_ref[...]).astype(BF16)
        cos_k, sin_k = cos_ref[rs, :], sin_ref[rs, :]
        cos_sin = ((cos_k * QUERY_SCALE, sin_k * QUERY_SCALE), (cos_k, sin_k))
        for which in range(3):
            for g, d in enumerate(DILATIONS):
                c0 = (which * n_groups + g) * GROUP_WIDTH
                z = jnp.dot(a, w_ref[:, c0:c0 + GROUP_WIDTH], preferred_element_type=F32)
                out_rows = slice(s * sub // d, (s + 1) * sub // d)
                for half in range(GROUP_WIDTH // LANES):
                    t = z[:, half * LANES:(half + 1) * LANES]
                    if which < 2:
                        partner = jnp.where(low_half,
                                            pltpu.roll(t, LANES - ROT_HALF, 1),
                                            pltpu.roll(t, ROT_HALF, 1))
                        cos_t, sin_t = cos_sin[which]
                        t = t * cos_t + partner * sin_t
                    if d == 1:
                        outs[g][which, out_rows, half * LANES:(half + 1) * LANES] = t.astype(BF16)
                    else:
                        slab_ref[slab] = t
                        d1 = d if d <= FREE_STRIDE else FREE_STRIDE
                        d2 = d // d1
                        for r1 in range(d1):
                            part = slab_ref[slab, pl.ds(r1, sub // d1, stride=d1), :]
                            if d2 == 1:
                                l0 = r1 * GROUP_WIDTH + half * LANES
                                outs[g][which, out_rows, l0:l0 + LANES] = part.astype(BF16)
                                continue
                            slab2_ref[slab2] = part
                            for r2 in range(d2):
                                l0 = (r1 + d1 * r2) * GROUP_WIDTH + half * LANES
                                rows = slab2_ref[slab2, pl.ds(r2, sub // d, stride=d2), :]
                                outs[g][which, out_rows, l0:l0 + LANES] = rows.astype(BF16)
                            slab2 += 1
                        slab += 1


def _qkv_projection(layer, h2d, g, w_in, cos_t, sin_t):
    n_tok, d_model = h2d.shape
    tm = QKV_ROWS
    qkv_width = 3 * len(DILATIONS) * GROUP_WIDTH
    per_group = 3 * (GROUP_WIDTH // LANES) * (tm // QKV_SUB_ROWS)
    n_slabs = per_group * sum(d > 1 for d in DILATIONS)
    n_slabs2 = per_group * FREE_STRIDE * sum(d > FREE_STRIDE for d in DILATIONS)
    row_tile = pl.BlockSpec((tm, d_model), lambda i: (i, 0))
    out_specs = [pl.BlockSpec((3, tm // d, d * GROUP_WIDTH), lambda i: (0, i, 0)) for d in DILATIONS]
    out_shape = [jax.ShapeDtypeStruct((3, n_tok // d, d * GROUP_WIDTH), BF16) for d in DILATIONS]
    return pl.pallas_call(
        _qkv_kernel,
        grid=(n_tok // tm,),
        in_specs=[row_tile, _resident(g, layer)] + [
            pl.BlockSpec((None, d_model, qkv_width), lambda i: (0, 0, 0),
                         pipeline_mode=pl.Buffered(1)),
            pl.BlockSpec((tm, LANES), lambda i: (i, 0)),
            pl.BlockSpec((tm, LANES), lambda i: (i, 0)),
        ],
        out_specs=out_specs,
        out_shape=out_shape,
        scratch_shapes=[pltpu.VMEM((n_slabs, QKV_SUB_ROWS, LANES), F32),
                        pltpu.VMEM((max(n_slabs2, 1), QKV_SUB_ROWS // FREE_STRIDE, LANES), F32)],
        compiler_params=pltpu.CompilerParams(
            dimension_semantics=("arbitrary",), vmem_limit_bytes=VMEM_LIMIT_BYTES),
        name="qkv_proj",
    )(h2d, g, w_in, cos_t, sin_t)


def _attend(q_ref, k_ref, kp_ref, v_ref, vp_ref, o_ref, l_ref, first_step):
    def window(ref, prev_ref, r0, c0):
        if r0 > 0:
            return ref[r0 - ATTN_BLOCK:r0 + ATTN_BLOCK, c0:c0 + LANES]
        cur = ref[0:ATTN_BLOCK, c0:c0 + LANES]
        before = jnp.zeros_like(cur) if prev_ref is None else prev_ref[:, c0:c0 + LANES]
        return jnp.concatenate([before, cur], axis=0)

    two = 2 * ATTN_BLOCK
    qi = lax.broadcasted_iota(jnp.int32, (two, two), 0) % ATTN_BLOCK
    kj = lax.broadcasted_iota(jnp.int32, (two, two), 1)
    dist = qi + ATTN_BLOCK - kj
    band = (dist >= 0) & (dist <= ATTN_BLOCK)
    band_first = band & ((kj >= ATTN_BLOCK) | jnp.logical_not(first_step))
    mask_rest = jnp.where(band, 0.0, -jnp.inf)
    mask_first = jnp.where(band_first, 0.0, -jnp.inf)
    lane = lax.broadcasted_iota(jnp.int32, (ATTN_BLOCK, LANES), 1)
    head0 = lane < HEAD_DIM
    keep0 = jnp.where(head0, 1.0, 0.0).astype(BF16)
    keep1 = jnp.where(head0, 0.0, 1.0).astype(BF16)
    ones = jnp.ones((two, LANES), BF16)

    for i in range(q_ref.shape[0] // ATTN_BLOCK):
        mask = mask_first if i == 0 else mask_rest
        r0 = i * ATTN_BLOCK
        for p in range(q_ref.shape[1] // LANES):
            c0 = p * LANES
            q2 = q_ref[r0:r0 + ATTN_BLOCK, c0:c0 + LANES]
            qs = jnp.concatenate([q2 * keep0, q2 * keep1], axis=0)
            kk = window(k_ref, kp_ref, r0, c0)
            s = lax.dot_general(qs, kk, (((1,), (1,)), ((), ())),
                                preferred_element_type=F32)
            s = s + mask
            m = jnp.max(s, axis=-1, keepdims=True)
            e = jnp.exp2((s - m).astype(BF16))
            vv = jnp.concatenate([window(v_ref, vp_ref, r0, c0), ones], axis=1)
            pv = jnp.dot(e, vv, preferred_element_type=F32)
            pick = lambda x: jnp.where(head0, x[:ATTN_BLOCK], x[ATTN_BLOCK:])
            den = pick(pv[:, LANES:])
            o_ref[r0:r0 + ATTN_BLOCK, c0:c0 + LANES] = pick(pv[:, :LANES]) * (1.0 / den)
            l_ref[r0:r0 + ATTN_BLOCK, c0:c0 + LANES] = pick(m) + jnp.log2(den)


def _attn_kernel(*refs, plans):
    refs = iter(refs)
    inputs = []
    for has_prev, _ in plans:
        q_ref, k_ref = next(refs), next(refs)
        kp_ref = next(refs) if has_prev else None
        v_ref = next(refs)
        vp_ref = next(refs) if has_prev else None
        inputs.append((q_ref, k_ref, kp_ref, v_ref, vp_ref))
    for (has_prev, row_steps), group_inputs in zip(plans, inputs):
        first_step = (pl.program_id(1) % row_steps == 0) if has_prev else True
        _attend(*group_inputs, next(refs), next(refs), first_step)


def _attention(qkv, batch, seq):
    in_specs, operands, out_specs, out_shapes, plans, steps = [], [], [], [], [], set()
    for qkv_g, dilation in zip(qkv, DILATIONS):
        m_len = seq // dilation
        mb = min(m_len, ATTN_ROWS_MAX)
        nblk = mb // ATTN_BLOCK
        n_res = min(dilation, ATTN_ROWS_MAX // mb)
        width = n_res * GROUP_WIDTH
        row_steps = m_len // mb
        has_prev = row_steps > 1
        steps.add(row_steps * (dilation // n_res))
        view = qkv_g.reshape(3, batch, m_len, dilation * GROUP_WIDTH)

        def rows(which, row_steps=row_steps, mb=mb, width=width):
            return pl.BlockSpec((None, None, mb, width),
                                lambda b, t: (which, b, t % row_steps, t // row_steps))

        def prev(which, row_steps=row_steps, nblk=nblk, width=width):
            return pl.BlockSpec(
                (None, None, ATTN_BLOCK, width),
                lambda b, t: (which, b, jnp.maximum((t % row_steps) * nblk - 1, 0), t // row_steps))

        if has_prev:
            specs = [rows(0), rows(1), prev(1), rows(2), prev(2)]
        else:
            specs = [rows(0), rows(1), rows(2)]
        in_specs += specs
        operands += [view] * len(specs)
        out_spec = pl.BlockSpec((None, mb, width),
                                lambda b, t, row_steps=row_steps: (b, t % row_steps, t // row_steps))
        out_specs += [out_spec, out_spec]
        out_shapes += [jax.ShapeDtypeStruct((batch, m_len, dilation * GROUP_WIDTH), F32)] * 2
        plans.append((has_prev, row_steps))
    (n_steps,) = steps
    outs = pl.pallas_call(
        functools.partial(_attn_kernel, plans=tuple(plans)),
        grid=(batch, n_steps),
        in_specs=in_specs,
        out_specs=out_specs,
        out_shape=out_shapes,
        compiler_params=pltpu.CompilerParams(
            dimension_semantics=("arbitrary", "arbitrary"), vmem_limit_bytes=VMEM_LIMIT_BYTES),
        name="dilated_attn",
    )(*operands)
    flat = [x.reshape(-1, x.shape[-1]) for x in outs]
    return list(zip(flat[0::2], flat[1::2]))


def _mix_kernel(h_ref, g_ref, o0_ref, o1_ref, o2_ref, l0_ref, l1_ref, l2_ref,
                wr_ref, cw_ref, lng_ref, lnb_ref, sgw_ref, sgb_ref,
                wa_ref, wb_ref, wc_ref, wo_ref, out_ref, carry_ref, yc_ref, nat_ref,
                *, tiles_per_seq, conv_w, sg_w):
    tm, d_model = h_ref.shape
    halves = GROUP_WIDTH // LANES
    w_skip = 3 * len(DILATIONS) * GROUP_WIDTH

    @pl.when(pl.program_id(0) % tiles_per_seq == 0)
    def _():
        carry_ref[...] = jnp.zeros_like(carry_ref)

    att_refs = (o0_ref, l0_ref, o1_ref, l1_ref, o2_ref, l2_ref)
    for slot, src_ref in enumerate(att_refs):
        d = DILATIONS[slot // 2]
        for r in range(d if d > 1 else 0):
            for half in range(halves):
                l0 = r * GROUP_WIDTH + half * LANES
                nat_ref[slot * halves + half, pl.ds(r, tm // d, stride=d), :] = src_ref[:, l0:l0 + LANES]

    def natural(slot, rs):
        if DILATIONS[slot // 2] == 1:
            return att_refs[slot][rs, :]
        return jnp.concatenate([nat_ref[slot * halves + half, rs, :] for half in range(halves)], axis=1)

    c_b, c_c, c_u, c_v, c_g = conv_w, 2 * conv_w, 3 * conv_w, 3 * conv_w + sg_w, 3 * conv_w + 2 * sg_w
    ti = lax.broadcasted_iota(jnp.int32, (SG_CHUNK, SG_CHUNK), 0)
    si = lax.broadcasted_iota(jnp.int32, (SG_CHUNK, SG_CHUNK), 1)
    gch = sg_w // SG_GROUPS
    w_sg = [jnp.where(ti >= si, sgw_ref[g], 0.0).astype(BF16) for g in range(SG_GROUPS)]
    cw = cw_ref[...]
    carry = carry_ref[...]

    for s in range(tm // MIX_SUB_ROWS):
        r0 = s * MIX_SUB_ROWS
        rs = slice(r0, r0 + MIX_SUB_ROWS)
        h = h_ref[rs, :]
        a = _rms_norm(h, g_ref[...]).astype(BF16)

        def project(c0, c1):
            return jnp.dot(a, wr_ref[:, w_skip + c0:w_skip + c1], preferred_element_type=F32)

        z_conv = project(0, c_u)
        z_sg = project(c_u, c_g)

        zx, zb, zc = z_conv[:, :c_b], z_conv[:, c_b:c_c], z_conv[:, c_c:c_u]
        u = zc * zx
        ext = jnp.concatenate([carry, u], axis=0)
        u1 = pltpu.roll(ext, 1, 0)[SUBLANES:]
        u2 = pltpu.roll(ext, 2, 0)[SUBLANES:]
        carry = u[MIX_SUB_ROWS - SUBLANES:]
        yb = (zb * (cw[0:1] * u2 + cw[1:2] * u1 + cw[2:3] * u)).astype(BF16)

        z_g0 = project(c_g, c_g + d_model)

        l0, l1, l2 = natural(1, rs), natural(3, rs), natural(5, rs)
        mx = jnp.maximum(jnp.maximum(l0, l1), l2)
        e0, e1, e2 = jnp.exp2(l0 - mx), jnp.exp2(l1 - mx), jnp.exp2(l2 - mx)
        inv = 1.0 / (e0 + e1 + e2)
        ya = jnp.concatenate([natural(0, rs) * (e0 * inv), natural(2, rs) * (e1 * inv),
                              natural(4, rs) * (e2 * inv)], axis=1).astype(BF16)
        pa = jnp.dot(ya, wa_ref[...], preferred_element_type=F32)

        us = _gelu(z_sg[:, :sg_w])
        vs = _gelu(z_sg[:, sg_w:])
        mu = jnp.mean(vs, axis=-1, keepdims=True)
        xc = vs - mu
        var = jnp.mean(xc * xc, axis=-1, keepdims=True)
        vln = (xc * lax.rsqrt(var + LN_EPS) * lng_ref[...] + lnb_ref[...]).astype(BF16)

        z_g1 = project(c_g + d_model, c_g + 2 * d_model)
        pb = jnp.dot(yb, wb_ref[...], preferred_element_type=F32)
        m = _sigmoid(z_g0) * pa + _sigmoid(z_g1) * pb

        for g in range(SG_GROUPS):
            bias = sgb_ref[g]
            cs = slice(g * gch, (g + 1) * gch)
            for n in range(MIX_SUB_ROWS // SG_CHUNK):
                ns = slice(n * SG_CHUNK, (n + 1) * SG_CHUNK)
                sv = jnp.dot(w_sg[g], vln[ns, cs], preferred_element_type=F32) + bias
                yc_ref[r0 + n * SG_CHUNK:r0 + (n + 1) * SG_CHUNK, cs] = (us[ns, cs] * sv).astype(BF16)

        z_g2 = project(c_g + 2 * d_model, c_g + 3 * d_model)
        pc = jnp.dot(yc_ref[rs, :], wc_ref[...], preferred_element_type=F32)
        m = m + _sigmoid(z_g2) * pc
        out_ref[rs, :] = h + jnp.dot(m.astype(BF16), wo_ref[...], preferred_element_type=F32)

    carry_ref[...] = carry


def _mixer_merge(layer, h2d, g, att, w_in, conv_w, ln_g, ln_b, sg_w, sg_b, w_a, w_b, w_c, w_o, seq):
    n_tok, d_model = h2d.shape
    tm = MIX_ROWS
    conv_width = conv_w.shape[-1]
    sg_width = ln_g.shape[-1]
    row_tile = lambda width: pl.BlockSpec((tm, width), lambda i: (i, 0))
    (o0, l0), (o1, l1), (o2, l2) = att
    att_tiles = [pl.BlockSpec((tm // d, d * GROUP_WIDTH), lambda i: (i, 0)) for d in DILATIONS]
    kern = functools.partial(_mix_kernel, tiles_per_seq=seq // tm, conv_w=conv_width, sg_w=sg_width)
    return pl.pallas_call(
        kern,
        grid=(n_tok // tm,),
        in_specs=[row_tile(d_model), _resident(g, layer)]
        + att_tiles * 2
        + [_resident(w_in, 0)]
        + [_resident(x, layer) for x in (conv_w, ln_g, ln_b, sg_w, sg_b, w_a, w_b, w_c, w_o)],
        out_specs=row_tile(d_model),
        out_shape=jax.ShapeDtypeStruct((n_tok, d_model), F32),
        scratch_shapes=[pltpu.VMEM((SUBLANES, conv_width), F32),
                        pltpu.VMEM((tm, sg_width), BF16),
                        pltpu.VMEM((2 * len(DILATIONS) * (GROUP_WIDTH // LANES), tm, LANES), F32)],
        compiler_params=pltpu.CompilerParams(
            dimension_semantics=("arbitrary",), vmem_limit_bytes=VMEM_LIMIT_BYTES),
        name="mixer_merge",
    )(h2d, g, o0, o1, o2, l0, l1, l2, w_in, conv_w, ln_g, ln_b, sg_w, sg_b, w_a, w_b, w_c, w_o)


def _mlp_kernel(h_ref, gm_ref, wu_ref, wd_ref, p_ref, gp_ref, wg_ref, wp_ref, gf_ref, *refs,
                final_norm):
    n_cast = len(refs) // 2
    out_ref = refs[n_cast]
    for src_ref, dst_ref in zip(refs[:n_cast], refs[n_cast + 1:]):
        dst_ref[...] = src_ref[...].astype(BF16)
    d_ff = wu_ref.shape[1]

    for s in range(h_ref.shape[0] // MLP_SUB_ROWS):
        rs = slice(s * MLP_SUB_ROWS, (s + 1) * MLP_SUB_ROWS)
        h = h_ref[rs, :]
        c = _rms_norm(h, gm_ref[...]).astype(BF16)
        acc = h
        for f0 in range(0, d_ff, MLP_FF_TILE):
            t = jnp.dot(c, wu_ref[:, f0:f0 + MLP_FF_TILE], preferred_element_type=F32)
            t = jnp.square(jnp.maximum(t, 0.0)).astype(BF16)
            acc = acc + jnp.dot(t, wd_ref[f0:f0 + MLP_FF_TILE, :], preferred_element_type=F32)
        e = _rms_norm(acc, gp_ref[...]).astype(BF16)
        gate = _sigmoid(jnp.dot(e, wg_ref[...], preferred_element_type=F32))
        emb = jnp.dot(p_ref[rs, :].astype(BF16), wp_ref[...], preferred_element_type=F32)
        out = acc + gate * emb
        out_ref[rs, :] = _rms_norm(out, gf_ref[...]) if final_norm else out


def _mlp_ple(layer, h2d, g_mlp, w_up, w_down, p3d, g_ple, w_pg, w_pe, g_final, next_weights):
    n_tok, d_model = h2d.shape
    tm = MLP_ROWS
    steps = n_tok // tm
    final_norm = not next_weights
    chunk = lambda w: (None, w.shape[1] // steps, w.shape[2])
    assert all(w.shape[1] % (steps * BF16_SUBLANES) == 0 for w in next_weights)
    row_tile = pl.BlockSpec((tm, d_model), lambda i: (i, 0))
    outs = pl.pallas_call(
        functools.partial(_mlp_kernel, final_norm=final_norm),
        grid=(steps,),
        in_specs=[
            row_tile,
            _resident(g_mlp, layer),
            _resident(w_up, 0),
            _resident(w_down, 0),
            pl.BlockSpec((None, tm, p3d.shape[-1]), lambda i: (layer, i, 0)),
            _resident(g_ple, layer),
            _resident(w_pg, layer),
            _resident(w_pe, layer),
            _resident(g_final, 0),
        ] + [pl.BlockSpec(chunk(w), lambda i: (layer + 1, i, 0)) for w in next_weights],
        out_specs=[row_tile] + [pl.BlockSpec(chunk(w), lambda i: (0, i, 0)) for w in next_weights],
        out_shape=[jax.ShapeDtypeStruct((n_tok, d_model), F32)]
        + [jax.ShapeDtypeStruct((1,) + w.shape[1:], BF16) for w in next_weights],
        compiler_params=pltpu.CompilerParams(
            dimension_semantics=("arbitrary",), vmem_limit_bytes=VMEM_LIMIT_BYTES),
        name="mlp_ple",
    )(h2d, g_mlp, w_up, w_down, p3d, g_ple, w_pg, w_pe, g_final, *next_weights)
    return outs[0], outs[1:]


def _cast_kernel(*refs):
    n = len(refs) // 2
    for src_ref, dst_ref in zip(refs[:n], refs[n:]):
        dst_ref[...] = src_ref[...].astype(BF16)


def _cast_weights(first_layer_of, whole):
    steps = CAST_STEPS
    in_specs, out_specs, out_shape, operands = [], [], [], []
    for w in first_layer_of:
        block = (None, w.shape[1] // steps, w.shape[2])
        in_specs.append(pl.BlockSpec(block, lambda i: (0, i, 0)))
        out_specs.append(pl.BlockSpec(block, lambda i: (0, i, 0)))
        out_shape.append(jax.ShapeDtypeStruct((1,) + w.shape[1:], BF16))
        operands.append(w)
    for w in whole:
        flat = w.reshape(-1, w.shape[-1])
        block = (flat.shape[0] // steps, flat.shape[1])
        in_specs.append(pl.BlockSpec(block, lambda i: (i, 0)))
        out_specs.append(pl.BlockSpec(block, lambda i: (i, 0)))
        out_shape.append(jax.ShapeDtypeStruct(flat.shape, BF16))
        operands.append(flat)
    assert all(spec.block_shape[-2] % BF16_SUBLANES == 0 for spec in out_specs)
    outs = pl.pallas_call(
        _cast_kernel,
        grid=(steps,),
        in_specs=in_specs,
        out_specs=out_specs,
        out_shape=out_shape,
        compiler_params=pltpu.CompilerParams(
            dimension_semantics=("arbitrary",), vmem_limit_bytes=VMEM_LIMIT_BYTES),
        name="cast_weights",
    )(*operands)
    n = len(first_layer_of)
    return outs[:n], [o.reshape(w.shape) for o, w in zip(outs[n:], whole)]


def _rotary_tables(positions):
    inv_freq = ROPE_THETA ** (-(jnp.arange(0, 2 * ROT_HALF, 2, dtype=F32) / (2 * ROT_HALF)))
    n_tok = positions.size
    pack = LANES // (2 * ROT_HALF)
    ang = (positions.astype(F32).reshape(n_tok // pack, pack, 1) * inv_freq).reshape(n_tok // pack, -1)
    cos8 = jnp.cos(ang).reshape(n_tok, ROT_HALF)
    sin8 = jnp.sin(ang).reshape(n_tok, ROT_HALF)
    in_head = jnp.arange(LANES) % HEAD_DIM
    rotated = in_head < 2 * ROT_HALF
    sign = jnp.where(in_head < ROT_HALF, -1.0, 1.0).astype(F32)
    spread = lambda t: jnp.tile(t, (1, LANES // ROT_HALF))
    return (jnp.where(rotated, spread(cos8), 1.0), jnp.where(rotated, sign * spread(sin8), 0.0))


def kernel(x, p, positions, norm_mix_g, w_in, conv_w, sg_ln_g, sg_ln_b, sg_w, sg_b,
           w_branch_a, w_branch_b, w_branch_c, w_out, norm_mlp_g, w_up, w_down,
           norm_ple_g, w_ple_gate, w_ple_proj, norm_final_g):
    batch, seq, d_model = x.shape
    depth = w_in.shape[0]
    n_tok = batch * seq
    qkv_width = 3 * len(DILATIONS) * GROUP_WIDTH
    assert seq % (max(DILATIONS) * ATTN_BLOCK) == 0 and seq % ATTN_ROWS_MAX == 0
    assert n_tok % MLP_ROWS == 0 and seq % MIX_ROWS == 0

    cos_t, sin_t = _rotary_tables(positions)
    rows = lambda v: v.reshape(v.shape[0], 1, v.shape[-1])
    sg_bias = jnp.broadcast_to(sg_b[:, :, :, None], sg_w.shape)
    big_f32 = (w_in, w_up, w_down)
    (w_in_l, w_up_l, w_down_l), (w_a, w_b, w_c, w_o, w_pg, w_pe) = _cast_weights(
        big_f32, (w_branch_a, w_branch_b, w_branch_c, w_out, w_ple_gate, w_ple_proj))
    g_mix, g_mlp, g_ple = rows(norm_mix_g), rows(norm_mlp_g), rows(norm_ple_g)
    ln_g, ln_b = rows(sg_ln_g), rows(sg_ln_b)
    p3d = p.reshape(depth, n_tok, p.shape[-1])

    g_final = norm_final_g.reshape(1, 1, -1)

    h = x.reshape(n_tok, d_model)
    for i in range(depth):
        qkv = _qkv_projection(i, h, g_mix, w_in_l, cos_t, sin_t)
        att = _attention(qkv, batch, seq)
        h = _mixer_merge(i, h, g_mix, att, w_in_l, conv_w, ln_g, ln_b, sg_w, sg_bias,
                         w_a, w_b, w_c, w_o, seq)
        h, nxt = _mlp_ple(i, h, g_mlp, w_up_l, w_down_l, p3d, g_ple, w_pg, w_pe, g_final,
                          big_f32 if i + 1 < depth else ())
        if nxt:
            w_in_l, w_up_l, w_down_l = nxt
    return h.reshape(batch, seq, d_model)
```

```python
import functools
import math

import jax
import jax.numpy as jnp
from jax import lax
from jax.experimental import pallas as pl
from jax.experimental.pallas import tpu as pltpu

F32 = jnp.float32
BF16 = jnp.bfloat16

HEAD_DIM = 64
HEADS_PER_GROUP = 4
GROUP_WIDTH = HEADS_PER_GROUP * HEAD_DIM
DILATIONS = (1, 4, 16)
ATTN_BLOCK = 128
ROT_HALF = 8
ROPE_THETA = 500000.0
CONV_K = 3
SG_CHUNK = 128
SG_GROUPS = 4
RMS_EPS = 1e-6
LN_EPS = 1e-5
QUERY_SCALE = math.log2(math.e) * HEAD_DIM ** -0.5

LANES = 128
SUBLANES = 8
BF16_SUBLANES = 16
FREE_STRIDE = 4
VMEM_LIMIT_BYTES = 56 * 1024 * 1024

QKV_ROWS = 1024
QKV_SUB_ROWS = 256
ATTN_ROWS_MAX = 2048
MIX_ROWS = 512
MIX_SUB_ROWS = 256
MLP_ROWS = 512
MLP_SUB_ROWS = 512
MLP_FF_TILE = 1024
CAST_STEPS = 32


def _rms_norm(h, g):
    ms = jnp.mean(h * h, axis=-1, keepdims=True)
    return h * lax.rsqrt(ms + RMS_EPS) * g


def _sigmoid(x):
    return 1.0 / (1.0 + jnp.exp(-x))


def _gelu(x):
    return 0.5 * x * (1.0 + lax.erf(x * (1.0 / math.sqrt(2.0))))


def _resident(stacked, layer):
    shape = stacked.shape[1:]
    index = (layer,) + (0,) * len(shape)
    return pl.BlockSpec((None,) + shape, lambda *_: index, pipeline_mode=pl.Buffered(1))


def _qkv_kernel(h_ref, g_ref, w_ref, cos_ref, sin_ref, *refs):
    outs, slab_ref, slab2_ref = refs[:-2], refs[-2], refs[-1]
    n_groups = len(outs)
    sub = QKV_SUB_ROWS
    lane = lax.broadcasted_iota(jnp.int32, (sub, LANES), 1)
    low_half = (lane % HEAD_DIM) < ROT_HALF
    slab = slab2 = 0
    for s in range(h_ref.shape[0] // sub):
        rs = slice(s * sub, (s + 1) * sub)
        a = _rms_norm(h_ref[rs, :], g_ref[...]).astype(BF16)
        cos_k, sin_k = cos_ref[rs, :], sin_ref[rs, :]
        cos_sin = ((cos_k * QUERY_SCALE, sin_k * QUERY_SCALE), (cos_k, sin_k))
        for which in range(3):
            for g, d in enumerate(DILATIONS):
                c0 = (which * n_groups + g) * GROUP_WIDTH
                z = jnp.dot(a, w_ref[:, c0:c0 + GROUP_WIDTH], preferred_element_type=F32)
                out_rows = slice(s * sub // d, (s + 1) * sub // d)
                for half in range(GROUP_WIDTH // LANES):
                    t = z[:, half * LANES:(half + 1) * LANES]
                    if which < 2:
                        partner = jnp.where(low_half,
                                            pltpu.roll(t, LANES - ROT_HALF, 1),
                                            pltpu.roll(t, ROT_HALF, 1))
                        cos_t, sin_t = cos_sin[which]
                        t = t * cos_t + partner * sin_t
                    if d == 1:
                        outs[g][which, out_rows, half * LANES:(half + 1) * LANES] = t.astype(BF16)
                    else:
                        slab_ref[slab] = t
                        d1 = d if d <= FREE_STRIDE else FREE_STRIDE
                        d2 = d // d1
                        for r1 in range(d1):
                            part = slab_ref[slab, pl.ds(r1, sub // d1, stride=d1), :]
                            if d2 == 1:
                                l0 = r1 * GROUP_WIDTH + half * LANES
                                outs[g][which, out_rows, l0:l0 + LANES] = part.astype(BF16)
                                continue
                            slab2_ref[slab2] = part
                            for r2 in range(d2):
                                l0 = (r1 + d1 * r2) * GROUP_WIDTH + half * LANES
                                rows = slab2_ref[slab2, pl.ds(r2, sub // d, stride=d2), :]
                                outs[g][which, out_rows, l0:l0 + LANES] = rows.astype(BF16)
                            slab2 += 1
                        slab += 1


def _qkv_projection(layer, h2d, g, w_in, cos_t, sin_t):
    n_tok, d_model = h2d.shape
    tm = QKV_ROWS
    qkv_width = 3 * len(DILATIONS) * GROUP_WIDTH
    per_group = 3 * (GROUP_WIDTH // LANES) * (tm // QKV_SUB_ROWS)
    n_slabs = per_group * sum(d > 1 for d in DILATIONS)
    n_slabs2 = per_group * FREE_STRIDE * sum(d > FREE_STRIDE for d in DILATIONS)
    row_tile = pl.BlockSpec((tm, d_model), lambda i: (i, 0))
    out_specs = [pl.BlockSpec((3, tm // d, d * GROUP_WIDTH), lambda i: (0, i, 0)) for d in DILATIONS]
    out_shape = [jax.ShapeDtypeStruct((3, n_tok // d, d * GROUP_WIDTH), BF16) for d in DILATIONS]
    return pl.pallas_call(
        _qkv_kernel,
        grid=(n_tok // tm,),
        in_specs=[row_tile, _resident(g, layer)] + [
            pl.BlockSpec((None, d_model, qkv_width), lambda i: (0, 0, 0),
                         pipeline_mode=pl.Buffered(1)),
            pl.BlockSpec((tm, LANES), lambda i: (i, 0)),
            pl.BlockSpec((tm, LANES), lambda i: (i, 0)),
        ],
        out_specs=out_specs,
        out_shape=out_shape,
        scratch_shapes=[pltpu.VMEM((n_slabs, QKV_SUB_ROWS, LANES), F32),
                        pltpu.VMEM((max(n_slabs2, 1), QKV_SUB_ROWS // FREE_STRIDE, LANES), F32)],
        compiler_params=pltpu.CompilerParams(
            dimension_semantics=("arbitrary",), vmem_limit_bytes=VMEM_LIMIT_BYTES),
        name="qkv_proj",
    )(h2d, g, w_in, cos_t, sin_t)


def _attend(q_ref, k_ref, kp_ref, v_ref, vp_ref, o_ref, l_ref, first_step):
    def window(ref, prev_ref, r0, c0):
        if r0 > 0:
            return ref[r0 - ATTN_BLOCK:r0 + ATTN_BLOCK, c0:c0 + LANES]
        cur = ref[0:ATTN_BLOCK, c0:c0 + LANES]
        before = jnp.zeros_like(cur) if prev_ref is None else prev_ref[:, c0:c0 + LANES]
        return jnp.concatenate([before, cur], axis=0)

    two = 2 * ATTN_BLOCK
    qi = lax.broadcasted_iota(jnp.int32, (two, two), 0) % ATTN_BLOCK
    kj = lax.broadcasted_iota(jnp.int32, (two, two), 1)
    dist = qi + ATTN_BLOCK - kj
    band = (dist >= 0) & (dist <= ATTN_BLOCK)
    band_first = band & ((kj >= ATTN_BLOCK) | jnp.logical_not(first_step))
    mask_rest = jnp.where(band, 0.0, -jnp.inf)
    mask_first = jnp.where(band_first, 0.0, -jnp.inf)
    lane = lax.broadcasted_iota(jnp.int32, (ATTN_BLOCK, LANES), 1)
    head0 = lane < HEAD_DIM
    keep0 = jnp.where(head0, 1.0, 0.0).astype(BF16)
    keep1 = jnp.where(head0, 0.0, 1.0).astype(BF16)
    ones = jnp.ones((two, LANES), BF16)

    for i in range(q_ref.shape[0] // ATTN_BLOCK):
        mask = mask_first if i == 0 else mask_rest
        r0 = i * ATTN_BLOCK
        for p in range(q_ref.shape[1] // LANES):
            c0 = p * LANES
            q2 = q_ref[r0:r0 + ATTN_BLOCK, c0:c0 + LANES]
            qs = jnp.concatenate([q2 * keep0, q2 * keep1], axis=0)
            kk = window(k_ref, kp_ref, r0, c0)
            s = lax.dot_general(qs, kk, (((1,), (1,)), ((), ())),
                                preferred_element_type=F32)
            s = s + mask
            m = jnp.max(s, axis=-1, keepdims=True)
            e = jnp.exp2((s - m).astype(BF16))
            vv = jnp.concatenate([window(v_ref, vp_ref, r0, c0), ones], axis=1)
            pv = jnp.dot(e, vv, preferred_element_type=F32)
            pick = lambda x: jnp.where(head0, x[:ATTN_BLOCK], x[ATTN_BLOCK:])
            den = pick(pv[:, LANES:])
            o_ref[r0:r0 + ATTN_BLOCK, c0:c0 + LANES] = pick(pv[:, :LANES]) * (1.0 / den)
            l_ref[r0:r0 + ATTN_BLOCK, c0:c0 + LANES] = pick(m) + jnp.log2(den)


def _attn_kernel(*refs, plans):
    refs = iter(refs)
    inputs = []
    for has_prev, _ in plans:
        q_ref, k_ref = next(refs), next(refs)
        kp_ref = next(refs) if has_prev else None
        v_ref = next(refs)
        vp_ref = next(refs) if has_prev else None
        inputs.append((q_ref, k_ref, kp_ref, v_ref, vp_ref))
    for (has_prev, row_steps), group_inputs in zip(plans, inputs):
        first_step = (pl.program_id(1) % row_steps == 0) if has_prev else True
        _attend(*group_inputs, next(refs), next(refs), first_step)


def _attention(qkv, batch, seq):
    in_specs, operands, out_specs, out_shapes, plans, steps = [], [], [], [], [], set()
    for qkv_g, dilation in zip(qkv, DILATIONS):
        m_len = seq // dilation
        mb = min(m_len, ATTN_ROWS_MAX)
        nblk = mb // ATTN_BLOCK
        n_res = min(dilation, ATTN_ROWS_MAX // mb)
        width = n_res * GROUP_WIDTH
        row_steps = m_len // mb
        has_prev = row_steps > 1
        steps.add(row_steps * (dilation // n_res))
        view = qkv_g.reshape(3, batch, m_len, dilation * GROUP_WIDTH)

        def rows(which, row_steps=row_steps, mb=mb, width=width):
            return pl.BlockSpec((None, None, mb, width),
                                lambda b, t: (which, b, t % row_steps, t // row_steps))

        def prev(which, row_steps=row_steps, nblk=nblk, width=width):
            return pl.BlockSpec(
                (None, None, ATTN_BLOCK, width),
                lambda b, t: (which, b, jnp.maximum((t % row_steps) * nblk - 1, 0), t // row_steps))

        if has_prev:
            specs = [rows(0), rows(1), prev(1), rows(2), prev(2)]
        else:
            specs = [rows(0), rows(1), rows(2)]
        in_specs += specs
        operands += [view] * len(specs)
        out_spec = pl.BlockSpec((None, mb, width),
                                lambda b, t, row_steps=row_steps: (b, t % row_steps, t // row_steps))
        out_specs += [out_spec, out_spec]
        out_shapes += [jax.ShapeDtypeStruct((batch, m_len, dilation * GROUP_WIDTH), F32)] * 2
        plans.append((has_prev, row_steps))
    (n_steps,) = steps
    outs = pl.pallas_call(
        functools.partial(_attn_kernel, plans=tuple(plans)),
        grid=(batch, n_steps),
        in_specs=in_specs,
        out_specs=out_specs,
        out_shape=out_shapes,
        compiler_params=pltpu.CompilerParams(
            dimension_semantics=("arbitrary", "arbitrary"), vmem_limit_bytes=VMEM_LIMIT_BYTES),
        name="dilated_attn",
    )(*operands)
    flat = [x.reshape(-1, x.shape[-1]) for x in outs]
    return list(zip(flat[0::2], flat[1::2]))


def _mix_kernel(h_ref, g_ref, o0_ref, o1_ref, o2_ref, l0_ref, l1_ref, l2_ref,
                wr_ref, cw_ref, lng_ref, lnb_ref, sgw_ref, sgb_ref,
                wa_ref, wb_ref, wc_ref, wo_ref, out_ref, carry_ref, yc_ref, nat_ref, stage_ref,
                *, tiles_per_seq, conv_w, sg_w):
    tm, d_model = h_ref.shape
    halves = GROUP_WIDTH // LANES
    w_skip = 3 * len(DILATIONS) * GROUP_WIDTH

    @pl.when(pl.program_id(0) % tiles_per_seq == 0)
    def _():
        carry_ref[...] = jnp.zeros_like(carry_ref)

    att_refs = (o0_ref, l0_ref, o1_ref, l1_ref, o2_ref, l2_ref)
    stage = 0
    for slot, src_ref in enumerate(att_refs):
        d = DILATIONS[slot // 2]
        if d == 1:
            continue
        d1 = d if d <= FREE_STRIDE else FREE_STRIDE
        d2 = d // d1
        for half in range(halves):
            dst = slot * halves + half
            for r1 in range(d1):
                if d2 == 1:
                    l0 = r1 * GROUP_WIDTH + half * LANES
                    nat_ref[dst, pl.ds(r1, tm // d1, stride=d1), :] = src_ref[:, l0:l0 + LANES]
                    continue
                for r2 in range(d2):
                    l0 = (r1 + d1 * r2) * GROUP_WIDTH + half * LANES
                    stage_ref[stage, pl.ds(r2, tm // d, stride=d2), :] = src_ref[:, l0:l0 + LANES]
                nat_ref[dst, pl.ds(r1, tm // d1, stride=d1), :] = stage_ref[stage]
                stage += 1

    def natural(slot, rs):
        if DILATIONS[slot // 2] == 1:
            return att_refs[slot][rs, :]
        return jnp.concatenate([nat_ref[slot * halves + half, rs, :] for half in range(halves)], axis=1)

    c_b, c_c, c_u, c_v, c_g = conv_w, 2 * conv_w, 3 * conv_w, 3 * conv_w + sg_w, 3 * conv_w + 2 * sg_w
    ti = lax.broadcasted_iota(jnp.int32, (SG_CHUNK, SG_CHUNK), 0)
    si = lax.broadcasted_iota(jnp.int32, (SG_CHUNK, SG_CHUNK), 1)
    gch = sg_w // SG_GROUPS
    w_sg = [jnp.where(ti >= si, sgw_ref[g], 0.0).astype(BF16) for g in range(SG_GROUPS)]
    cw = cw_ref[...]
    carry = carry_ref[...]

    for s in range(tm // MIX_SUB_ROWS):
        r0 = s * MIX_SUB_ROWS
        rs = slice(r0, r0 + MIX_SUB_ROWS)
        h = h_ref[rs, :]
        a = _rms_norm(h, g_ref[...]).astype(BF16)

        def project(c0, c1):
            return jnp.dot(a, wr_ref[:, w_skip + c0:w_skip + c1], preferred_element_type=F32)

        z_conv = project(0, c_u)
        z_sg = project(c_u, c_g)

        zx, zb, zc = z_conv[:, :c_b], z_conv[:, c_b:c_c], z_conv[:, c_c:c_u]
        u = zc * zx
        ext = jnp.concatenate([carry, u], axis=0)
        u1 = pltpu.roll(ext, 1, 0)[SUBLANES:]
        u2 = pltpu.roll(ext, 2, 0)[SUBLANES:]
        carry = u[MIX_SUB_ROWS - SUBLANES:]
        yb = (zb * (cw[0:1] * u2 + cw[1:2] * u1 + cw[2:3] * u)).astype(BF16)

        z_g0 = project(c_g, c_g + d_model)

        l0, l1, l2 = natural(1, rs), natural(3, rs), natural(5, rs)
        mx = jnp.maximum(jnp.maximum(l0, l1), l2)
        e0, e1, e2 = jnp.exp2(l0 - mx), jnp.exp2(l1 - mx), jnp.exp2(l2 - mx)
        inv = 1.0 / (e0 + e1 + e2)
        ya = jnp.concatenate([natural(0, rs) * (e0 * inv), natural(2, rs) * (e1 * inv),
                              natural(4, rs) * (e2 * inv)], axis=1).astype(BF16)
        pa = jnp.dot(ya, wa_ref[...], preferred_element_type=F32)

        us = _gelu(z_sg[:, :sg_w])
        vs = _gelu(z_sg[:, sg_w:])
        mu = jnp.mean(vs, axis=-1, keepdims=True)
        xc = vs - mu
        var = jnp.mean(xc * xc, axis=-1, keepdims=True)
        vln = (xc * lax.rsqrt(var + LN_EPS) * lng_ref[...] + lnb_ref[...]).astype(BF16)

        z_g1 = project(c_g + d_model, c_g + 2 * d_model)
        pb = jnp.dot(yb, wb_ref[...], preferred_element_type=F32)
        m = _sigmoid(z_g0) * pa + _sigmoid(z_g1) * pb

        for g in range(SG_GROUPS):
            bias = sgb_ref[g]
            cs = slice(g * gch, (g + 1) * gch)
            for n in range(MIX_SUB_ROWS // SG_CHUNK):
                ns = slice(n * SG_CHUNK, (n + 1) * SG_CHUNK)
                sv = jnp.dot(w_sg[g], vln[ns, cs], preferred_element_type=F32) + bias
                yc_ref[r0 + n * SG_CHUNK:r0 + (n + 1) * SG_CHUNK, cs] = (us[ns, cs] * sv).astype(BF16)

        z_g2 = project(c_g + 2 * d_model, c_g + 3 * d_model)
        pc = jnp.dot(yc_ref[rs, :], wc_ref[...], preferred_element_type=F32)
        m = m + _sigmoid(z_g2) * pc
        out_ref[rs, :] = h + jnp.dot(m.astype(BF16), wo_ref[...], preferred_element_type=F32)

    carry_ref[...] = carry


def _mixer_merge(layer, h2d, g, att, w_in, conv_w, ln_g, ln_b, sg_w, sg_b, w_a, w_b, w_c, w_o, seq):
    n_tok, d_model = h2d.shape
    tm = MIX_ROWS
    conv_width = conv_w.shape[-1]
    sg_width = ln_g.shape[-1]
    row_tile = lambda width: pl.BlockSpec((tm, width), lambda i: (i, 0))
    (o0, l0), (o1, l1), (o2, l2) = att
    att_tiles = [pl.BlockSpec((tm // d, d * GROUP_WIDTH), lambda i: (i, 0)) for d in DILATIONS]
    n_stage = 2 * (GROUP_WIDTH // LANES) * FREE_STRIDE * sum(d > FREE_STRIDE for d in DILATIONS)
    kern = functools.partial(_mix_kernel, tiles_per_seq=seq // tm, conv_w=conv_width, sg_w=sg_width)
    return pl.pallas_call(
        kern,
        grid=(n_tok // tm,),
        in_specs=[row_tile(d_model), _resident(g, layer)]
        + att_tiles * 2
        + [_resident(w_in, 0)]
        + [_resident(x, layer) for x in (conv_w, ln_g, ln_b, sg_w, sg_b, w_a, w_b, w_c, w_o)],
        out_specs=row_tile(d_model),
        out_shape=jax.ShapeDtypeStruct((n_tok, d_model), F32),
        scratch_shapes=[pltpu.VMEM((SUBLANES, conv_width), F32),
                        pltpu.VMEM((tm, sg_width), BF16),
                        pltpu.VMEM((2 * len(DILATIONS) * (GROUP_WIDTH // LANES), tm, LANES), F32),
                        pltpu.VMEM((max(n_stage, 1), tm // FREE_STRIDE, LANES), F32)],
        compiler_params=pltpu.CompilerParams(
            dimension_semantics=("arbitrary",), vmem_limit_bytes=VMEM_LIMIT_BYTES),
        name="mixer_merge",
    )(h2d, g, o0, o1, o2, l0, l1, l2, w_in, conv_w, ln_g, ln_b, sg_w, sg_b, w_a, w_b, w_c, w_o)


def _mlp_kernel(h_ref, gm_ref, wu_ref, wd_ref, p_ref, gp_ref, wg_ref, wp_ref, gf_ref, *refs,
                final_norm):
    n_cast = len(refs) // 2
    out_ref = refs[n_cast]
    for src_ref, dst_ref in zip(refs[:n_cast], refs[n_cast + 1:]):
        dst_ref[...] = src_ref[...].astype(BF16)
    d_ff = wu_ref.shape[1]

    for s in range(h_ref.shape[0] // MLP_SUB_ROWS):
        rs = slice(s * MLP_SUB_ROWS, (s + 1) * MLP_SUB_ROWS)
        h = h_ref[rs, :]
        c = _rms_norm(h, gm_ref[...]).astype(BF16)
        acc = h
        for f0 in range(0, d_ff, MLP_FF_TILE):
            t = jnp.dot(c, wu_ref[:, f0:f0 + MLP_FF_TILE], preferred_element_type=F32)
            t = jnp.square(jnp.maximum(t, 0.0)).astype(BF16)
            acc = acc + jnp.dot(t, wd_ref[f0:f0 + MLP_FF_TILE, :], preferred_element_type=F32)
        e = _rms_norm(acc, gp_ref[...]).astype(BF16)
        gate = _sigmoid(jnp.dot(e, wg_ref[...], preferred_element_type=F32))
        emb = jnp.dot(p_ref[rs, :].astype(BF16), wp_ref[...], preferred_element_type=F32)
        out = acc + gate * emb
        out_ref[rs, :] = _rms_norm(out, gf_ref[...]) if final_norm else out


def _mlp_ple(layer, h2d, g_mlp, w_up, w_down, p3d, g_ple, w_pg, w_pe, g_final, next_weights):
    n_tok, d_model = h2d.shape
    tm = MLP_ROWS
    steps = n_tok // tm
    final_norm = not next_weights
    chunk = lambda w: (None, w.shape[1] // steps, w.shape[2])
    assert all(w.shape[1] % (steps * BF16_SUBLANES) == 0 for w in next_weights)
    row_tile = pl.BlockSpec((tm, d_model), lambda i: (i, 0))
    outs = pl.pallas_call(
        functools.partial(_mlp_kernel, final_norm=final_norm),
        grid=(steps,),
        in_specs=[
            row_tile,
            _resident(g_mlp, layer),
            _resident(w_up, 0),
            _resident(w_down, 0),
            pl.BlockSpec((None, tm, p3d.shape[-1]), lambda i: (layer, i, 0)),
            _resident(g_ple, layer),
            _resident(w_pg, layer),
            _resident(w_pe, layer),
            _resident(g_final, 0),
        ] + [pl.BlockSpec(chunk(w), lambda i: (layer + 1, i, 0)) for w in next_weights],
        out_specs=[row_tile] + [pl.BlockSpec(chunk(w), lambda i: (0, i, 0)) for w in next_weights],
        out_shape=[jax.ShapeDtypeStruct((n_tok, d_model), F32)]
        + [jax.ShapeDtypeStruct((1,) + w.shape[1:], BF16) for w in next_weights],
        compiler_params=pltpu.CompilerParams(
            dimension_semantics=("arbitrary",), vmem_limit_bytes=VMEM_LIMIT_BYTES),
        name="mlp_ple",
    )(h2d, g_mlp, w_up, w_down, p3d, g_ple, w_pg, w_pe, g_final, *next_weights)
    return outs[0], outs[1:]


def _cast_kernel(*refs):
    n = len(refs) // 2
    for src_ref, dst_ref in zip(refs[:n], refs[n:]):
        dst_ref[...] = src_ref[...].astype(BF16)


def _cast_weights(first_layer_of, whole):
    steps = CAST_STEPS
    in_specs, out_specs, out_shape, operands = [], [], [], []
    for w in first_layer_of:
        block = (None, w.shape[1] // steps, w.shape[2])
        in_specs.append(pl.BlockSpec(block, lambda i: (0, i, 0)))
        out_specs.append(pl.BlockSpec(block, lambda i: (0, i, 0)))
        out_shape.append(jax.ShapeDtypeStruct((1,) + w.shape[1:], BF16))
        operands.append(w)
    for w in whole:
        flat = w.reshape(-1, w.shape[-1])
        block = (flat.shape[0] // steps, flat.shape[1])
        in_specs.append(pl.BlockSpec(block, lambda i: (i, 0)))
        out_specs.append(pl.BlockSpec(block, lambda i: (i, 0)))
        out_shape.append(jax.ShapeDtypeStruct(flat.shape, BF16))
        operands.append(flat)
    assert all(spec.block_shape[-2] % BF16_SUBLANES == 0 for spec in out_specs)
    outs = pl.pallas_call(
        _cast_kernel,
        grid=(steps,),
        in_specs=in_specs,
        out_specs=out_specs,
        out_shape=out_shape,
        compiler_params=pltpu.CompilerParams(
            dimension_semantics=("arbitrary",), vmem_limit_bytes=VMEM_LIMIT_BYTES),
        name="cast_weights",
    )(*operands)
    n = len(first_layer_of)
    return outs[:n], [o.reshape(w.shape) for o, w in zip(outs[n:], whole)]


def _rotary_tables(positions):
    inv_freq = ROPE_THETA ** (-(jnp.arange(0, 2 * ROT_HALF, 2, dtype=F32) / (2 * ROT_HALF)))
    n_tok = positions.size
    pack = LANES // (2 * ROT_HALF)
    ang = (positions.astype(F32).reshape(n_tok // pack, pack, 1) * inv_freq).reshape(n_tok // pack, -1)
    cos8 = jnp.cos(ang).reshape(n_tok, ROT_HALF)
    sin8 = jnp.sin(ang).reshape(n_tok, ROT_HALF)
    in_head = jnp.arange(LANES) % HEAD_DIM
    rotated = in_head < 2 * ROT_HALF
    sign = jnp.where(in_head < ROT_HALF, -1.0, 1.0).astype(F32)
    spread = lambda t: jnp.tile(t, (1, LANES // ROT_HALF))
    return (jnp.where(rotated, spread(cos8), 1.0), jnp.where(rotated, sign * spread(sin8), 0.0))


def kernel(x, p, positions, norm_mix_g, w_in, conv_w, sg_ln_g, sg_ln_b, sg_w, sg_b,
           w_branch_a, w_branch_b, w_branch_c, w_out, norm_mlp_g, w_up, w_down,
           norm_ple_g, w_ple_gate, w_ple_proj, norm_final_g):
    batch, seq, d_model = x.shape
    depth = w_in.shape[0]
    n_tok = batch * seq
    qkv_width = 3 * len(DILATIONS) * GROUP_WIDTH
    assert seq % (max(DILATIONS) * ATTN_BLOCK) == 0 and seq % ATTN_ROWS_MAX == 0
    assert n_tok % MLP_ROWS == 0 and seq % MIX_ROWS == 0

    cos_t, sin_t = _rotary_tables(positions)
    rows = lambda v: v.reshape(v.shape[0], 1, v.shape[-1])
    sg_bias = jnp.broadcast_to(sg_b[:, :, :, None], sg_w.shape)
    big_f32 = (w_in, w_up, w_down)
    (w_in_l, w_up_l, w_down_l), (w_a, w_b, w_c, w_o, w_pg, w_pe) = _cast_weights(
        big_f32, (w_branch_a, w_branch_b, w_branch_c, w_out, w_ple_gate, w_ple_proj))
    g_mix, g_mlp, g_ple = rows(norm_mix_g), rows(norm_mlp_g), rows(norm_ple_g)
    ln_g, ln_b = rows(sg_ln_g), rows(sg_ln_b)
    p3d = p.reshape(depth, n_tok, p.shape[-1])

    g_final = norm_final_g.reshape(1, 1, -1)

    h = x.reshape(n_tok, d_model)
    for i in range(depth):
        qkv = _qkv_projection(i, h, g_mix, w_in_l, cos_t, sin_t)
        att = _attention(qkv, batch, seq)
        h = _mixer_merge(i, h, g_mix, att, w_in_l, conv_w, ln_g, ln_b, sg_w, sg_bias,
                         w_a, w_b, w_c, w_o, seq)
        h, nxt = _mlp_ple(i, h, g_mlp, w_up_l, w_down_l, p3d, g_ple, w_pg, w_pe, g_final,
                          big_f32 if i + 1 < depth else ())
        if nxt:
            w_in_l, w_up_l, w_down_l = nxt
    return h.reshape(batch, seq, d_model)
```

```python
import functools
import math

import jax
import jax.numpy as jnp
from jax import lax
from jax.experimental import pallas as pl
from jax.experimental.pallas import tpu as pltpu

F32 = jnp.float32
BF16 = jnp.bfloat16

HEAD_DIM = 64
HEADS_PER_GROUP = 4
GROUP_WIDTH = HEADS_PER_GROUP * HEAD_DIM
DILATIONS = (1, 4, 16)
ATTN_BLOCK = 128
ROT_HALF = 8
ROPE_THETA = 500000.0
CONV_K = 3
SG_CHUNK = 128
SG_GROUPS = 4
RMS_EPS = 1e-6
LN_EPS = 1e-5
QUERY_SCALE = math.log2(math.e) * HEAD_DIM ** -0.5

LANES = 128
SUBLANES = 8
BF16_SUBLANES = 16
FREE_STRIDE = 4
VMEM_LIMIT_BYTES = 56 * 1024 * 1024

QKV_ROWS = 1024
QKV_SUB_ROWS = 256
ATTN_ROWS_MAX = 2048
MIX_ROWS = 512
MIX_SUB_ROWS = 256
MLP_ROWS = 512
MLP_SUB_ROWS = 512
MLP_FF_TILE = 1024
CAST_STEPS = 32


def _rms_norm(h, g):
    ms = jnp.mean(h * h, axis=-1, keepdims=True)
    return h * lax.rsqrt(ms + RMS_EPS) * g


def _sigmoid(x):
    return 1.0 / (1.0 + jnp.exp(-x))


def _gelu(x):
    return 0.5 * x * (1.0 + lax.erf(x * (1.0 / math.sqrt(2.0))))


def _resident(stacked, layer):
    shape = stacked.shape[1:]
    index = (layer,) + (0,) * len(shape)
    return pl.BlockSpec((None,) + shape, lambda *_: index, pipeline_mode=pl.Buffered(1))


def _qkv_kernel(h_ref, g_ref, w_ref, cos_ref, sin_ref, *refs):
    outs, slab_ref, slab2_ref = refs[:-2], refs[-2], refs[-1]
    n_groups = len(outs)
    sub = QKV_SUB_ROWS
    lane = lax.broadcasted_iota(jnp.int32, (sub, LANES), 1)
    low_half = (lane % HEAD_DIM) < ROT_HALF
    slab = slab2 = 0
    for s in range(h_ref.shape[0] // sub):
        rs = slice(s * sub, (s + 1) * sub)
        a = _rms_norm(h_ref[rs, :], g_ref[...]).astype(BF16)
        cos_k, sin_k = cos_ref[rs, :], sin_ref[rs, :]
        cos_sin = ((cos_k * QUERY_SCALE, sin_k * QUERY_SCALE), (cos_k, sin_k))
        for which in range(3):
            for g, d in enumerate(DILATIONS):
                c0 = (which * n_groups + g) * GROUP_WIDTH
                z = jnp.dot(a, w_ref[:, c0:c0 + GROUP_WIDTH], preferred_element_type=F32)
                out_rows = slice(s * sub // d, (s + 1) * sub // d)
                for half in range(GROUP_WIDTH // LANES):
                    t = z[:, half * LANES:(half + 1) * LANES]
                    if which < 2:
                        partner = jnp.where(low_half,
                                            pltpu.roll(t, LANES - ROT_HALF, 1),
                                            pltpu.roll(t, ROT_HALF, 1))
                        cos_t, sin_t = cos_sin[which]
                        t = t * cos_t + partner * sin_t
                    if d == 1:
                        outs[g][which, out_rows, half * LANES:(half + 1) * LANES] = t.astype(BF16)
                    else:
                        slab_ref[slab] = t
                        d1 = d if d <= FREE_STRIDE else FREE_STRIDE
                        d2 = d // d1
                        for r1 in range(d1):
                            part = slab_ref[slab, pl.ds(r1, sub // d1, stride=d1), :]
                            if d2 == 1:
                                l0 = r1 * GROUP_WIDTH + half * LANES
                                outs[g][which, out_rows, l0:l0 + LANES] = part.astype(BF16)
                                continue
                            slab2_ref[slab2] = part
                            for r2 in range(d2):
                                l0 = (r1 + d1 * r2) * GROUP_WIDTH + half * LANES
                                rows = slab2_ref[slab2, pl.ds(r2, sub // d, stride=d2), :]
                                outs[g][which, out_rows, l0:l0 + LANES] = rows.astype(BF16)
                            slab2 += 1
                        slab += 1


def _qkv_projection(layer, h2d, g, w_in, cos_t, sin_t):
    n_tok, d_model = h2d.shape
    tm = QKV_ROWS
    qkv_width = 3 * len(DILATIONS) * GROUP_WIDTH
    per_group = 3 * (GROUP_WIDTH // LANES) * (tm // QKV_SUB_ROWS)
    n_slabs = per_group * sum(d > 1 for d in DILATIONS)
    n_slabs2 = per_group * FREE_STRIDE * sum(d > FREE_STRIDE for d in DILATIONS)
    row_tile = pl.BlockSpec((tm, d_model), lambda i: (i, 0))
    out_specs = [pl.BlockSpec((3, tm // d, d * GROUP_WIDTH), lambda i: (0, i, 0)) for d in DILATIONS]
    out_shape = [jax.ShapeDtypeStruct((3, n_tok // d, d * GROUP_WIDTH), BF16) for d in DILATIONS]
    return pl.pallas_call(
        _qkv_kernel,
        grid=(n_tok // tm,),
        in_specs=[row_tile, _resident(g, layer)] + [
            pl.BlockSpec((None, d_model, qkv_width), lambda i: (0, 0, 0),
                         pipeline_mode=pl.Buffered(1)),
            pl.BlockSpec((tm, LANES), lambda i: (i, 0)),
            pl.BlockSpec((tm, LANES), lambda i: (i, 0)),
        ],
        out_specs=out_specs,
        out_shape=out_shape,
        scratch_shapes=[pltpu.VMEM((n_slabs, QKV_SUB_ROWS, LANES), F32),
                        pltpu.VMEM((max(n_slabs2, 1), QKV_SUB_ROWS // FREE_STRIDE, LANES), F32)],
        compiler_params=pltpu.CompilerParams(
            dimension_semantics=("arbitrary",), vmem_limit_bytes=VMEM_LIMIT_BYTES),
        name="qkv_proj",
    )(h2d, g, w_in, cos_t, sin_t)


def _attend(q_ref, k_ref, kp_ref, v_ref, vp_ref, ol_ref, first_step):
    def window(ref, prev_ref, r0, c0):
        if r0 > 0:
            return ref[r0 - ATTN_BLOCK:r0 + ATTN_BLOCK, c0:c0 + LANES]
        cur = ref[0:ATTN_BLOCK, c0:c0 + LANES]
        before = jnp.zeros_like(cur) if prev_ref is None else prev_ref[:, c0:c0 + LANES]
        return jnp.concatenate([before, cur], axis=0)

    two = 2 * ATTN_BLOCK
    qi = lax.broadcasted_iota(jnp.int32, (two, two), 0) % ATTN_BLOCK
    kj = lax.broadcasted_iota(jnp.int32, (two, two), 1)
    dist = qi + ATTN_BLOCK - kj
    band = (dist >= 0) & (dist <= ATTN_BLOCK)
    band_first = band & ((kj >= ATTN_BLOCK) | jnp.logical_not(first_step))
    mask_rest = jnp.where(band, 0.0, -jnp.inf)
    mask_first = jnp.where(band_first, 0.0, -jnp.inf)
    lane = lax.broadcasted_iota(jnp.int32, (ATTN_BLOCK, LANES), 1)
    head0 = lane < HEAD_DIM
    keep0 = jnp.where(head0, 1.0, 0.0).astype(BF16)
    keep1 = jnp.where(head0, 0.0, 1.0).astype(BF16)
    ones = jnp.ones((two, LANES), BF16)

    for i in range(q_ref.shape[0] // ATTN_BLOCK):
        mask = mask_first if i == 0 else mask_rest
        r0 = i * ATTN_BLOCK
        for p in range(q_ref.shape[1] // LANES):
            c0 = p * LANES
            q2 = q_ref[r0:r0 + ATTN_BLOCK, c0:c0 + LANES]
            qs = jnp.concatenate([q2 * keep0, q2 * keep1], axis=0)
            kk = window(k_ref, kp_ref, r0, c0)
            s = lax.dot_general(qs, kk, (((1,), (1,)), ((), ())),
                                preferred_element_type=F32)
            s = s + mask
            m = jnp.max(s, axis=-1, keepdims=True)
            e = jnp.exp2((s - m).astype(BF16))
            vv = jnp.concatenate([window(v_ref, vp_ref, r0, c0), ones], axis=1)
            pv = jnp.dot(e, vv, preferred_element_type=F32)
            pick = lambda x: jnp.where(head0, x[:ATTN_BLOCK], x[ATTN_BLOCK:])
            den = pick(pv[:, LANES:])
            ol_ref[0, r0:r0 + ATTN_BLOCK, c0:c0 + LANES] = pick(pv[:, :LANES]) * (1.0 / den)
            ol_ref[1, r0:r0 + ATTN_BLOCK, c0:c0 + LANES] = pick(m) + jnp.log2(den)


def _attn_kernel(*refs, plans):
    refs = iter(refs)
    inputs = []
    for has_prev, _ in plans:
        q_ref, k_ref = next(refs), next(refs)
        kp_ref = next(refs) if has_prev else None
        v_ref = next(refs)
        vp_ref = next(refs) if has_prev else None
        inputs.append((q_ref, k_ref, kp_ref, v_ref, vp_ref))
    for (has_prev, row_steps), group_inputs in zip(plans, inputs):
        first_step = (pl.program_id(1) % row_steps == 0) if has_prev else True
        _attend(*group_inputs, next(refs), first_step)


def _attention(qkv, batch, seq):
    in_specs, operands, out_specs, out_shapes, plans, steps = [], [], [], [], [], set()
    for qkv_g, dilation in zip(qkv, DILATIONS):
        m_len = seq // dilation
        mb = min(m_len, ATTN_ROWS_MAX)
        nblk = mb // ATTN_BLOCK
        n_res = min(dilation, ATTN_ROWS_MAX // mb)
        width = n_res * GROUP_WIDTH
        row_steps = m_len // mb
        has_prev = row_steps > 1
        steps.add(row_steps * (dilation // n_res))
        view = qkv_g.reshape(3, batch, m_len, dilation * GROUP_WIDTH)

        def rows(which, row_steps=row_steps, mb=mb, width=width):
            return pl.BlockSpec((None, None, mb, width),
                                lambda b, t: (which, b, t % row_steps, t // row_steps))

        def prev(which, row_steps=row_steps, nblk=nblk, width=width):
            return pl.BlockSpec(
                (None, None, ATTN_BLOCK, width),
                lambda b, t: (which, b, jnp.maximum((t % row_steps) * nblk - 1, 0), t // row_steps))

        if has_prev:
            specs = [rows(0), rows(1), prev(1), rows(2), prev(2)]
        else:
            specs = [rows(0), rows(1), rows(2)]
        in_specs += specs
        operands += [view] * len(specs)
        out_specs.append(pl.BlockSpec(
            (2, None, mb, width),
            lambda b, t, row_steps=row_steps: (0, b, t % row_steps, t // row_steps)))
        out_shapes.append(jax.ShapeDtypeStruct((2, batch, m_len, dilation * GROUP_WIDTH), F32))
        plans.append((has_prev, row_steps))
    (n_steps,) = steps
    outs = pl.pallas_call(
        functools.partial(_attn_kernel, plans=tuple(plans)),
        grid=(batch, n_steps),
        in_specs=in_specs,
        out_specs=out_specs,
        out_shape=out_shapes,
        compiler_params=pltpu.CompilerParams(
            dimension_semantics=("arbitrary", "arbitrary"), vmem_limit_bytes=VMEM_LIMIT_BYTES),
        name="dilated_attn",
    )(*operands)
    return [x.reshape(2, -1, x.shape[-1]) for x in outs]


def _mix_kernel(h_ref, g_ref, ol0_ref, ol1_ref, ol2_ref,
                wr_ref, cw_ref, lng_ref, lnb_ref, sgw_ref, sgb_ref,
                wa_ref, wb_ref, wc_ref, wo_ref, out_ref, carry_ref, yc_ref, nat_ref, stage_ref,
                *, tiles_per_seq, conv_w, sg_w):
    tm, d_model = h_ref.shape
    halves = GROUP_WIDTH // LANES
    w_skip = 3 * len(DILATIONS) * GROUP_WIDTH

    @pl.when(pl.program_id(0) % tiles_per_seq == 0)
    def _():
        carry_ref[...] = jnp.zeros_like(carry_ref)

    att_refs = tuple(ol_ref.at[k] for ol_ref in (ol0_ref, ol1_ref, ol2_ref) for k in range(2))
    stage = 0
    for slot, src_ref in enumerate(att_refs):
        d = DILATIONS[slot // 2]
        if d == 1:
            continue
        d1 = d if d <= FREE_STRIDE else FREE_STRIDE
        d2 = d // d1
        for half in range(halves):
            dst = slot * halves + half
            for r1 in range(d1):
                if d2 == 1:
                    l0 = r1 * GROUP_WIDTH + half * LANES
                    nat_ref[dst, pl.ds(r1, tm // d1, stride=d1), :] = src_ref[:, l0:l0 + LANES]
                    continue
                for r2 in range(d2):
                    l0 = (r1 + d1 * r2) * GROUP_WIDTH + half * LANES
                    stage_ref[stage, pl.ds(r2, tm // d, stride=d2), :] = src_ref[:, l0:l0 + LANES]
                nat_ref[dst, pl.ds(r1, tm // d1, stride=d1), :] = stage_ref[stage]
                stage += 1

    def natural(slot, rs):
        if DILATIONS[slot // 2] == 1:
            return att_refs[slot][rs, :]
        return jnp.concatenate([nat_ref[slot * halves + half, rs, :] for half in range(halves)], axis=1)

    c_b, c_c, c_u, c_v, c_g = conv_w, 2 * conv_w, 3 * conv_w, 3 * conv_w + sg_w, 3 * conv_w + 2 * sg_w
    ti = lax.broadcasted_iota(jnp.int32, (SG_CHUNK, SG_CHUNK), 0)
    si = lax.broadcasted_iota(jnp.int32, (SG_CHUNK, SG_CHUNK), 1)
    gch = sg_w // SG_GROUPS
    w_sg = [jnp.where(ti >= si, sgw_ref[g], 0.0).astype(BF16) for g in range(SG_GROUPS)]
    cw = cw_ref[...]
    carry = carry_ref[...]

    for s in range(tm // MIX_SUB_ROWS):
        r0 = s * MIX_SUB_ROWS
        rs = slice(r0, r0 + MIX_SUB_ROWS)
        h = h_ref[rs, :]
        a = _rms_norm(h, g_ref[...]).astype(BF16)

        def project(c0, c1):
            return jnp.dot(a, wr_ref[:, w_skip + c0:w_skip + c1], preferred_element_type=F32)

        z_conv = project(0, c_u)
        z_sg = project(c_u, c_g)

        zx, zb, zc = z_conv[:, :c_b], z_conv[:, c_b:c_c], z_conv[:, c_c:c_u]
        u = zc * zx
        ext = jnp.concatenate([carry, u], axis=0)
        u1 = pltpu.roll(ext, 1, 0)[SUBLANES:]
        u2 = pltpu.roll(ext, 2, 0)[SUBLANES:]
        carry = u[MIX_SUB_ROWS - SUBLANES:]
        yb = (zb * (cw[0:1] * u2 + cw[1:2] * u1 + cw[2:3] * u)).astype(BF16)

        z_g0 = project(c_g, c_g + d_model)

        l0, l1, l2 = natural(1, rs), natural(3, rs), natural(5, rs)
        mx = jnp.maximum(jnp.maximum(l0, l1), l2)
        e0, e1, e2 = jnp.exp2(l0 - mx), jnp.exp2(l1 - mx), jnp.exp2(l2 - mx)
        inv = 1.0 / (e0 + e1 + e2)
        ya = jnp.concatenate([natural(0, rs) * (e0 * inv), natural(2, rs) * (e1 * inv),
                              natural(4, rs) * (e2 * inv)], axis=1).astype(BF16)
        pa = jnp.dot(ya, wa_ref[...], preferred_element_type=F32)

        us = _gelu(z_sg[:, :sg_w])
        vs = _gelu(z_sg[:, sg_w:])
        mu = jnp.mean(vs, axis=-1, keepdims=True)
        xc = vs - mu
        var = jnp.mean(xc * xc, axis=-1, keepdims=True)
        vln = (xc * lax.rsqrt(var + LN_EPS) * lng_ref[...] + lnb_ref[...]).astype(BF16)

        z_g1 = project(c_g + d_model, c_g + 2 * d_model)
        pb = jnp.dot(yb, wb_ref[...], preferred_element_type=F32)
        m = _sigmoid(z_g0) * pa + _sigmoid(z_g1) * pb

        for g in range(SG_GROUPS):
            bias = sgb_ref[g]
            cs = slice(g * gch, (g + 1) * gch)
            for n in range(MIX_SUB_ROWS // SG_CHUNK):
                ns = slice(n * SG_CHUNK, (n + 1) * SG_CHUNK)
                sv = jnp.dot(w_sg[g], vln[ns, cs], preferred_element_type=F32) + bias
                yc_ref[r0 + n * SG_CHUNK:r0 + (n + 1) * SG_CHUNK, cs] = (us[ns, cs] * sv).astype(BF16)

        z_g2 = project(c_g + 2 * d_model, c_g + 3 * d_model)
        pc = jnp.dot(yc_ref[rs, :], wc_ref[...], preferred_element_type=F32)
        m = m + _sigmoid(z_g2) * pc
        out_ref[rs, :] = h + jnp.dot(m.astype(BF16), wo_ref[...], preferred_element_type=F32)

    carry_ref[...] = carry


def _mixer_merge(layer, h2d, g, att, w_in, conv_w, ln_g, ln_b, sg_w, sg_b, w_a, w_b, w_c, w_o, seq):
    n_tok, d_model = h2d.shape
    tm = MIX_ROWS
    conv_width = conv_w.shape[-1]
    sg_width = ln_g.shape[-1]
    row_tile = lambda width: pl.BlockSpec((tm, width), lambda i: (i, 0))
    att_tiles = [pl.BlockSpec((2, tm // d, d * GROUP_WIDTH), lambda i: (0, i, 0)) for d in DILATIONS]
    n_stage = 2 * (GROUP_WIDTH // LANES) * FREE_STRIDE * sum(d > FREE_STRIDE for d in DILATIONS)
    kern = functools.partial(_mix_kernel, tiles_per_seq=seq // tm, conv_w=conv_width, sg_w=sg_width)
    return pl.pallas_call(
        kern,
        grid=(n_tok // tm,),
        in_specs=[row_tile(d_model), _resident(g, layer)]
        + att_tiles
        + [_resident(w_in, 0)]
        + [_resident(x, layer) for x in (conv_w, ln_g, ln_b, sg_w, sg_b, w_a, w_b, w_c, w_o)],
        out_specs=row_tile(d_model),
        out_shape=jax.ShapeDtypeStruct((n_tok, d_model), F32),
        scratch_shapes=[pltpu.VMEM((SUBLANES, conv_width), F32),
                        pltpu.VMEM((tm, sg_width), BF16),
                        pltpu.VMEM((2 * len(DILATIONS) * (GROUP_WIDTH // LANES), tm, LANES), F32),
                        pltpu.VMEM((max(n_stage, 1), tm // FREE_STRIDE, LANES), F32)],
        compiler_params=pltpu.CompilerParams(
            dimension_semantics=("arbitrary",), vmem_limit_bytes=VMEM_LIMIT_BYTES),
        name="mixer_merge",
    )(h2d, g, *att, w_in, conv_w, ln_g, ln_b, sg_w, sg_b, w_a, w_b, w_c, w_o)


def _mlp_kernel(h_ref, gm_ref, wu_ref, wd_ref, p_ref, gp_ref, wg_ref, wp_ref, gf_ref, *refs,
                final_norm):
    n_cast = len(refs) // 2
    out_ref = refs[n_cast]
    for src_ref, dst_ref in zip(refs[:n_cast], refs[n_cast + 1:]):
        dst_ref[...] = src_ref[...].astype(BF16)
    d_ff = wu_ref.shape[1]

    for s in range(h_ref.shape[0] // MLP_SUB_ROWS):
        rs = slice(s * MLP_SUB_ROWS, (s + 1) * MLP_SUB_ROWS)
        h = h_ref[rs, :]
        c = _rms_norm(h, gm_ref[...]).astype(BF16)
        acc = h
        for f0 in range(0, d_ff, MLP_FF_TILE):
            t = jnp.dot(c, wu_ref[:, f0:f0 + MLP_FF_TILE], preferred_element_type=F32)
            t = jnp.square(jnp.maximum(t, 0.0)).astype(BF16)
            acc = acc + jnp.dot(t, wd_ref[f0:f0 + MLP_FF_TILE, :], preferred_element_type=F32)
        e = _rms_norm(acc, gp_ref[...]).astype(BF16)
        gate = _sigmoid(jnp.dot(e, wg_ref[...], preferred_element_type=F32))
        emb = jnp.dot(p_ref[rs, :].astype(BF16), wp_ref[...], preferred_element_type=F32)
        out = acc + gate * emb
        out_ref[rs, :] = _rms_norm(out, gf_ref[...]) if final_norm else out


def _mlp_ple(layer, h2d, g_mlp, w_up, w_down, p3d, g_ple, w_pg, w_pe, g_final, next_weights):
    n_tok, d_model = h2d.shape
    tm = MLP_ROWS
    steps = n_tok // tm
    final_norm = not next_weights
    chunk = lambda w: (None, w.shape[1] // steps, w.shape[2])
    assert all(w.shape[1] % (steps * BF16_SUBLANES) == 0 for w in next_weights)
    row_tile = pl.BlockSpec((tm, d_model), lambda i: (i, 0))
    outs = pl.pallas_call(
        functools.partial(_mlp_kernel, final_norm=final_norm),
        grid=(steps,),
        in_specs=[
            row_tile,
            _resident(g_mlp, layer),
            _resident(w_up, 0),
            _resident(w_down, 0),
            pl.BlockSpec((None, tm, p3d.shape[-1]), lambda i: (layer, i, 0)),
            _resident(g_ple, layer),
            _resident(w_pg, layer),
            _resident(w_pe, layer),
            _resident(g_final, 0),
        ] + [pl.BlockSpec(chunk(w), lambda i: (layer + 1, i, 0)) for w in next_weights],
        out_specs=[row_tile] + [pl.BlockSpec(chunk(w), lambda i: (0, i, 0)) for w in next_weights],
        out_shape=[jax.ShapeDtypeStruct((n_tok, d_model), F32)]
        + [jax.ShapeDtypeStruct((1,) + w.shape[1:], BF16) for w in next_weights],
        compiler_params=pltpu.CompilerParams(
            dimension_semantics=("arbitrary",), vmem_limit_bytes=VMEM_LIMIT_BYTES),
        name="mlp_ple",
    )(h2d, g_mlp, w_up, w_down, p3d, g_ple, w_pg, w_pe, g_final, *next_weights)
    return outs[0], outs[1:]


def _cast_kernel(*refs):
    n = len(refs) // 2
    for src_ref, dst_ref in zip(refs[:n], refs[n:]):
        dst_ref[...] = src_ref[...].astype(BF16)


def _cast_weights(first_layer_of, whole):
    steps = CAST_STEPS
    in_specs, out_specs, out_shape, operands = [], [], [], []
    for w in first_layer_of:
        block = (None, w.shape[1] // steps, w.shape[2])
        in_specs.append(pl.BlockSpec(block, lambda i: (0, i, 0)))
        out_specs.append(pl.BlockSpec(block, lambda i: (0, i, 0)))
        out_shape.append(jax.ShapeDtypeStruct((1,) + w.shape[1:], BF16))
        operands.append(w)
    for w in whole:
        flat = w.reshape(-1, w.shape[-1])
        block = (flat.shape[0] // steps, flat.shape[1])
        in_specs.append(pl.BlockSpec(block, lambda i: (i, 0)))
        out_specs.append(pl.BlockSpec(block, lambda i: (i, 0)))
        out_shape.append(jax.ShapeDtypeStruct(flat.shape, BF16))
        operands.append(flat)
    assert all(spec.block_shape[-2] % BF16_SUBLANES == 0 for spec in out_specs)
    outs = pl.pallas_call(
        _cast_kernel,
        grid=(steps,),
        in_specs=in_specs,
        out_specs=out_specs,
        out_shape=out_shape,
        compiler_params=pltpu.CompilerParams(
            dimension_semantics=("arbitrary",), vmem_limit_bytes=VMEM_LIMIT_BYTES),
        name="cast_weights",
    )(*operands)
    n = len(first_layer_of)
    return outs[:n], [o.reshape(w.shape) for o, w in zip(outs[n:], whole)]


def _rotary_tables(positions):
    inv_freq = ROPE_THETA ** (-(jnp.arange(0, 2 * ROT_HALF, 2, dtype=F32) / (2 * ROT_HALF)))
    n_tok = positions.size
    pack = LANES // (2 * ROT_HALF)
    ang = (positions.astype(F32).reshape(n_tok // pack, pack, 1) * inv_freq).reshape(n_tok // pack, -1)
    cos8 = jnp.cos(ang).reshape(n_tok, ROT_HALF)
    sin8 = jnp.sin(ang).reshape(n_tok, ROT_HALF)
    in_head = jnp.arange(LANES) % HEAD_DIM
    rotated = in_head < 2 * ROT_HALF
    sign = jnp.where(in_head < ROT_HALF, -1.0, 1.0).astype(F32)
    spread = lambda t: jnp.tile(t, (1, LANES // ROT_HALF))
    return (jnp.where(rotated, spread(cos8), 1.0), jnp.where(rotated, sign * spread(sin8), 0.0))


def kernel(x, p, positions, norm_mix_g, w_in, conv_w, sg_ln_g, sg_ln_b, sg_w, sg_b,
           w_branch_a, w_branch_b, w_branch_c, w_out, norm_mlp_g, w_up, w_down,
           norm_ple_g, w_ple_gate, w_ple_proj, norm_final_g):
    batch, seq, d_model = x.shape
    depth = w_in.shape[0]
    n_tok = batch * seq
    qkv_width = 3 * len(DILATIONS) * GROUP_WIDTH
    assert seq % (max(DILATIONS) * ATTN_BLOCK) == 0 and seq % ATTN_ROWS_MAX == 0
    assert n_tok % MLP_ROWS == 0 and seq % MIX_ROWS == 0

    cos_t, sin_t = _rotary_tables(positions)
    rows = lambda v: v.reshape(v.shape[0], 1, v.shape[-1])
    sg_bias = jnp.broadcast_to(sg_b[:, :, :, None], sg_w.shape)
    big_f32 = (w_in, w_up, w_down)
    (w_in_l, w_up_l, w_down_l), (w_a, w_b, w_c, w_o, w_pg, w_pe) = _cast_weights(
        big_f32, (w_branch_a, w_branch_b, w_branch_c, w_out, w_ple_gate, w_ple_proj))
    g_mix, g_mlp, g_ple = rows(norm_mix_g), rows(norm_mlp_g), rows(norm_ple_g)
    ln_g, ln_b = rows(sg_ln_g), rows(sg_ln_b)
    p3d = p.reshape(depth, n_tok, p.shape[-1])

    g_final = norm_final_g.reshape(1, 1, -1)

    h = x.reshape(n_tok, d_model)
    for i in range(depth):
        qkv = _qkv_projection(i, h, g_mix, w_in_l, cos_t, sin_t)
        att = _attention(qkv, batch, seq)
        h = _mixer_merge(i, h, g_mix, att, w_in_l, conv_w, ln_g, ln_b, sg_w, sg_bias,
                         w_a, w_b, w_c, w_o, seq)
        h, nxt = _mlp_ple(i, h, g_mlp, w_up_l, w_down_l, p3d, g_ple, w_pg, w_pe, g_final,
                          big_f32 if i + 1 < depth else ())
        if nxt:
            w_in_l, w_up_l, w_down_l = nxt
    return h.reshape(batch, seq, d_model)
```

```python
import functools
import math

import jax
import jax.numpy as jnp
from jax import lax
from jax.experimental import pallas as pl
from jax.experimental.pallas import tpu as pltpu

F32 = jnp.float32
BF16 = jnp.bfloat16

HEAD_DIM = 64
HEADS_PER_GROUP = 4
GROUP_WIDTH = HEADS_PER_GROUP * HEAD_DIM
DILATIONS = (1, 4, 16)
ATTN_BLOCK = 128
ROT_HALF = 8
ROPE_THETA = 500000.0
CONV_K = 3
SG_CHUNK = 128
SG_GROUPS = 4
RMS_EPS = 1e-6
LN_EPS = 1e-5
QUERY_SCALE = math.log2(math.e) * HEAD_DIM ** -0.5

LANES = 128
SUBLANES = 8
BF16_SUBLANES = 16
FREE_STRIDE = 4
VMEM_LIMIT_BYTES = 56 * 1024 * 1024

QKV_ROWS = 1024
QKV_SUB_ROWS = 256
ATTN_ROWS_MAX = 2048
MIX_ROWS = 512
MIX_SUB_ROWS = 256
MLP_ROWS = 512
MLP_SUB_ROWS = 512
MLP_FF_TILE = 1024


def _rms_norm(h, g):
    ms = jnp.mean(h * h, axis=-1, keepdims=True)
    return h * lax.rsqrt(ms + RMS_EPS) * g


def _sigmoid(x):
    return 1.0 / (1.0 + jnp.exp(-x))


def _gelu(x):
    return 0.5 * x * (1.0 + lax.erf(x * (1.0 / math.sqrt(2.0))))


def _resident(stacked, layer):
    shape = stacked.shape[1:]
    index = (layer,) + (0,) * len(shape)
    return pl.BlockSpec((None,) + shape, lambda *_: index, pipeline_mode=pl.Buffered(1))


def _qkv_kernel(h_ref, g_ref, w_ref, cos_ref, sin_ref, *refs, n_cast):
    n_groups = len(DILATIONS)
    outs = refs[n_cast:n_cast + n_groups]
    slab_ref, slab2_ref = refs[-2], refs[-1]
    for src_ref, dst_ref in zip(refs[:n_cast], refs[n_cast + n_groups:-2]):
        dst_ref[...] = src_ref[...].astype(BF16)
    sub = QKV_SUB_ROWS
    lane = lax.broadcasted_iota(jnp.int32, (sub, LANES), 1)
    low_half = (lane % HEAD_DIM) < ROT_HALF
    slab = slab2 = 0
    for s in range(h_ref.shape[0] // sub):
        rs = slice(s * sub, (s + 1) * sub)
        a = _rms_norm(h_ref[rs, :], g_ref[...]).astype(BF16)
        cos_k, sin_k = cos_ref[rs, :], sin_ref[rs, :]
        cos_sin = ((cos_k * QUERY_SCALE, sin_k * QUERY_SCALE), (cos_k, sin_k))
        for which in range(3):
            for g, d in enumerate(DILATIONS):
                c0 = (which * n_groups + g) * GROUP_WIDTH
                z = jnp.dot(a, w_ref[:, c0:c0 + GROUP_WIDTH], preferred_element_type=F32)
                out_rows = slice(s * sub // d, (s + 1) * sub // d)
                for half in range(GROUP_WIDTH // LANES):
                    t = z[:, half * LANES:(half + 1) * LANES]
                    if which < 2:
                        partner = jnp.where(low_half,
                                            pltpu.roll(t, LANES - ROT_HALF, 1),
                                            pltpu.roll(t, ROT_HALF, 1))
                        cos_t, sin_t = cos_sin[which]
                        t = t * cos_t + partner * sin_t
                    if d == 1:
                        outs[g][which, out_rows, half * LANES:(half + 1) * LANES] = t.astype(BF16)
                    else:
                        slab_ref[slab] = t
                        d1 = d if d <= FREE_STRIDE else FREE_STRIDE
                        d2 = d // d1
                        for r1 in range(d1):
                            part = slab_ref[slab, pl.ds(r1, sub // d1, stride=d1), :]
                            if d2 == 1:
                                l0 = r1 * GROUP_WIDTH + half * LANES
                                outs[g][which, out_rows, l0:l0 + LANES] = part.astype(BF16)
                                continue
                            slab2_ref[slab2] = part
                            for r2 in range(d2):
                                l0 = (r1 + d1 * r2) * GROUP_WIDTH + half * LANES
                                rows = slab2_ref[slab2, pl.ds(r2, sub // d, stride=d2), :]
                                outs[g][which, out_rows, l0:l0 + LANES] = rows.astype(BF16)
                            slab2 += 1
                        slab += 1


def _qkv_projection(layer, h2d, g, w_in, cos_t, sin_t, cast_plan=None):
    n_tok, d_model = h2d.shape
    tm = QKV_ROWS
    cast_operands, cast_in, cast_out, cast_shape, restore = cast_plan or ([], [], [], [], None)
    qkv_width = 3 * len(DILATIONS) * GROUP_WIDTH
    per_group = 3 * (GROUP_WIDTH // LANES) * (tm // QKV_SUB_ROWS)
    n_slabs = per_group * sum(d > 1 for d in DILATIONS)
    n_slabs2 = per_group * FREE_STRIDE * sum(d > FREE_STRIDE for d in DILATIONS)
    row_tile = pl.BlockSpec((tm, d_model), lambda i: (i, 0))
    out_specs = [pl.BlockSpec((3, tm // d, d * GROUP_WIDTH), lambda i: (0, i, 0)) for d in DILATIONS]
    out_shape = [jax.ShapeDtypeStruct((3, n_tok // d, d * GROUP_WIDTH), BF16) for d in DILATIONS]
    outs = pl.pallas_call(
        functools.partial(_qkv_kernel, n_cast=len(cast_operands)),
        grid=(n_tok // tm,),
        in_specs=[row_tile, _resident(g, layer)] + [
            pl.BlockSpec((None, d_model, qkv_width), lambda i: (0, 0, 0),
                         pipeline_mode=pl.Buffered(1)),
            pl.BlockSpec((tm, LANES), lambda i: (i, 0)),
            pl.BlockSpec((tm, LANES), lambda i: (i, 0)),
        ] + cast_in,
        out_specs=out_specs + cast_out,
        out_shape=out_shape + cast_shape,
        scratch_shapes=[pltpu.VMEM((n_slabs, QKV_SUB_ROWS, LANES), F32),
                        pltpu.VMEM((max(n_slabs2, 1), QKV_SUB_ROWS // FREE_STRIDE, LANES), F32)],
        compiler_params=pltpu.CompilerParams(
            dimension_semantics=("arbitrary",), vmem_limit_bytes=VMEM_LIMIT_BYTES),
        name="qkv_proj",
    )(h2d, g, w_in, cos_t, sin_t, *cast_operands)
    n_groups = len(DILATIONS)
    return outs[:n_groups], restore(outs[n_groups:]) if restore else None


def _attend(q_ref, k_ref, kp_ref, v_ref, vp_ref, ol_ref, first_step):
    def window(ref, prev_ref, r0, c0):
        if r0 > 0:
            return ref[r0 - ATTN_BLOCK:r0 + ATTN_BLOCK, c0:c0 + LANES]
        cur = ref[0:ATTN_BLOCK, c0:c0 + LANES]
        before = jnp.zeros_like(cur) if prev_ref is None else prev_ref[:, c0:c0 + LANES]
        return jnp.concatenate([before, cur], axis=0)

    two = 2 * ATTN_BLOCK
    qi = lax.broadcasted_iota(jnp.int32, (two, two), 0) % ATTN_BLOCK
    kj = lax.broadcasted_iota(jnp.int32, (two, two), 1)
    dist = qi + ATTN_BLOCK - kj
    band = (dist >= 0) & (dist <= ATTN_BLOCK)
    band_first = band & ((kj >= ATTN_BLOCK) | jnp.logical_not(first_step))
    mask_rest = jnp.where(band, 0.0, -jnp.inf)
    mask_first = jnp.where(band_first, 0.0, -jnp.inf)
    lane = lax.broadcasted_iota(jnp.int32, (ATTN_BLOCK, LANES), 1)
    head0 = lane < HEAD_DIM
    keep0 = jnp.where(head0, 1.0, 0.0).astype(BF16)
    keep1 = jnp.where(head0, 0.0, 1.0).astype(BF16)
    ones = jnp.ones((two, LANES), BF16)

    for i in range(q_ref.shape[0] // ATTN_BLOCK):
        mask = mask_first if i == 0 else mask_rest
        r0 = i * ATTN_BLOCK
        for p in range(q_ref.shape[1] // LANES):
            c0 = p * LANES
            q2 = q_ref[r0:r0 + ATTN_BLOCK, c0:c0 + LANES]
            qs = jnp.concatenate([q2 * keep0, q2 * keep1], axis=0)
            kk = window(k_ref, kp_ref, r0, c0)
            s = lax.dot_general(qs, kk, (((1,), (1,)), ((), ())),
                                preferred_element_type=F32)
            s = s + mask
            m = jnp.max(s, axis=-1, keepdims=True)
            e = jnp.exp2((s - m).astype(BF16))
            vv = jnp.concatenate([window(v_ref, vp_ref, r0, c0), ones], axis=1)
            pv = jnp.dot(e, vv, preferred_element_type=F32)
            pick = lambda x: jnp.where(head0, x[:ATTN_BLOCK], x[ATTN_BLOCK:])
            den = pick(pv[:, LANES:])
            ol_ref[0, r0:r0 + ATTN_BLOCK, c0:c0 + LANES] = pick(pv[:, :LANES]) * (1.0 / den)
            ol_ref[1, r0:r0 + ATTN_BLOCK, c0:c0 + LANES] = pick(m) + jnp.log2(den)


def _attn_kernel(*refs, plans):
    refs = iter(refs)
    inputs = []
    for has_prev, _ in plans:
        q_ref, k_ref = next(refs), next(refs)
        kp_ref = next(refs) if has_prev else None
        v_ref = next(refs)
        vp_ref = next(refs) if has_prev else None
        inputs.append((q_ref, k_ref, kp_ref, v_ref, vp_ref))
    for (has_prev, row_steps), group_inputs in zip(plans, inputs):
        first_step = (pl.program_id(1) % row_steps == 0) if has_prev else True
        _attend(*group_inputs, next(refs), first_step)


def _attention(qkv, batch, seq):
    in_specs, operands, out_specs, out_shapes, plans, steps = [], [], [], [], [], set()
    for qkv_g, dilation in zip(qkv, DILATIONS):
        m_len = seq // dilation
        mb = min(m_len, ATTN_ROWS_MAX)
        nblk = mb // ATTN_BLOCK
        n_res = min(dilation, ATTN_ROWS_MAX // mb)
        width = n_res * GROUP_WIDTH
        row_steps = m_len // mb
        has_prev = row_steps > 1
        steps.add(row_steps * (dilation // n_res))
        view = qkv_g.reshape(3, batch, m_len, dilation * GROUP_WIDTH)

        def rows(which, row_steps=row_steps, mb=mb, width=width):
            return pl.BlockSpec((None, None, mb, width),
                                lambda b, t: (which, b, t % row_steps, t // row_steps))

        def prev(which, row_steps=row_steps, nblk=nblk, width=width):
            return pl.BlockSpec(
                (None, None, ATTN_BLOCK, width),
                lambda b, t: (which, b, jnp.maximum((t % row_steps) * nblk - 1, 0), t // row_steps))

        if has_prev:
            specs = [rows(0), rows(1), prev(1), rows(2), prev(2)]
        else:
            specs = [rows(0), rows(1), rows(2)]
        in_specs += specs
        operands += [view] * len(specs)
        out_specs.append(pl.BlockSpec(
            (2, None, mb, width),
            lambda b, t, row_steps=row_steps: (0, b, t % row_steps, t // row_steps)))
        out_shapes.append(jax.ShapeDtypeStruct((2, batch, m_len, dilation * GROUP_WIDTH), F32))
        plans.append((has_prev, row_steps))
    (n_steps,) = steps
    outs = pl.pallas_call(
        functools.partial(_attn_kernel, plans=tuple(plans)),
        grid=(batch, n_steps),
        in_specs=in_specs,
        out_specs=out_specs,
        out_shape=out_shapes,
        compiler_params=pltpu.CompilerParams(
            dimension_semantics=("arbitrary", "arbitrary"), vmem_limit_bytes=VMEM_LIMIT_BYTES),
        name="dilated_attn",
    )(*operands)
    return [x.reshape(2, -1, x.shape[-1]) for x in outs]


def _mix_kernel(h_ref, g_ref, ol0_ref, ol1_ref, ol2_ref,
                wr_ref, cw_ref, lng_ref, lnb_ref, sgw_ref, sgb_ref,
                wa_ref, wb_ref, wc_ref, wo_ref, out_ref, carry_ref, yc_ref, nat_ref, stage_ref,
                *, tiles_per_seq, conv_w, sg_w):
    tm, d_model = h_ref.shape
    halves = GROUP_WIDTH // LANES
    w_skip = 3 * len(DILATIONS) * GROUP_WIDTH

    @pl.when(pl.program_id(0) % tiles_per_seq == 0)
    def _():
        carry_ref[...] = jnp.zeros_like(carry_ref)

    att_refs = tuple(ol_ref.at[k] for ol_ref in (ol0_ref, ol1_ref, ol2_ref) for k in range(2))
    stage = 0
    for slot, src_ref in enumerate(att_refs):
        d = DILATIONS[slot // 2]
        if d == 1:
            continue
        d1 = d if d <= FREE_STRIDE else FREE_STRIDE
        d2 = d // d1
        for half in range(halves):
            dst = slot * halves + half
            for r1 in range(d1):
                if d2 == 1:
                    l0 = r1 * GROUP_WIDTH + half * LANES
                    nat_ref[dst, pl.ds(r1, tm // d1, stride=d1), :] = src_ref[:, l0:l0 + LANES]
                    continue
                for r2 in range(d2):
                    l0 = (r1 + d1 * r2) * GROUP_WIDTH + half * LANES
                    stage_ref[stage, pl.ds(r2, tm // d, stride=d2), :] = src_ref[:, l0:l0 + LANES]
                nat_ref[dst, pl.ds(r1, tm // d1, stride=d1), :] = stage_ref[stage]
                stage += 1

    def natural(slot, rs):
        if DILATIONS[slot // 2] == 1:
            return att_refs[slot][rs, :]
        return jnp.concatenate([nat_ref[slot * halves + half, rs, :] for half in range(halves)], axis=1)

    c_b, c_c, c_u, c_v, c_g = conv_w, 2 * conv_w, 3 * conv_w, 3 * conv_w + sg_w, 3 * conv_w + 2 * sg_w
    ti = lax.broadcasted_iota(jnp.int32, (SG_CHUNK, SG_CHUNK), 0)
    si = lax.broadcasted_iota(jnp.int32, (SG_CHUNK, SG_CHUNK), 1)
    gch = sg_w // SG_GROUPS
    w_sg = [jnp.where(ti >= si, sgw_ref[g], 0.0).astype(BF16) for g in range(SG_GROUPS)]
    cw = cw_ref[...]
    carry = carry_ref[...]

    for s in range(tm // MIX_SUB_ROWS):
        r0 = s * MIX_SUB_ROWS
        rs = slice(r0, r0 + MIX_SUB_ROWS)
        h = h_ref[rs, :]
        a = _rms_norm(h, g_ref[...]).astype(BF16)

        def project(c0, c1):
            return jnp.dot(a, wr_ref[:, w_skip + c0:w_skip + c1], preferred_element_type=F32)

        z_conv = project(0, c_u)
        z_sg = project(c_u, c_g)

        zx, zb, zc = z_conv[:, :c_b], z_conv[:, c_b:c_c], z_conv[:, c_c:c_u]
        u = zc * zx
        ext = jnp.concatenate([carry, u], axis=0)
        u1 = pltpu.roll(ext, 1, 0)[SUBLANES:]
        u2 = pltpu.roll(ext, 2, 0)[SUBLANES:]
        carry = u[MIX_SUB_ROWS - SUBLANES:]
        yb = (zb * (cw[0:1] * u2 + cw[1:2] * u1 + cw[2:3] * u)).astype(BF16)

        z_g0 = project(c_g, c_g + d_model)

        l0, l1, l2 = natural(1, rs), natural(3, rs), natural(5, rs)
        mx = jnp.maximum(jnp.maximum(l0, l1), l2)
        e0, e1, e2 = jnp.exp2(l0 - mx), jnp.exp2(l1 - mx), jnp.exp2(l2 - mx)
        inv = 1.0 / (e0 + e1 + e2)
        ya = jnp.concatenate([natural(0, rs) * (e0 * inv), natural(2, rs) * (e1 * inv),
                              natural(4, rs) * (e2 * inv)], axis=1).astype(BF16)
        pa = jnp.dot(ya, wa_ref[...], preferred_element_type=F32)

        us = _gelu(z_sg[:, :sg_w])
        vs = _gelu(z_sg[:, sg_w:])
        mu = jnp.mean(vs, axis=-1, keepdims=True)
        xc = vs - mu
        var = jnp.mean(xc * xc, axis=-1, keepdims=True)
        vln = (xc * lax.rsqrt(var + LN_EPS) * lng_ref[...] + lnb_ref[...]).astype(BF16)

        z_g1 = project(c_g + d_model, c_g + 2 * d_model)
        pb = jnp.dot(yb, wb_ref[...], preferred_element_type=F32)
        m = _sigmoid(z_g0) * pa + _sigmoid(z_g1) * pb

        for g in range(SG_GROUPS):
            bias = sgb_ref[g]
            cs = slice(g * gch, (g + 1) * gch)
            for n in range(MIX_SUB_ROWS // SG_CHUNK):
                ns = slice(n * SG_CHUNK, (n + 1) * SG_CHUNK)
                sv = jnp.dot(w_sg[g], vln[ns, cs], preferred_element_type=F32) + bias
                yc_ref[r0 + n * SG_CHUNK:r0 + (n + 1) * SG_CHUNK, cs] = (us[ns, cs] * sv).astype(BF16)

        z_g2 = project(c_g + 2 * d_model, c_g + 3 * d_model)
        pc = jnp.dot(yc_ref[rs, :], wc_ref[...], preferred_element_type=F32)
        m = m + _sigmoid(z_g2) * pc
        out_ref[rs, :] = h + jnp.dot(m.astype(BF16), wo_ref[...], preferred_element_type=F32)

    carry_ref[...] = carry


def _mixer_merge(layer, h2d, g, att, w_in, conv_w, ln_g, ln_b, sg_w, sg_b, w_a, w_b, w_c, w_o, seq):
    n_tok, d_model = h2d.shape
    tm = MIX_ROWS
    conv_width = conv_w.shape[-1]
    sg_width = ln_g.shape[-1]
    row_tile = lambda width: pl.BlockSpec((tm, width), lambda i: (i, 0))
    att_tiles = [pl.BlockSpec((2, tm // d, d * GROUP_WIDTH), lambda i: (0, i, 0)) for d in DILATIONS]
    n_stage = 2 * (GROUP_WIDTH // LANES) * FREE_STRIDE * sum(d > FREE_STRIDE for d in DILATIONS)
    kern = functools.partial(_mix_kernel, tiles_per_seq=seq // tm, conv_w=conv_width, sg_w=sg_width)
    return pl.pallas_call(
        kern,
        grid=(n_tok // tm,),
        in_specs=[row_tile(d_model), _resident(g, layer)]
        + att_tiles
        + [_resident(w_in, 0)]
        + [_resident(x, layer) for x in (conv_w, ln_g, ln_b, sg_w, sg_b, w_a, w_b, w_c, w_o)],
        out_specs=row_tile(d_model),
        out_shape=jax.ShapeDtypeStruct((n_tok, d_model), F32),
        scratch_shapes=[pltpu.VMEM((SUBLANES, conv_width), F32),
                        pltpu.VMEM((tm, sg_width), BF16),
                        pltpu.VMEM((2 * len(DILATIONS) * (GROUP_WIDTH // LANES), tm, LANES), F32),
                        pltpu.VMEM((max(n_stage, 1), tm // FREE_STRIDE, LANES), F32)],
        compiler_params=pltpu.CompilerParams(
            dimension_semantics=("arbitrary",), vmem_limit_bytes=VMEM_LIMIT_BYTES),
        name="mixer_merge",
    )(h2d, g, *att, w_in, conv_w, ln_g, ln_b, sg_w, sg_b, w_a, w_b, w_c, w_o)


def _mlp_kernel(h_ref, gm_ref, wu_ref, wd_ref, p_ref, gp_ref, wg_ref, wp_ref, gf_ref, *refs,
                final_norm):
    n_cast = len(refs) // 2
    out_ref = refs[n_cast]
    for src_ref, dst_ref in zip(refs[:n_cast], refs[n_cast + 1:]):
        dst_ref[...] = src_ref[...].astype(BF16)
    d_ff = wu_ref.shape[1]

    for s in range(h_ref.shape[0] // MLP_SUB_ROWS):
        rs = slice(s * MLP_SUB_ROWS, (s + 1) * MLP_SUB_ROWS)
        h = h_ref[rs, :]
        c = _rms_norm(h, gm_ref[...]).astype(BF16)
        acc = h
        for f0 in range(0, d_ff, MLP_FF_TILE):
            t = jnp.dot(c, wu_ref[:, f0:f0 + MLP_FF_TILE], preferred_element_type=F32)
            t = jnp.square(jnp.maximum(t, 0.0)).astype(BF16)
            acc = acc + jnp.dot(t, wd_ref[f0:f0 + MLP_FF_TILE, :], preferred_element_type=F32)
        e = _rms_norm(acc, gp_ref[...]).astype(BF16)
        gate = _sigmoid(jnp.dot(e, wg_ref[...], preferred_element_type=F32))
        emb = jnp.dot(p_ref[rs, :].astype(BF16), wp_ref[...], preferred_element_type=F32)
        out = acc + gate * emb
        out_ref[rs, :] = _rms_norm(out, gf_ref[...]) if final_norm else out


def _mlp_ple(layer, h2d, g_mlp, w_up, w_down, p3d, g_ple, w_pg, w_pe, g_final, next_weights):
    n_tok, d_model = h2d.shape
    tm = MLP_ROWS
    steps = n_tok // tm
    final_norm = not next_weights
    chunk = lambda w: (None, w.shape[1] // steps, w.shape[2])
    assert all(w.shape[1] % (steps * BF16_SUBLANES) == 0 for w in next_weights)
    row_tile = pl.BlockSpec((tm, d_model), lambda i: (i, 0))
    outs = pl.pallas_call(
        functools.partial(_mlp_kernel, final_norm=final_norm),
        grid=(steps,),
        in_specs=[
            row_tile,
            _resident(g_mlp, layer),
            _resident(w_up, 0),
            _resident(w_down, 0),
            pl.BlockSpec((None, tm, p3d.shape[-1]), lambda i: (layer, i, 0)),
            _resident(g_ple, layer),
            _resident(w_pg, layer),
            _resident(w_pe, layer),
            _resident(g_final, 0),
        ] + [pl.BlockSpec(chunk(w), lambda i: (layer + 1, i, 0)) for w in next_weights],
        out_specs=[row_tile] + [pl.BlockSpec(chunk(w), lambda i: (0, i, 0)) for w in next_weights],
        out_shape=[jax.ShapeDtypeStruct((n_tok, d_model), F32)]
        + [jax.ShapeDtypeStruct((1,) + w.shape[1:], BF16) for w in next_weights],
        compiler_params=pltpu.CompilerParams(
            dimension_semantics=("arbitrary",), vmem_limit_bytes=VMEM_LIMIT_BYTES),
        name="mlp_ple",
    )(h2d, g_mlp, w_up, w_down, p3d, g_ple, w_pg, w_pe, g_final, *next_weights)
    return outs[0], outs[1:]


def _cast_plan(first_layer_of, whole, steps):
    in_specs, out_specs, out_shape, operands = [], [], [], []
    for w in first_layer_of:
        block = (None, w.shape[1] // steps, w.shape[2])
        in_specs.append(pl.BlockSpec(block, lambda i: (0, i, 0)))
        out_specs.append(pl.BlockSpec(block, lambda i: (0, i, 0)))
        out_shape.append(jax.ShapeDtypeStruct((1,) + w.shape[1:], BF16))
        operands.append(w)
    for w in whole:
        flat = w.reshape(-1, w.shape[-1])
        block = (flat.shape[0] // steps, flat.shape[1])
        in_specs.append(pl.BlockSpec(block, lambda i: (i, 0)))
        out_specs.append(pl.BlockSpec(block, lambda i: (i, 0)))
        out_shape.append(jax.ShapeDtypeStruct(flat.shape, BF16))
        operands.append(flat)
    assert all(spec.block_shape[-2] % BF16_SUBLANES == 0 for spec in out_specs)
    n = len(first_layer_of)

    def restore(outs):
        return outs[:n], [o.reshape(w.shape) for o, w in zip(outs[n:], whole)]

    return operands, in_specs, out_specs, out_shape, restore


def _rotary_tables(positions):
    inv_freq = ROPE_THETA ** (-(jnp.arange(0, 2 * ROT_HALF, 2, dtype=F32) / (2 * ROT_HALF)))
    n_tok = positions.size
    pack = LANES // (2 * ROT_HALF)
    ang = (positions.astype(F32).reshape(n_tok // pack, pack, 1) * inv_freq).reshape(n_tok // pack, -1)
    cos8 = jnp.cos(ang).reshape(n_tok, ROT_HALF)
    sin8 = jnp.sin(ang).reshape(n_tok, ROT_HALF)
    in_head = jnp.arange(LANES) % HEAD_DIM
    rotated = in_head < 2 * ROT_HALF
    sign = jnp.where(in_head < ROT_HALF, -1.0, 1.0).astype(F32)
    spread = lambda t: jnp.tile(t, (1, LANES // ROT_HALF))
    return (jnp.where(rotated, spread(cos8), 1.0), jnp.where(rotated, sign * spread(sin8), 0.0))


def kernel(x, p, positions, norm_mix_g, w_in, conv_w, sg_ln_g, sg_ln_b, sg_w, sg_b,
           w_branch_a, w_branch_b, w_branch_c, w_out, norm_mlp_g, w_up, w_down,
           norm_ple_g, w_ple_gate, w_ple_proj, norm_final_g):
    batch, seq, d_model = x.shape
    depth = w_in.shape[0]
    n_tok = batch * seq
    qkv_width = 3 * len(DILATIONS) * GROUP_WIDTH
    assert seq % (max(DILATIONS) * ATTN_BLOCK) == 0 and seq % ATTN_ROWS_MAX == 0
    assert n_tok % MLP_ROWS == 0 and seq % MIX_ROWS == 0

    cos_t, sin_t = _rotary_tables(positions)
    rows = lambda v: v.reshape(v.shape[0], 1, v.shape[-1])
    sg_bias = jnp.broadcast_to(sg_b[:, :, :, None], sg_w.shape)
    big_f32 = (w_in, w_up, w_down)
    small_f32 = (w_branch_a, w_branch_b, w_branch_c, w_out, w_ple_gate, w_ple_proj)
    w_in_l = w_in[:1, :, :qkv_width].astype(BF16)
    first_casts = _cast_plan(big_f32, small_f32, n_tok // QKV_ROWS)
    g_mix, g_mlp, g_ple = rows(norm_mix_g), rows(norm_mlp_g), rows(norm_ple_g)
    ln_g, ln_b = rows(sg_ln_g), rows(sg_ln_b)
    p3d = p.reshape(depth, n_tok, p.shape[-1])

    g_final = norm_final_g.reshape(1, 1, -1)

    h = x.reshape(n_tok, d_model)
    for i in range(depth):
        qkv, cast = _qkv_projection(i, h, g_mix, w_in_l, cos_t, sin_t,
                                    first_casts if i == 0 else None)
        if cast:
            (w_in_l, w_up_l, w_down_l), (w_a, w_b, w_c, w_o, w_pg, w_pe) = cast
        att = _attention(qkv, batch, seq)
        h = _mixer_merge(i, h, g_mix, att, w_in_l, conv_w, ln_g, ln_b, sg_w, sg_bias,
                         w_a, w_b, w_c, w_o, seq)
        h, nxt = _mlp_ple(i, h, g_mlp, w_up_l, w_down_l, p3d, g_ple, w_pg, w_pe, g_final,
                          big_f32 if i + 1 < depth else ())
        if nxt:
            w_in_l, w_up_l, w_down_l = nxt
    return h.reshape(batch, seq, d_model)
```

```python
import functools
import math

import jax
import jax.numpy as jnp
from jax import lax
from jax.experimental import pallas as pl
from jax.experimental.pallas import tpu as pltpu

F32 = jnp.float32
BF16 = jnp.bfloat16

HEAD_DIM = 64
HEADS_PER_GROUP = 4
GROUP_WIDTH = HEADS_PER_GROUP * HEAD_DIM
DILATIONS = (1, 4, 16)
ATTN_BLOCK = 128
ROT_HALF = 8
ROPE_THETA = 500000.0
CONV_K = 3
SG_CHUNK = 128
SG_GROUPS = 4
RMS_EPS = 1e-6
LN_EPS = 1e-5
QUERY_SCALE = math.log2(math.e) * HEAD_DIM ** -0.5

LANES = 128
SUBLANES = 8
BF16_SUBLANES = 16
FREE_STRIDE = 4
VMEM_LIMIT_BYTES = 56 * 1024 * 1024

QKV_ROWS = 1024
QKV_SUB_ROWS = 256
ATTN_ROWS_MAX = 2048
MIX_ROWS = 512
MIX_SUB_ROWS = 256
MLP_ROWS = 512
MLP_SUB_ROWS = 512
MLP_FF_TILE = 1024
CAST_STEPS = 4


def _rms_norm(h, g):
    ms = jnp.mean(h * h, axis=-1, keepdims=True)
    return h * lax.rsqrt(ms + RMS_EPS) * g


def _sigmoid(x):
    return 1.0 / (1.0 + jnp.exp(-x))


def _gelu(x):
    return 0.5 * x * (1.0 + lax.erf(x * (1.0 / math.sqrt(2.0))))


def _resident(stacked, layer):
    shape = stacked.shape[1:]
    index = (layer,) + (0,) * len(shape)
    return pl.BlockSpec((None,) + shape, lambda *_: index, pipeline_mode=pl.Buffered(1))


def _qkv_kernel(h_ref, g_ref, w_ref, cos_ref, sin_ref, *refs, n_cast):
    n_groups = len(DILATIONS)
    outs = refs[n_cast:n_cast + n_groups]
    slab_ref, slab2_ref = refs[-2], refs[-1]
    for src_ref, dst_ref in zip(refs[:n_cast], refs[n_cast + n_groups:-2]):
        dst_ref[...] = src_ref[...].astype(BF16)
    sub = QKV_SUB_ROWS
    lane = lax.broadcasted_iota(jnp.int32, (sub, LANES), 1)
    low_half = (lane % HEAD_DIM) < ROT_HALF
    slab = slab2 = 0
    for s in range(h_ref.shape[0] // sub):
        rs = slice(s * sub, (s + 1) * sub)
        a = _rms_norm(h_ref[rs, :], g_ref[...]).astype(BF16)
        cos_k, sin_k = cos_ref[rs, :], sin_ref[rs, :]
        cos_sin = ((cos_k * QUERY_SCALE, sin_k * QUERY_SCALE), (cos_k, sin_k))
        for which in range(3):
            for g, d in enumerate(DILATIONS):
                c0 = (which * n_groups + g) * GROUP_WIDTH
                z = jnp.dot(a, w_ref[:, c0:c0 + GROUP_WIDTH], preferred_element_type=F32)
                out_rows = slice(s * sub // d, (s + 1) * sub // d)
                for half in range(GROUP_WIDTH // LANES):
                    t = z[:, half * LANES:(half + 1) * LANES]
                    if which < 2:
                        partner = jnp.where(low_half,
                                            pltpu.roll(t, LANES - ROT_HALF, 1),
                                            pltpu.roll(t, ROT_HALF, 1))
                        cos_t, sin_t = cos_sin[which]
                        t = t * cos_t + partner * sin_t
                    if d == 1:
                        outs[g][which, out_rows, half * LANES:(half + 1) * LANES] = t.astype(BF16)
                    else:
                        slab_ref[slab] = t
                        d1 = d if d <= FREE_STRIDE else FREE_STRIDE
                        d2 = d // d1
                        for r1 in range(d1):
                            part = slab_ref[slab, pl.ds(r1, sub // d1, stride=d1), :]
                            if d2 == 1:
                                l0 = r1 * GROUP_WIDTH + half * LANES
                                outs[g][which, out_rows, l0:l0 + LANES] = part.astype(BF16)
                                continue
                            slab2_ref[slab2] = part
                            for r2 in range(d2):
                                l0 = (r1 + d1 * r2) * GROUP_WIDTH + half * LANES
                                rows = slab2_ref[slab2, pl.ds(r2, sub // d, stride=d2), :]
                                outs[g][which, out_rows, l0:l0 + LANES] = rows.astype(BF16)
                            slab2 += 1
                        slab += 1


def _qkv_projection(layer, h2d, g, w_in, cos_t, sin_t, cast_plan=None):
    n_tok, d_model = h2d.shape
    tm = QKV_ROWS
    cast_operands, cast_in, cast_out, cast_shape, restore = cast_plan or ([], [], [], [], None)
    qkv_width = 3 * len(DILATIONS) * GROUP_WIDTH
    per_group = 3 * (GROUP_WIDTH // LANES) * (tm // QKV_SUB_ROWS)
    n_slabs = per_group * sum(d > 1 for d in DILATIONS)
    n_slabs2 = per_group * FREE_STRIDE * sum(d > FREE_STRIDE for d in DILATIONS)
    row_tile = pl.BlockSpec((tm, d_model), lambda i: (i, 0))
    out_specs = [pl.BlockSpec((3, tm // d, d * GROUP_WIDTH), lambda i: (0, i, 0)) for d in DILATIONS]
    out_shape = [jax.ShapeDtypeStruct((3, n_tok // d, d * GROUP_WIDTH), BF16) for d in DILATIONS]
    outs = pl.pallas_call(
        functools.partial(_qkv_kernel, n_cast=len(cast_operands)),
        grid=(n_tok // tm,),
        in_specs=[row_tile, _resident(g, layer)] + [
            pl.BlockSpec((None, d_model, qkv_width), lambda i: (0, 0, 0),
                         pipeline_mode=pl.Buffered(1)),
            pl.BlockSpec((tm, LANES), lambda i: (i, 0)),
            pl.BlockSpec((tm, LANES), lambda i: (i, 0)),
        ] + cast_in,
        out_specs=out_specs + cast_out,
        out_shape=out_shape + cast_shape,
        scratch_shapes=[pltpu.VMEM((n_slabs, QKV_SUB_ROWS, LANES), F32),
                        pltpu.VMEM((max(n_slabs2, 1), QKV_SUB_ROWS // FREE_STRIDE, LANES), F32)],
        compiler_params=pltpu.CompilerParams(
            dimension_semantics=("arbitrary",), vmem_limit_bytes=VMEM_LIMIT_BYTES),
        name="qkv_proj",
    )(h2d, g, w_in, cos_t, sin_t, *cast_operands)
    n_groups = len(DILATIONS)
    return outs[:n_groups], restore(outs[n_groups:]) if restore else None


def _attend(q_ref, k_ref, kp_ref, v_ref, vp_ref, ol_ref, first_step):
    def window(ref, prev_ref, r0, c0):
        if r0 > 0:
            return ref[r0 - ATTN_BLOCK:r0 + ATTN_BLOCK, c0:c0 + LANES]
        cur = ref[0:ATTN_BLOCK, c0:c0 + LANES]
        before = jnp.zeros_like(cur) if prev_ref is None else prev_ref[:, c0:c0 + LANES]
        return jnp.concatenate([before, cur], axis=0)

    two = 2 * ATTN_BLOCK
    qi = lax.broadcasted_iota(jnp.int32, (two, two), 0) % ATTN_BLOCK
    kj = lax.broadcasted_iota(jnp.int32, (two, two), 1)
    dist = qi + ATTN_BLOCK - kj
    band = (dist >= 0) & (dist <= ATTN_BLOCK)
    band_first = band & ((kj >= ATTN_BLOCK) | jnp.logical_not(first_step))
    mask_rest = jnp.where(band, 0.0, -jnp.inf)
    mask_first = jnp.where(band_first, 0.0, -jnp.inf)
    lane = lax.broadcasted_iota(jnp.int32, (ATTN_BLOCK, LANES), 1)
    head0 = lane < HEAD_DIM
    keep0 = jnp.where(head0, 1.0, 0.0).astype(BF16)
    keep1 = jnp.where(head0, 0.0, 1.0).astype(BF16)
    ones = jnp.ones((two, LANES), BF16)

    for i in range(q_ref.shape[0] // ATTN_BLOCK):
        mask = mask_first if i == 0 else mask_rest
        r0 = i * ATTN_BLOCK
        for p in range(q_ref.shape[1] // LANES):
            c0 = p * LANES
            q2 = q_ref[r0:r0 + ATTN_BLOCK, c0:c0 + LANES]
            qs = jnp.concatenate([q2 * keep0, q2 * keep1], axis=0)
            kk = window(k_ref, kp_ref, r0, c0)
            s = lax.dot_general(qs, kk, (((1,), (1,)), ((), ())),
                                preferred_element_type=F32)
            s = s + mask
            m = jnp.max(s, axis=-1, keepdims=True)
            e = jnp.exp2((s - m).astype(BF16))
            vv = jnp.concatenate([window(v_ref, vp_ref, r0, c0), ones], axis=1)
            pv = jnp.dot(e, vv, preferred_element_type=F32)
            pick = lambda x: jnp.where(head0, x[:ATTN_BLOCK], x[ATTN_BLOCK:])
            den = pick(pv[:, LANES:])
            ol_ref[0, r0:r0 + ATTN_BLOCK, c0:c0 + LANES] = pick(pv[:, :LANES]) * (1.0 / den)
            ol_ref[1, r0:r0 + ATTN_BLOCK, c0:c0 + LANES] = pick(m) + jnp.log2(den)


def _attn_kernel(*refs, plans):
    refs = iter(refs)
    inputs = []
    for has_prev, _ in plans:
        q_ref, k_ref = next(refs), next(refs)
        kp_ref = next(refs) if has_prev else None
        v_ref = next(refs)
        vp_ref = next(refs) if has_prev else None
        inputs.append((q_ref, k_ref, kp_ref, v_ref, vp_ref))
    for (has_prev, row_steps), group_inputs in zip(plans, inputs):
        first_step = (pl.program_id(1) % row_steps == 0) if has_prev else True
        _attend(*group_inputs, next(refs), first_step)


def _attention(qkv, batch, seq):
    in_specs, operands, out_specs, out_shapes, plans, steps = [], [], [], [], [], set()
    for qkv_g, dilation in zip(qkv, DILATIONS):
        m_len = seq // dilation
        mb = min(m_len, ATTN_ROWS_MAX)
        nblk = mb // ATTN_BLOCK
        n_res = min(dilation, ATTN_ROWS_MAX // mb)
        width = n_res * GROUP_WIDTH
        row_steps = m_len // mb
        has_prev = row_steps > 1
        steps.add(row_steps * (dilation // n_res))
        view = qkv_g.reshape(3, batch, m_len, dilation * GROUP_WIDTH)

        def rows(which, row_steps=row_steps, mb=mb, width=width):
            return pl.BlockSpec((None, None, mb, width),
                                lambda b, t: (which, b, t % row_steps, t // row_steps))

        def prev(which, row_steps=row_steps, nblk=nblk, width=width):
            return pl.BlockSpec(
                (None, None, ATTN_BLOCK, width),
                lambda b, t: (which, b, jnp.maximum((t % row_steps) * nblk - 1, 0), t // row_steps))

        if has_prev:
            specs = [rows(0), rows(1), prev(1), rows(2), prev(2)]
        else:
            specs = [rows(0), rows(1), rows(2)]
        in_specs += specs
        operands += [view] * len(specs)
        out_specs.append(pl.BlockSpec(
            (2, None, mb, width),
            lambda b, t, row_steps=row_steps: (0, b, t % row_steps, t // row_steps)))
        out_shapes.append(jax.ShapeDtypeStruct((2, batch, m_len, dilation * GROUP_WIDTH), F32))
        plans.append((has_prev, row_steps))
    (n_steps,) = steps
    outs = pl.pallas_call(
        functools.partial(_attn_kernel, plans=tuple(plans)),
        grid=(batch, n_steps),
        in_specs=in_specs,
        out_specs=out_specs,
        out_shape=out_shapes,
        compiler_params=pltpu.CompilerParams(
            dimension_semantics=("arbitrary", "arbitrary"), vmem_limit_bytes=VMEM_LIMIT_BYTES),
        name="dilated_attn",
    )(*operands)
    return [x.reshape(2, -1, x.shape[-1]) for x in outs]


def _mix_kernel(h_ref, g_ref, ol0_ref, ol1_ref, ol2_ref,
                wr_ref, cw_ref, lng_ref, lnb_ref, sgw_ref, sgb_ref,
                wa_ref, wb_ref, wc_ref, wo_ref, out_ref, carry_ref, yc_ref, nat_ref, stage_ref,
                *, tiles_per_seq, conv_w, sg_w):
    tm, d_model = h_ref.shape
    halves = GROUP_WIDTH // LANES
    w_skip = 3 * len(DILATIONS) * GROUP_WIDTH

    @pl.when(pl.program_id(0) % tiles_per_seq == 0)
    def _():
        carry_ref[...] = jnp.zeros_like(carry_ref)

    att_refs = tuple(ol_ref.at[k] for ol_ref in (ol0_ref, ol1_ref, ol2_ref) for k in range(2))
    stage = 0
    for slot, src_ref in enumerate(att_refs):
        d = DILATIONS[slot // 2]
        if d == 1:
            continue
        d1 = d if d <= FREE_STRIDE else FREE_STRIDE
        d2 = d // d1
        for half in range(halves):
            dst = slot * halves + half
            for r1 in range(d1):
                if d2 == 1:
                    l0 = r1 * GROUP_WIDTH + half * LANES
                    nat_ref[dst, pl.ds(r1, tm // d1, stride=d1), :] = src_ref[:, l0:l0 + LANES]
                    continue
                for r2 in range(d2):
                    l0 = (r1 + d1 * r2) * GROUP_WIDTH + half * LANES
                    stage_ref[stage, pl.ds(r2, tm // d, stride=d2), :] = src_ref[:, l0:l0 + LANES]
                nat_ref[dst, pl.ds(r1, tm // d1, stride=d1), :] = stage_ref[stage]
                stage += 1

    def natural(slot, rs):
        if DILATIONS[slot // 2] == 1:
            return att_refs[slot][rs, :]
        return jnp.concatenate([nat_ref[slot * halves + half, rs, :] for half in range(halves)], axis=1)

    c_b, c_c, c_u, c_v, c_g = conv_w, 2 * conv_w, 3 * conv_w, 3 * conv_w + sg_w, 3 * conv_w + 2 * sg_w
    ti = lax.broadcasted_iota(jnp.int32, (SG_CHUNK, SG_CHUNK), 0)
    si = lax.broadcasted_iota(jnp.int32, (SG_CHUNK, SG_CHUNK), 1)
    gch = sg_w // SG_GROUPS
    w_sg = [jnp.where(ti >= si, sgw_ref[g], 0.0).astype(BF16) for g in range(SG_GROUPS)]
    cw = cw_ref[...]
    carry = carry_ref[...]

    for s in range(tm // MIX_SUB_ROWS):
        r0 = s * MIX_SUB_ROWS
        rs = slice(r0, r0 + MIX_SUB_ROWS)
        h = h_ref[rs, :]
        a = _rms_norm(h, g_ref[...]).astype(BF16)

        def project(c0, c1):
            return jnp.dot(a, wr_ref[:, w_skip + c0:w_skip + c1], preferred_element_type=F32)

        z_conv = project(0, c_u)
        z_sg = project(c_u, c_g)

        zx, zb, zc = z_conv[:, :c_b], z_conv[:, c_b:c_c], z_conv[:, c_c:c_u]
        u = zc * zx
        ext = jnp.concatenate([carry, u], axis=0)
        u1 = pltpu.roll(ext, 1, 0)[SUBLANES:]
        u2 = pltpu.roll(ext, 2, 0)[SUBLANES:]
        carry = u[MIX_SUB_ROWS - SUBLANES:]
        yb = (zb * (cw[0:1] * u2 + cw[1:2] * u1 + cw[2:3] * u)).astype(BF16)

        z_g0 = project(c_g, c_g + d_model)

        l0, l1, l2 = natural(1, rs), natural(3, rs), natural(5, rs)
        mx = jnp.maximum(jnp.maximum(l0, l1), l2)
        e0, e1, e2 = jnp.exp2(l0 - mx), jnp.exp2(l1 - mx), jnp.exp2(l2 - mx)
        inv = 1.0 / (e0 + e1 + e2)
        ya = jnp.concatenate([natural(0, rs) * (e0 * inv), natural(2, rs) * (e1 * inv),
                              natural(4, rs) * (e2 * inv)], axis=1).astype(BF16)
        pa = jnp.dot(ya, wa_ref[...], preferred_element_type=F32)

        us = _gelu(z_sg[:, :sg_w])
        vs = _gelu(z_sg[:, sg_w:])
        mu = jnp.mean(vs, axis=-1, keepdims=True)
        xc = vs - mu
        var = jnp.mean(xc * xc, axis=-1, keepdims=True)
        vln = (xc * lax.rsqrt(var + LN_EPS) * lng_ref[...] + lnb_ref[...]).astype(BF16)

        z_g1 = project(c_g + d_model, c_g + 2 * d_model)
        pb = jnp.dot(yb, wb_ref[...], preferred_element_type=F32)
        m = _sigmoid(z_g0) * pa + _sigmoid(z_g1) * pb

        for g in range(SG_GROUPS):
            bias = sgb_ref[g]
            cs = slice(g * gch, (g + 1) * gch)
            for n in range(MIX_SUB_ROWS // SG_CHUNK):
                ns = slice(n * SG_CHUNK, (n + 1) * SG_CHUNK)
                sv = jnp.dot(w_sg[g], vln[ns, cs], preferred_element_type=F32) + bias
                yc_ref[r0 + n * SG_CHUNK:r0 + (n + 1) * SG_CHUNK, cs] = (us[ns, cs] * sv).astype(BF16)

        z_g2 = project(c_g + 2 * d_model, c_g + 3 * d_model)
        pc = jnp.dot(yc_ref[rs, :], wc_ref[...], preferred_element_type=F32)
        m = m + _sigmoid(z_g2) * pc
        out_ref[rs, :] = h + jnp.dot(m.astype(BF16), wo_ref[...], preferred_element_type=F32)

    carry_ref[...] = carry


def _mixer_merge(layer, h2d, g, att, w_in, conv_w, ln_g, ln_b, sg_w, sg_b, w_a, w_b, w_c, w_o, seq):
    n_tok, d_model = h2d.shape
    tm = MIX_ROWS
    conv_width = conv_w.shape[-1]
    sg_width = ln_g.shape[-1]
    row_tile = lambda width: pl.BlockSpec((tm, width), lambda i: (i, 0))
    att_tiles = [pl.BlockSpec((2, tm // d, d * GROUP_WIDTH), lambda i: (0, i, 0)) for d in DILATIONS]
    n_stage = 2 * (GROUP_WIDTH // LANES) * FREE_STRIDE * sum(d > FREE_STRIDE for d in DILATIONS)
    kern = functools.partial(_mix_kernel, tiles_per_seq=seq // tm, conv_w=conv_width, sg_w=sg_width)
    return pl.pallas_call(
        kern,
        grid=(n_tok // tm,),
        in_specs=[row_tile(d_model), _resident(g, layer)]
        + att_tiles
        + [_resident(w_in, 0)]
        + [_resident(x, layer) for x in (conv_w, ln_g, ln_b, sg_w, sg_b, w_a, w_b, w_c, w_o)],
        out_specs=row_tile(d_model),
        out_shape=jax.ShapeDtypeStruct((n_tok, d_model), F32),
        scratch_shapes=[pltpu.VMEM((SUBLANES, conv_width), F32),
                        pltpu.VMEM((tm, sg_width), BF16),
                        pltpu.VMEM((2 * len(DILATIONS) * (GROUP_WIDTH // LANES), tm, LANES), F32),
                        pltpu.VMEM((max(n_stage, 1), tm // FREE_STRIDE, LANES), F32)],
        compiler_params=pltpu.CompilerParams(
            dimension_semantics=("arbitrary",), vmem_limit_bytes=VMEM_LIMIT_BYTES),
        name="mixer_merge",
    )(h2d, g, *att, w_in, conv_w, ln_g, ln_b, sg_w, sg_b, w_a, w_b, w_c, w_o)


def _mlp_kernel(h_ref, gm_ref, wu_ref, wd_ref, p_ref, gp_ref, wg_ref, wp_ref, gf_ref, *refs,
                final_norm):
    n_cast = len(refs) // 2
    out_ref = refs[n_cast]
    for src_ref, dst_ref in zip(refs[:n_cast], refs[n_cast + 1:]):
        dst_ref[...] = src_ref[...].astype(BF16)
    d_ff = wu_ref.shape[1]

    for s in range(h_ref.shape[0] // MLP_SUB_ROWS):
        rs = slice(s * MLP_SUB_ROWS, (s + 1) * MLP_SUB_ROWS)
        h = h_ref[rs, :]
        c = _rms_norm(h, gm_ref[...]).astype(BF16)
        acc = h
        for f0 in range(0, d_ff, MLP_FF_TILE):
            t = jnp.dot(c, wu_ref[:, f0:f0 + MLP_FF_TILE], preferred_element_type=F32)
            t = jnp.square(jnp.maximum(t, 0.0)).astype(BF16)
            acc = acc + jnp.dot(t, wd_ref[f0:f0 + MLP_FF_TILE, :], preferred_element_type=F32)
        e = _rms_norm(acc, gp_ref[...]).astype(BF16)
        gate = _sigmoid(jnp.dot(e, wg_ref[...], preferred_element_type=F32))
        emb = jnp.dot(p_ref[rs, :].astype(BF16), wp_ref[...], preferred_element_type=F32)
        out = acc + gate * emb
        out_ref[rs, :] = _rms_norm(out, gf_ref[...]) if final_norm else out


def _mlp_ple(layer, h2d, g_mlp, w_up, w_down, p3d, g_ple, w_pg, w_pe, g_final, next_weights):
    n_tok, d_model = h2d.shape
    tm = MLP_ROWS
    steps = n_tok // tm
    final_norm = not next_weights
    chunk = lambda w: (None, w.shape[1] // steps, w.shape[2])
    assert all(w.shape[1] % (steps * BF16_SUBLANES) == 0 for w in next_weights)
    row_tile = pl.BlockSpec((tm, d_model), lambda i: (i, 0))
    outs = pl.pallas_call(
        functools.partial(_mlp_kernel, final_norm=final_norm),
        grid=(steps,),
        in_specs=[
            row_tile,
            _resident(g_mlp, layer),
            _resident(w_up, 0),
            _resident(w_down, 0),
            pl.BlockSpec((None, tm, p3d.shape[-1]), lambda i: (layer, i, 0)),
            _resident(g_ple, layer),
            _resident(w_pg, layer),
            _resident(w_pe, layer),
            _resident(g_final, 0),
        ] + [pl.BlockSpec(chunk(w), lambda i: (layer + 1, i, 0)) for w in next_weights],
        out_specs=[row_tile] + [pl.BlockSpec(chunk(w), lambda i: (0, i, 0)) for w in next_weights],
        out_shape=[jax.ShapeDtypeStruct((n_tok, d_model), F32)]
        + [jax.ShapeDtypeStruct((1,) + w.shape[1:], BF16) for w in next_weights],
        compiler_params=pltpu.CompilerParams(
            dimension_semantics=("arbitrary",), vmem_limit_bytes=VMEM_LIMIT_BYTES),
        name="mlp_ple",
    )(h2d, g_mlp, w_up, w_down, p3d, g_ple, w_pg, w_pe, g_final, *next_weights)
    return outs[0], outs[1:]


def _cast_kernel(src_ref, dst_ref):
    dst_ref[...] = src_ref[...].astype(BF16)


def _cast_leading_columns(w, width):
    block = pl.BlockSpec((None, w.shape[1] // CAST_STEPS, width), lambda i: (0, i, 0))
    return pl.pallas_call(
        _cast_kernel,
        grid=(CAST_STEPS,),
        in_specs=[block],
        out_specs=block,
        out_shape=jax.ShapeDtypeStruct((1, w.shape[1], width), BF16),
        compiler_params=pltpu.CompilerParams(dimension_semantics=("arbitrary",)),
        name="cast_qkv_weights",
    )(w)


def _cast_plan(first_layer_of, whole, steps):
    in_specs, out_specs, out_shape, operands = [], [], [], []
    for w in first_layer_of:
        block = (None, w.shape[1] // steps, w.shape[2])
        in_specs.append(pl.BlockSpec(block, lambda i: (0, i, 0)))
        out_specs.append(pl.BlockSpec(block, lambda i: (0, i, 0)))
        out_shape.append(jax.ShapeDtypeStruct((1,) + w.shape[1:], BF16))
        operands.append(w)
    for w in whole:
        flat = w.reshape(-1, w.shape[-1])
        block = (flat.shape[0] // steps, flat.shape[1])
        in_specs.append(pl.BlockSpec(block, lambda i: (i, 0)))
        out_specs.append(pl.BlockSpec(block, lambda i: (i, 0)))
        out_shape.append(jax.ShapeDtypeStruct(flat.shape, BF16))
        operands.append(flat)
    assert all(spec.block_shape[-2] % BF16_SUBLANES == 0 for spec in out_specs)
    n = len(first_layer_of)

    def restore(outs):
        return outs[:n], [o.reshape(w.shape) for o, w in zip(outs[n:], whole)]

    return operands, in_specs, out_specs, out_shape, restore


def _rotary_tables(positions):
    inv_freq = ROPE_THETA ** (-(jnp.arange(0, 2 * ROT_HALF, 2, dtype=F32) / (2 * ROT_HALF)))
    n_tok = positions.size
    pack = LANES // (2 * ROT_HALF)
    ang = (positions.astype(F32).reshape(n_tok // pack, pack, 1) * inv_freq).reshape(n_tok // pack, -1)
    cos8 = jnp.cos(ang).reshape(n_tok, ROT_HALF)
    sin8 = jnp.sin(ang).reshape(n_tok, ROT_HALF)
    in_head = jnp.arange(LANES) % HEAD_DIM
    rotated = in_head < 2 * ROT_HALF
    sign = jnp.where(in_head < ROT_HALF, -1.0, 1.0).astype(F32)
    spread = lambda t: jnp.tile(t, (1, LANES // ROT_HALF))
    return (jnp.where(rotated, spread(cos8), 1.0), jnp.where(rotated, sign * spread(sin8), 0.0))


def kernel(x, p, positions, norm_mix_g, w_in, conv_w, sg_ln_g, sg_ln_b, sg_w, sg_b,
           w_branch_a, w_branch_b, w_branch_c, w_out, norm_mlp_g, w_up, w_down,
           norm_ple_g, w_ple_gate, w_ple_proj, norm_final_g):
    batch, seq, d_model = x.shape
    depth = w_in.shape[0]
    n_tok = batch * seq
    qkv_width = 3 * len(DILATIONS) * GROUP_WIDTH
    assert seq % (max(DILATIONS) * ATTN_BLOCK) == 0 and seq % ATTN_ROWS_MAX == 0
    assert n_tok % MLP_ROWS == 0 and seq % MIX_ROWS == 0

    cos_t, sin_t = _rotary_tables(positions)
    rows = lambda v: v.reshape(v.shape[0], 1, v.shape[-1])
    sg_bias = jnp.broadcast_to(sg_b[:, :, :, None], sg_w.shape)
    big_f32 = (w_in, w_up, w_down)
    small_f32 = (w_branch_a, w_branch_b, w_branch_c, w_out, w_ple_gate, w_ple_proj)
    w_in_l = _cast_leading_columns(w_in, qkv_width)
    first_casts = _cast_plan(big_f32, small_f32, n_tok // QKV_ROWS)
    g_mix, g_mlp, g_ple = rows(norm_mix_g), rows(norm_mlp_g), rows(norm_ple_g)
    ln_g, ln_b = rows(sg_ln_g), rows(sg_ln_b)
    p3d = p.reshape(depth, n_tok, p.shape[-1])

    g_final = norm_final_g.reshape(1, 1, -1)

    h = x.reshape(n_tok, d_model)
    for i in range(depth):
        qkv, cast = _qkv_projection(i, h, g_mix, w_in_l, cos_t, sin_t,
                                    first_casts if i == 0 else None)
        if cast:
            (w_in_l, w_up_l, w_down_l), (w_a, w_b, w_c, w_o, w_pg, w_pe) = cast
        att = _attention(qkv, batch, seq)
        h = _mixer_merge(i, h, g_mix, att, w_in_l, conv_w, ln_g, ln_b, sg_w, sg_bias,
                         w_a, w_b, w_c, w_o, seq)
        h, nxt = _mlp_ple(i, h, g_mlp, w_up_l, w_down_l, p3d, g_ple, w_pg, w_pe, g_final,
                          big_f32 if i + 1 < depth else ())
        if nxt:
            w_in_l, w_up_l, w_down_l = nxt
    return h.reshape(batch, seq, d_model)
```

```python
import functools
import math

import jax
import jax.numpy as jnp
from jax import lax
from jax.experimental import pallas as pl
from jax.experimental.pallas import tpu as pltpu

F32 = jnp.float32
BF16 = jnp.bfloat16

HEAD_DIM = 64
HEADS_PER_GROUP = 4
GROUP_WIDTH = HEADS_PER_GROUP * HEAD_DIM
DILATIONS = (1, 4, 16)
ATTN_BLOCK = 128
ROT_HALF = 8
ROPE_THETA = 500000.0
SG_CHUNK = 128
SG_GROUPS = 4
RMS_EPS = 1e-6
LN_EPS = 1e-5
QUERY_SCALE = math.log2(math.e) * HEAD_DIM ** -0.5

LANES = 128
SUBLANES = 8
BF16_SUBLANES = 16
FREE_STRIDE = 4
VMEM_LIMIT_BYTES = 56 * 1024 * 1024

QKV_ROWS = 1024
QKV_SUB_ROWS = 256
ATTN_ROWS_MAX = 2048
MIX_ROWS = 512
MIX_SUB_ROWS = 256
MLP_ROWS = 512
MLP_SUB_ROWS = 512
MLP_FF_TILE = 1024
CAST_STEPS = 4


def _rms_norm(h, g):
    ms = jnp.mean(h * h, axis=-1, keepdims=True)
    return h * lax.rsqrt(ms + RMS_EPS) * g


def _sigmoid(x):
    return 1.0 / (1.0 + jnp.exp(-x))


def _gelu(x):
    return 0.5 * x * (1.0 + lax.erf(x * (1.0 / math.sqrt(2.0))))


def _resident(stacked, layer):
    shape = stacked.shape[1:]
    index = (layer,) + (0,) * len(shape)
    return pl.BlockSpec((None,) + shape, lambda *_: index, pipeline_mode=pl.Buffered(1))


def _qkv_kernel(h_ref, g_ref, w_ref, cos_ref, sin_ref, *refs, n_cast):
    n_groups = len(DILATIONS)
    outs = refs[n_cast:n_cast + n_groups]
    slab_ref, slab2_ref = refs[-2], refs[-1]
    for src_ref, dst_ref in zip(refs[:n_cast], refs[n_cast + n_groups:-2]):
        dst_ref[...] = src_ref[...].astype(BF16)
    sub = QKV_SUB_ROWS
    lane = lax.broadcasted_iota(jnp.int32, (sub, LANES), 1)
    low_half = (lane % HEAD_DIM) < ROT_HALF
    slab = slab2 = 0
    for s in range(h_ref.shape[0] // sub):
        rs = slice(s * sub, (s + 1) * sub)
        a = _rms_norm(h_ref[rs, :], g_ref[...]).astype(BF16)
        cos_k, sin_k = cos_ref[rs, :], sin_ref[rs, :]
        cos_sin = ((cos_k * QUERY_SCALE, sin_k * QUERY_SCALE), (cos_k, sin_k))
        for which in range(3):
            for g, d in enumerate(DILATIONS):
                c0 = (which * n_groups + g) * GROUP_WIDTH
                z = jnp.dot(a, w_ref[:, c0:c0 + GROUP_WIDTH], preferred_element_type=F32)
                out_rows = slice(s * sub // d, (s + 1) * sub // d)
                for half in range(GROUP_WIDTH // LANES):
                    t = z[:, half * LANES:(half + 1) * LANES]
                    if which < 2:
                        partner = jnp.where(low_half,
                                            pltpu.roll(t, LANES - ROT_HALF, 1),
                                            pltpu.roll(t, ROT_HALF, 1))
                        cos_t, sin_t = cos_sin[which]
                        t = t * cos_t + partner * sin_t
                    if d == 1:
                        outs[g][which, out_rows, half * LANES:(half + 1) * LANES] = t.astype(BF16)
                    else:
                        slab_ref[slab] = t
                        d1 = d if d <= FREE_STRIDE else FREE_STRIDE
                        d2 = d // d1
                        for r1 in range(d1):
                            part = slab_ref[slab, pl.ds(r1, sub // d1, stride=d1), :]
                            if d2 == 1:
                                l0 = r1 * GROUP_WIDTH + half * LANES
                                outs[g][which, out_rows, l0:l0 + LANES] = part.astype(BF16)
                                continue
                            slab2_ref[slab2] = part
                            for r2 in range(d2):
                                l0 = (r1 + d1 * r2) * GROUP_WIDTH + half * LANES
                                rows = slab2_ref[slab2, pl.ds(r2, sub // d, stride=d2), :]
                                outs[g][which, out_rows, l0:l0 + LANES] = rows.astype(BF16)
                            slab2 += 1
                        slab += 1


def _qkv_projection(layer, h2d, g, w_in, cos_t, sin_t, cast_plan=None):
    n_tok, d_model = h2d.shape
    tm = QKV_ROWS
    cast_operands, cast_in, cast_out, cast_shape, restore = cast_plan or ([], [], [], [], None)
    qkv_width = 3 * len(DILATIONS) * GROUP_WIDTH
    per_group = 3 * (GROUP_WIDTH // LANES) * (tm // QKV_SUB_ROWS)
    n_slabs = per_group * sum(d > 1 for d in DILATIONS)
    n_slabs2 = per_group * FREE_STRIDE * sum(d > FREE_STRIDE for d in DILATIONS)
    row_tile = pl.BlockSpec((tm, d_model), lambda i: (i, 0))
    out_specs = [pl.BlockSpec((3, tm // d, d * GROUP_WIDTH), lambda i: (0, i, 0)) for d in DILATIONS]
    out_shape = [jax.ShapeDtypeStruct((3, n_tok // d, d * GROUP_WIDTH), BF16) for d in DILATIONS]
    outs = pl.pallas_call(
        functools.partial(_qkv_kernel, n_cast=len(cast_operands)),
        grid=(n_tok // tm,),
        in_specs=[row_tile, _resident(g, layer)] + [
            pl.BlockSpec((None, d_model, qkv_width), lambda i: (0, 0, 0),
                         pipeline_mode=pl.Buffered(1)),
            pl.BlockSpec((tm, LANES), lambda i: (i, 0)),
            pl.BlockSpec((tm, LANES), lambda i: (i, 0)),
        ] + cast_in,
        out_specs=out_specs + cast_out,
        out_shape=out_shape + cast_shape,
        scratch_shapes=[pltpu.VMEM((n_slabs, QKV_SUB_ROWS, LANES), F32),
                        pltpu.VMEM((max(n_slabs2, 1), QKV_SUB_ROWS // FREE_STRIDE, LANES), F32)],
        compiler_params=pltpu.CompilerParams(
            dimension_semantics=("arbitrary",), vmem_limit_bytes=VMEM_LIMIT_BYTES),
        name="qkv_proj",
    )(h2d, g, w_in, cos_t, sin_t, *cast_operands)
    n_groups = len(DILATIONS)
    return outs[:n_groups], restore(outs[n_groups:]) if restore else None


def _attend(q_ref, k_ref, kp_ref, v_ref, vp_ref, ol_ref, first_step):
    def window(ref, prev_ref, r0, c0):
        if r0 > 0:
            return ref[r0 - ATTN_BLOCK:r0 + ATTN_BLOCK, c0:c0 + LANES]
        cur = ref[0:ATTN_BLOCK, c0:c0 + LANES]
        before = jnp.zeros_like(cur) if prev_ref is None else prev_ref[:, c0:c0 + LANES]
        return jnp.concatenate([before, cur], axis=0)

    two = 2 * ATTN_BLOCK
    qi = lax.broadcasted_iota(jnp.int32, (two, two), 0) % ATTN_BLOCK
    kj = lax.broadcasted_iota(jnp.int32, (two, two), 1)
    dist = qi + ATTN_BLOCK - kj
    band = (dist >= 0) & (dist <= ATTN_BLOCK)
    band_first = band & ((kj >= ATTN_BLOCK) | jnp.logical_not(first_step))
    mask_rest = jnp.where(band, 0.0, -jnp.inf)
    mask_first = jnp.where(band_first, 0.0, -jnp.inf)
    lane = lax.broadcasted_iota(jnp.int32, (ATTN_BLOCK, LANES), 1)
    head0 = lane < HEAD_DIM
    keep0 = jnp.where(head0, 1.0, 0.0).astype(BF16)
    keep1 = jnp.where(head0, 0.0, 1.0).astype(BF16)
    ones = jnp.ones((two, LANES), BF16)

    for i in range(q_ref.shape[0] // ATTN_BLOCK):
        mask = mask_first if i == 0 else mask_rest
        r0 = i * ATTN_BLOCK
        for p in range(q_ref.shape[1] // LANES):
            c0 = p * LANES
            q2 = q_ref[r0:r0 + ATTN_BLOCK, c0:c0 + LANES]
            qs = jnp.concatenate([q2 * keep0, q2 * keep1], axis=0)
            kk = window(k_ref, kp_ref, r0, c0)
            s = lax.dot_general(qs, kk, (((1,), (1,)), ((), ())),
                                preferred_element_type=F32)
            s = s + mask
            m = jnp.max(s, axis=-1, keepdims=True)
            e = jnp.exp2((s - m).astype(BF16))
            vv = jnp.concatenate([window(v_ref, vp_ref, r0, c0), ones], axis=1)
            pv = jnp.dot(e, vv, preferred_element_type=F32)
            pick = lambda x: jnp.where(head0, x[:ATTN_BLOCK], x[ATTN_BLOCK:])
            den = pick(pv[:, LANES:])
            ol_ref[0, r0:r0 + ATTN_BLOCK, c0:c0 + LANES] = pick(pv[:, :LANES]) * (1.0 / den)
            ol_ref[1, r0:r0 + ATTN_BLOCK, c0:c0 + LANES] = pick(m) + jnp.log2(den)


def _attn_kernel(*refs, plans):
    refs = iter(refs)
    inputs = []
    for has_prev, _ in plans:
        q_ref, k_ref = next(refs), next(refs)
        kp_ref = next(refs) if has_prev else None
        v_ref = next(refs)
        vp_ref = next(refs) if has_prev else None
        inputs.append((q_ref, k_ref, kp_ref, v_ref, vp_ref))
    for (has_prev, row_steps), group_inputs in zip(plans, inputs):
        first_step = (pl.program_id(1) % row_steps == 0) if has_prev else True
        _attend(*group_inputs, next(refs), first_step)


def _attention(qkv, batch, seq):
    in_specs, operands, out_specs, out_shapes, plans, steps = [], [], [], [], [], set()
    for qkv_g, dilation in zip(qkv, DILATIONS):
        m_len = seq // dilation
        mb = min(m_len, ATTN_ROWS_MAX)
        nblk = mb // ATTN_BLOCK
        n_res = min(dilation, ATTN_ROWS_MAX // mb)
        width = n_res * GROUP_WIDTH
        row_steps = m_len // mb
        has_prev = row_steps > 1
        steps.add(row_steps * (dilation // n_res))
        view = qkv_g.reshape(3, batch, m_len, dilation * GROUP_WIDTH)

        def rows(which, row_steps=row_steps, mb=mb, width=width):
            return pl.BlockSpec((None, None, mb, width),
                                lambda b, t: (which, b, t % row_steps, t // row_steps))

        def prev(which, row_steps=row_steps, nblk=nblk, width=width):
            return pl.BlockSpec(
                (None, None, ATTN_BLOCK, width),
                lambda b, t: (which, b, jnp.maximum((t % row_steps) * nblk - 1, 0), t // row_steps))

        if has_prev:
            specs = [rows(0), rows(1), prev(1), rows(2), prev(2)]
        else:
            specs = [rows(0), rows(1), rows(2)]
        in_specs += specs
        operands += [view] * len(specs)
        out_specs.append(pl.BlockSpec(
            (2, None, mb, width),
            lambda b, t, row_steps=row_steps: (0, b, t % row_steps, t // row_steps)))
        out_shapes.append(jax.ShapeDtypeStruct((2, batch, m_len, dilation * GROUP_WIDTH), F32))
        plans.append((has_prev, row_steps))
    (n_steps,) = steps
    outs = pl.pallas_call(
        functools.partial(_attn_kernel, plans=tuple(plans)),
        grid=(batch, n_steps),
        in_specs=in_specs,
        out_specs=out_specs,
        out_shape=out_shapes,
        compiler_params=pltpu.CompilerParams(
            dimension_semantics=("arbitrary", "arbitrary"), vmem_limit_bytes=VMEM_LIMIT_BYTES),
        name="dilated_attn",
    )(*operands)
    return [x.reshape(2, -1, x.shape[-1]) for x in outs]


def _mix_kernel(h_ref, g_ref, ol0_ref, ol1_ref, ol2_ref,
                wr_ref, cw_ref, lng_ref, lnb_ref, sgw_ref, sgb_ref,
                wa_ref, wb_ref, wc_ref, wo_ref, out_ref, carry_ref, yc_ref, nat_ref, stage_ref,
                *, tiles_per_seq, conv_w, sg_w):
    tm, d_model = h_ref.shape
    halves = GROUP_WIDTH // LANES
    w_skip = 3 * len(DILATIONS) * GROUP_WIDTH

    @pl.when(pl.program_id(0) % tiles_per_seq == 0)
    def _():
        carry_ref[...] = jnp.zeros_like(carry_ref)

    att_refs = tuple(ol_ref.at[k] for ol_ref in (ol0_ref, ol1_ref, ol2_ref) for k in range(2))
    stage = 0
    for slot, src_ref in enumerate(att_refs):
        d = DILATIONS[slot // 2]
        if d == 1:
            continue
        d1 = d if d <= FREE_STRIDE else FREE_STRIDE
        d2 = d // d1
        for half in range(halves):
            dst = slot * halves + half
            for r1 in range(d1):
                if d2 == 1:
                    l0 = r1 * GROUP_WIDTH + half * LANES
                    nat_ref[dst, pl.ds(r1, tm // d1, stride=d1), :] = src_ref[:, l0:l0 + LANES]
                    continue
                for r2 in range(d2):
                    l0 = (r1 + d1 * r2) * GROUP_WIDTH + half * LANES
                    stage_ref[stage, pl.ds(r2, tm // d, stride=d2), :] = src_ref[:, l0:l0 + LANES]
                nat_ref[dst, pl.ds(r1, tm // d1, stride=d1), :] = stage_ref[stage]
                stage += 1

    def natural(slot, rs):
        if DILATIONS[slot // 2] == 1:
            return att_refs[slot][rs, :]
        return jnp.concatenate([nat_ref[slot * halves + half, rs, :] for half in range(halves)], axis=1)

    c_b, c_c, c_u, c_v, c_g = conv_w, 2 * conv_w, 3 * conv_w, 3 * conv_w + sg_w, 3 * conv_w + 2 * sg_w
    ti = lax.broadcasted_iota(jnp.int32, (SG_CHUNK, SG_CHUNK), 0)
    si = lax.broadcasted_iota(jnp.int32, (SG_CHUNK, SG_CHUNK), 1)
    gch = sg_w // SG_GROUPS
    w_sg = [jnp.where(ti >= si, sgw_ref[g], 0.0).astype(BF16) for g in range(SG_GROUPS)]
    cw = cw_ref[...]
    carry = carry_ref[...]

    for s in range(tm // MIX_SUB_ROWS):
        r0 = s * MIX_SUB_ROWS
        rs = slice(r0, r0 + MIX_SUB_ROWS)
        h = h_ref[rs, :]
        a = _rms_norm(h, g_ref[...]).astype(BF16)

        def project(c0, c1):
            return jnp.dot(a, wr_ref[:, w_skip + c0:w_skip + c1], preferred_element_type=F32)

        z_conv = project(0, c_u)
        z_sg = project(c_u, c_g)

        zx, zb, zc = z_conv[:, :c_b], z_conv[:, c_b:c_c], z_conv[:, c_c:c_u]
        u = zc * zx
        ext = jnp.concatenate([carry, u], axis=0)
        u1 = pltpu.roll(ext, 1, 0)[SUBLANES:]
        u2 = pltpu.roll(ext, 2, 0)[SUBLANES:]
        carry = u[MIX_SUB_ROWS - SUBLANES:]
        yb = (zb * (cw[0:1] * u2 + cw[1:2] * u1 + cw[2:3] * u)).astype(BF16)

        z_g0 = project(c_g, c_g + d_model)

        l0, l1, l2 = natural(1, rs), natural(3, rs), natural(5, rs)
        mx = jnp.maximum(jnp.maximum(l0, l1), l2)
        e0, e1, e2 = jnp.exp2(l0 - mx), jnp.exp2(l1 - mx), jnp.exp2(l2 - mx)
        inv = 1.0 / (e0 + e1 + e2)
        ya = jnp.concatenate([natural(0, rs) * (e0 * inv), natural(2, rs) * (e1 * inv),
                              natural(4, rs) * (e2 * inv)], axis=1).astype(BF16)
        pa = jnp.dot(ya, wa_ref[...], preferred_element_type=F32)

        us = _gelu(z_sg[:, :sg_w])
        vs = _gelu(z_sg[:, sg_w:])
        mu = jnp.mean(vs, axis=-1, keepdims=True)
        xc = vs - mu
        var = jnp.mean(xc * xc, axis=-1, keepdims=True)
        vln = (xc * lax.rsqrt(var + LN_EPS) * lng_ref[...] + lnb_ref[...]).astype(BF16)

        z_g1 = project(c_g + d_model, c_g + 2 * d_model)
        pb = jnp.dot(yb, wb_ref[...], preferred_element_type=F32)
        m = _sigmoid(z_g0) * pa + _sigmoid(z_g1) * pb

        for g in range(SG_GROUPS):
            bias = sgb_ref[g]
            cs = slice(g * gch, (g + 1) * gch)
            for n in range(MIX_SUB_ROWS // SG_CHUNK):
                ns = slice(n * SG_CHUNK, (n + 1) * SG_CHUNK)
                sv = jnp.dot(w_sg[g], vln[ns, cs], preferred_element_type=F32) + bias
                yc_ref[r0 + n * SG_CHUNK:r0 + (n + 1) * SG_CHUNK, cs] = (us[ns, cs] * sv).astype(BF16)

        z_g2 = project(c_g + 2 * d_model, c_g + 3 * d_model)
        pc = jnp.dot(yc_ref[rs, :], wc_ref[...], preferred_element_type=F32)
        m = m + _sigmoid(z_g2) * pc
        out_ref[rs, :] = h + jnp.dot(m.astype(BF16), wo_ref[...], preferred_element_type=F32)

    carry_ref[...] = carry


def _mixer_merge(layer, h2d, g, att, w_in, conv_w, ln_g, ln_b, sg_w, sg_b, w_a, w_b, w_c, w_o, seq):
    n_tok, d_model = h2d.shape
    tm = MIX_ROWS
    conv_width = conv_w.shape[-1]
    sg_width = ln_g.shape[-1]
    row_tile = lambda width: pl.BlockSpec((tm, width), lambda i: (i, 0))
    att_tiles = [pl.BlockSpec((2, tm // d, d * GROUP_WIDTH), lambda i: (0, i, 0)) for d in DILATIONS]
    n_stage = 2 * (GROUP_WIDTH // LANES) * FREE_STRIDE * sum(d > FREE_STRIDE for d in DILATIONS)
    kern = functools.partial(_mix_kernel, tiles_per_seq=seq // tm, conv_w=conv_width, sg_w=sg_width)
    return pl.pallas_call(
        kern,
        grid=(n_tok // tm,),
        in_specs=[row_tile(d_model), _resident(g, layer)]
        + att_tiles
        + [_resident(w_in, 0)]
        + [_resident(x, layer) for x in (conv_w, ln_g, ln_b, sg_w, sg_b, w_a, w_b, w_c, w_o)],
        out_specs=row_tile(d_model),
        out_shape=jax.ShapeDtypeStruct((n_tok, d_model), F32),
        scratch_shapes=[pltpu.VMEM((SUBLANES, conv_width), F32),
                        pltpu.VMEM((tm, sg_width), BF16),
                        pltpu.VMEM((2 * len(DILATIONS) * (GROUP_WIDTH // LANES), tm, LANES), F32),
                        pltpu.VMEM((max(n_stage, 1), tm // FREE_STRIDE, LANES), F32)],
        compiler_params=pltpu.CompilerParams(
            dimension_semantics=("arbitrary",), vmem_limit_bytes=VMEM_LIMIT_BYTES),
        name="mixer_merge",
    )(h2d, g, *att, w_in, conv_w, ln_g, ln_b, sg_w, sg_b, w_a, w_b, w_c, w_o)


def _mlp_kernel(h_ref, gm_ref, wu_ref, wd_ref, p_ref, gp_ref, wg_ref, wp_ref, gf_ref, *refs,
                final_norm):
    n_cast = len(refs) // 2
    out_ref = refs[n_cast]
    for src_ref, dst_ref in zip(refs[:n_cast], refs[n_cast + 1:]):
        dst_ref[...] = src_ref[...].astype(BF16)
    d_ff = wu_ref.shape[1]

    for s in range(h_ref.shape[0] // MLP_SUB_ROWS):
        rs = slice(s * MLP_SUB_ROWS, (s + 1) * MLP_SUB_ROWS)
        h = h_ref[rs, :]
        c = _rms_norm(h, gm_ref[...]).astype(BF16)
        acc = h
        for f0 in range(0, d_ff, MLP_FF_TILE):
            t = jnp.dot(c, wu_ref[:, f0:f0 + MLP_FF_TILE], preferred_element_type=F32)
            t = jnp.square(jnp.maximum(t, 0.0)).astype(BF16)
            acc = acc + jnp.dot(t, wd_ref[f0:f0 + MLP_FF_TILE, :], preferred_element_type=F32)
        e = _rms_norm(acc, gp_ref[...]).astype(BF16)
        gate = _sigmoid(jnp.dot(e, wg_ref[...], preferred_element_type=F32))
        emb = jnp.dot(p_ref[rs, :].astype(BF16), wp_ref[...], preferred_element_type=F32)
        out = acc + gate * emb
        out_ref[rs, :] = _rms_norm(out, gf_ref[...]) if final_norm else out


def _mlp_ple(layer, h2d, g_mlp, w_up, w_down, p3d, g_ple, w_pg, w_pe, g_final, next_weights):
    n_tok, d_model = h2d.shape
    tm = MLP_ROWS
    steps = n_tok // tm
    final_norm = not next_weights
    chunk = lambda w: (None, w.shape[1] // steps, w.shape[2])
    assert all(w.shape[1] % (steps * BF16_SUBLANES) == 0 for w in next_weights)
    row_tile = pl.BlockSpec((tm, d_model), lambda i: (i, 0))
    outs = pl.pallas_call(
        functools.partial(_mlp_kernel, final_norm=final_norm),
        grid=(steps,),
        in_specs=[
            row_tile,
            _resident(g_mlp, layer),
            _resident(w_up, 0),
            _resident(w_down, 0),
            pl.BlockSpec((None, tm, p3d.shape[-1]), lambda i: (layer, i, 0)),
            _resident(g_ple, layer),
            _resident(w_pg, layer),
            _resident(w_pe, layer),
            _resident(g_final, 0),
        ] + [pl.BlockSpec(chunk(w), lambda i: (layer + 1, i, 0)) for w in next_weights],
        out_specs=[row_tile] + [pl.BlockSpec(chunk(w), lambda i: (0, i, 0)) for w in next_weights],
        out_shape=[jax.ShapeDtypeStruct((n_tok, d_model), F32)]
        + [jax.ShapeDtypeStruct((1,) + w.shape[1:], BF16) for w in next_weights],
        compiler_params=pltpu.CompilerParams(
            dimension_semantics=("arbitrary",), vmem_limit_bytes=VMEM_LIMIT_BYTES),
        name="mlp_ple",
    )(h2d, g_mlp, w_up, w_down, p3d, g_ple, w_pg, w_pe, g_final, *next_weights)
    return outs[0], outs[1:]


def _cast_kernel(src_ref, dst_ref):
    dst_ref[...] = src_ref[...].astype(BF16)


def _cast_leading_columns(w, width):
    block = pl.BlockSpec((None, w.shape[1] // CAST_STEPS, width), lambda i: (0, i, 0))
    return pl.pallas_call(
        _cast_kernel,
        grid=(CAST_STEPS,),
        in_specs=[block],
        out_specs=block,
        out_shape=jax.ShapeDtypeStruct((1, w.shape[1], width), BF16),
        compiler_params=pltpu.CompilerParams(dimension_semantics=("arbitrary",)),
        name="cast_qkv_weights",
    )(w)


def _cast_plan(first_layer_of, whole, steps):
    in_specs, out_specs, out_shape, operands = [], [], [], []
    for w in first_layer_of:
        block = (None, w.shape[1] // steps, w.shape[2])
        in_specs.append(pl.BlockSpec(block, lambda i: (0, i, 0)))
        out_specs.append(pl.BlockSpec(block, lambda i: (0, i, 0)))
        out_shape.append(jax.ShapeDtypeStruct((1,) + w.shape[1:], BF16))
        operands.append(w)
    for w in whole:
        flat = w.reshape(-1, w.shape[-1])
        block = (flat.shape[0] // steps, flat.shape[1])
        in_specs.append(pl.BlockSpec(block, lambda i: (i, 0)))
        out_specs.append(pl.BlockSpec(block, lambda i: (i, 0)))
        out_shape.append(jax.ShapeDtypeStruct(flat.shape, BF16))
        operands.append(flat)
    assert all(spec.block_shape[-2] % BF16_SUBLANES == 0 for spec in out_specs)
    n = len(first_layer_of)

    def restore(outs):
        return outs[:n], [o.reshape(w.shape) for o, w in zip(outs[n:], whole)]

    return operands, in_specs, out_specs, out_shape, restore


def _rotary_tables(positions):
    inv_freq = ROPE_THETA ** (-(jnp.arange(0, 2 * ROT_HALF, 2, dtype=F32) / (2 * ROT_HALF)))
    n_tok = positions.size
    pack = LANES // (2 * ROT_HALF)
    ang = (positions.astype(F32).reshape(n_tok // pack, pack, 1) * inv_freq).reshape(n_tok // pack, -1)
    cos8 = jnp.cos(ang).reshape(n_tok, ROT_HALF)
    sin8 = jnp.sin(ang).reshape(n_tok, ROT_HALF)
    in_head = jnp.arange(LANES) % HEAD_DIM
    rotated = in_head < 2 * ROT_HALF
    sign = jnp.where(in_head < ROT_HALF, -1.0, 1.0).astype(F32)
    spread = lambda t: jnp.tile(t, (1, LANES // ROT_HALF))
    return (jnp.where(rotated, spread(cos8), 1.0), jnp.where(rotated, sign * spread(sin8), 0.0))


def kernel(x, p, positions, norm_mix_g, w_in, conv_w, sg_ln_g, sg_ln_b, sg_w, sg_b,
           w_branch_a, w_branch_b, w_branch_c, w_out, norm_mlp_g, w_up, w_down,
           norm_ple_g, w_ple_gate, w_ple_proj, norm_final_g):
    batch, seq, d_model = x.shape
    depth = w_in.shape[0]
    n_tok = batch * seq
    qkv_width = 3 * len(DILATIONS) * GROUP_WIDTH
    assert seq % (max(DILATIONS) * ATTN_BLOCK) == 0 and seq % ATTN_ROWS_MAX == 0
    assert n_tok % MLP_ROWS == 0 and seq % MIX_ROWS == 0 and n_tok % QKV_ROWS == 0
    assert conv_w.shape[1] == 3, "the mixer kernel writes out the three conv taps"

    cos_t, sin_t = _rotary_tables(positions)
    rows = lambda v: v.reshape(v.shape[0], 1, v.shape[-1])
    sg_bias = jnp.broadcast_to(sg_b[:, :, :, None], sg_w.shape)
    big_f32 = (w_in, w_up, w_down)
    small_f32 = (w_branch_a, w_branch_b, w_branch_c, w_out, w_ple_gate, w_ple_proj)
    w_in_l = _cast_leading_columns(w_in, qkv_width)
    first_casts = _cast_plan(big_f32, small_f32, n_tok // QKV_ROWS)
    g_mix, g_mlp, g_ple = rows(norm_mix_g), rows(norm_mlp_g), rows(norm_ple_g)
    ln_g, ln_b = rows(sg_ln_g), rows(sg_ln_b)
    p3d = p.reshape(depth, n_tok, p.shape[-1])

    g_final = norm_final_g.reshape(1, 1, -1)

    h = x.reshape(n_tok, d_model)
    for i in range(depth):
        qkv, cast = _qkv_projection(i, h, g_mix, w_in_l, cos_t, sin_t,
                                    first_casts if i == 0 else None)
        if cast:
            (w_in_l, w_up_l, w_down_l), (w_a, w_b, w_c, w_o, w_pg, w_pe) = cast
        att = _attention(qkv, batch, seq)
        h = _mixer_merge(i, h, g_mix, att, w_in_l, conv_w, ln_g, ln_b, sg_w, sg_bias,
                         w_a, w_b, w_c, w_o, seq)
        h, nxt = _mlp_ple(i, h, g_mlp, w_up_l, w_down_l, p3d, g_ple, w_pg, w_pe, g_final,
                          big_f32 if i + 1 < depth else ())
        if nxt:
            w_in_l, w_up_l, w_down_l = nxt
    return h.reshape(batch, seq, d_model)
```

```python
import functools
import math

import jax
import jax.numpy as jnp
from jax import lax
from jax.experimental import pallas as pl
from jax.experimental.pallas import tpu as pltpu

F32 = jnp.float32
BF16 = jnp.bfloat16

HEAD_DIM = 64
HEADS_PER_GROUP = 4
GROUP_WIDTH = HEADS_PER_GROUP * HEAD_DIM
DILATIONS = (1, 4, 16)
ATTN_BLOCK = 128
ROT_HALF = 8
ROPE_THETA = 500000.0
SG_CHUNK = 128
SG_GROUPS = 4
RMS_EPS = 1e-6
LN_EPS = 1e-5
QUERY_SCALE = math.log2(math.e) * HEAD_DIM ** -0.5

LANES = 128
SUBLANES = 8
BF16_SUBLANES = 16
FREE_STRIDE = 4
VMEM_LIMIT_BYTES = 56 * 1024 * 1024

QKV_ROWS = 1024
QKV_SUB_ROWS = 256
ATTN_ROWS_MAX = 2048
MIX_ROWS = 512
MIX_SUB_ROWS = 256
MLP_ROWS = 512
MLP_SUB_ROWS = 512
MLP_FF_TILE = 1024
CAST_STEPS = 4


def _rms_norm(h, g):
    ms = jnp.mean(h * h, axis=-1, keepdims=True)
    return h * lax.rsqrt(ms + RMS_EPS) * g


def _sigmoid(x):
    return 1.0 / (1.0 + jnp.exp(-x))


def _gelu(x):
    return 0.5 * x * (1.0 + lax.erf(x * (1.0 / math.sqrt(2.0))))


def _resident(stacked, layer):
    shape = stacked.shape[1:]
    index = (layer,) + (0,) * len(shape)
    return pl.BlockSpec((None,) + shape, lambda *_: index, pipeline_mode=pl.Buffered(1))


def _qkv_kernel(h_ref, g_ref, w_ref, cos_ref, sin_ref, *refs, n_cast):
    n_groups = len(DILATIONS)
    outs = refs[n_cast:n_cast + n_groups]
    slab_ref, slab2_ref = refs[-2], refs[-1]
    for src_ref, dst_ref in zip(refs[:n_cast], refs[n_cast + n_groups:-2]):
        dst_ref[...] = src_ref[...].astype(BF16)
    sub = QKV_SUB_ROWS
    lane = lax.broadcasted_iota(jnp.int32, (sub, LANES), 1)
    low_half = (lane % HEAD_DIM) < ROT_HALF
    slab = slab2 = 0
    for s in range(h_ref.shape[0] // sub):
        rs = slice(s * sub, (s + 1) * sub)
        a = _rms_norm(h_ref[rs, :], g_ref[...]).astype(BF16)
        cos_k, sin_k = cos_ref[rs, :], sin_ref[rs, :]
        cos_sin = ((cos_k * QUERY_SCALE, sin_k * QUERY_SCALE), (cos_k, sin_k))
        for which in range(3):
            for g, d in enumerate(DILATIONS):
                c0 = (which * n_groups + g) * GROUP_WIDTH
                z = jnp.dot(a, w_ref[:, c0:c0 + GROUP_WIDTH], preferred_element_type=F32)
                out_rows = slice(s * sub // d, (s + 1) * sub // d)
                for half in range(GROUP_WIDTH // LANES):
                    t = z[:, half * LANES:(half + 1) * LANES]
                    if which < 2:
                        partner = jnp.where(low_half,
                                            pltpu.roll(t, LANES - ROT_HALF, 1),
                                            pltpu.roll(t, ROT_HALF, 1))
                        cos_t, sin_t = cos_sin[which]
                        t = t * cos_t + partner * sin_t
                    if d == 1:
                        outs[g][which, out_rows, half * LANES:(half + 1) * LANES] = t.astype(BF16)
                    else:
                        slab_ref[slab] = t
                        d1 = d if d <= FREE_STRIDE else FREE_STRIDE
                        d2 = d // d1
                        for r1 in range(d1):
                            part = slab_ref[slab, pl.ds(r1, sub // d1, stride=d1), :]
                            if d2 == 1:
                                l0 = r1 * GROUP_WIDTH + half * LANES
                                outs[g][which, out_rows, l0:l0 + LANES] = part.astype(BF16)
                                continue
                            slab2_ref[slab2] = part
                            for r2 in range(d2):
                                l0 = (r1 + d1 * r2) * GROUP_WIDTH + half * LANES
                                rows = slab2_ref[slab2, pl.ds(r2, sub // d, stride=d2), :]
                                outs[g][which, out_rows, l0:l0 + LANES] = rows.astype(BF16)
                            slab2 += 1
                        slab += 1


def _qkv_projection(layer, h2d, g, w_in, cos_t, sin_t, cast_plan=None):
    n_tok, d_model = h2d.shape
    tm = QKV_ROWS
    cast_operands, cast_in, cast_out, cast_shape, restore = cast_plan or ([], [], [], [], None)
    qkv_width = 3 * len(DILATIONS) * GROUP_WIDTH
    per_group = 3 * (GROUP_WIDTH // LANES) * (tm // QKV_SUB_ROWS)
    n_slabs = per_group * sum(d > 1 for d in DILATIONS)
    n_slabs2 = per_group * FREE_STRIDE * sum(d > FREE_STRIDE for d in DILATIONS)
    row_tile = pl.BlockSpec((tm, d_model), lambda i: (i, 0))
    out_specs = [pl.BlockSpec((3, tm // d, d * GROUP_WIDTH), lambda i: (0, i, 0)) for d in DILATIONS]
    out_shape = [jax.ShapeDtypeStruct((3, n_tok // d, d * GROUP_WIDTH), BF16) for d in DILATIONS]
    outs = pl.pallas_call(
        functools.partial(_qkv_kernel, n_cast=len(cast_operands)),
        grid=(n_tok // tm,),
        in_specs=[row_tile, _resident(g, layer)] + [
            pl.BlockSpec((None, d_model, qkv_width), lambda i: (0, 0, 0),
                         pipeline_mode=pl.Buffered(1)),
            pl.BlockSpec((tm, LANES), lambda i: (i, 0)),
            pl.BlockSpec((tm, LANES), lambda i: (i, 0)),
        ] + cast_in,
        out_specs=out_specs + cast_out,
        out_shape=out_shape + cast_shape,
        scratch_shapes=[pltpu.VMEM((n_slabs, QKV_SUB_ROWS, LANES), F32),
                        pltpu.VMEM((max(n_slabs2, 1), QKV_SUB_ROWS // FREE_STRIDE, LANES), F32)],
        compiler_params=pltpu.CompilerParams(
            dimension_semantics=("arbitrary",), vmem_limit_bytes=VMEM_LIMIT_BYTES),
        name="qkv_proj",
    )(h2d, g, w_in, cos_t, sin_t, *cast_operands)
    n_groups = len(DILATIONS)
    return outs[:n_groups], restore(outs[n_groups:]) if restore else None


def _attend(q_ref, k_ref, kp_ref, v_ref, vp_ref, ol_ref, first_step):
    def window(ref, prev_ref, r0, c0):
        if r0 > 0:
            return ref[r0 - ATTN_BLOCK:r0 + ATTN_BLOCK, c0:c0 + LANES]
        cur = ref[0:ATTN_BLOCK, c0:c0 + LANES]
        before = jnp.zeros_like(cur) if prev_ref is None else prev_ref[:, c0:c0 + LANES]
        return jnp.concatenate([before, cur], axis=0)

    two = 2 * ATTN_BLOCK
    qi = lax.broadcasted_iota(jnp.int32, (two, two), 0) % ATTN_BLOCK
    kj = lax.broadcasted_iota(jnp.int32, (two, two), 1)
    dist = qi + ATTN_BLOCK - kj
    band = (dist >= 0) & (dist <= ATTN_BLOCK)
    band_first = band & ((kj >= ATTN_BLOCK) | jnp.logical_not(first_step))
    mask_rest = jnp.where(band, 0.0, -jnp.inf)
    mask_first = jnp.where(band_first, 0.0, -jnp.inf)
    lane = lax.broadcasted_iota(jnp.int32, (ATTN_BLOCK, LANES), 1)
    head0 = lane < HEAD_DIM
    keep0 = jnp.where(head0, 1.0, 0.0).astype(BF16)
    keep1 = jnp.where(head0, 0.0, 1.0).astype(BF16)
    ones = jnp.ones((two, LANES), BF16)

    for i in range(q_ref.shape[0] // ATTN_BLOCK):
        mask = mask_first if i == 0 else mask_rest
        r0 = i * ATTN_BLOCK
        for p in range(q_ref.shape[1] // LANES):
            c0 = p * LANES
            q2 = q_ref[r0:r0 + ATTN_BLOCK, c0:c0 + LANES]
            qs = jnp.concatenate([q2 * keep0, q2 * keep1], axis=0)
            kk = window(k_ref, kp_ref, r0, c0)
            s = lax.dot_general(qs, kk, (((1,), (1,)), ((), ())),
                                preferred_element_type=F32)
            s = s + mask
            m = jnp.max(s, axis=-1, keepdims=True)
            e = jnp.exp2((s - m).astype(BF16))
            vv = jnp.concatenate([window(v_ref, vp_ref, r0, c0), ones], axis=1)
            pv = jnp.dot(e, vv, preferred_element_type=F32)
            pick = lambda x: jnp.where(head0, x[:ATTN_BLOCK], x[ATTN_BLOCK:])
            den = pick(pv[:, LANES:])
            ol_ref[0, r0:r0 + ATTN_BLOCK, c0:c0 + LANES] = pick(pv[:, :LANES]) * (1.0 / den)
            ol_ref[1, r0:r0 + ATTN_BLOCK, c0:c0 + LANES] = pick(m) + jnp.log2(den)


def _attn_kernel(*refs, plans):
    refs = iter(refs)
    inputs = []
    for has_prev, _ in plans:
        q_ref, k_ref = next(refs), next(refs)
        kp_ref = next(refs) if has_prev else None
        v_ref = next(refs)
        vp_ref = next(refs) if has_prev else None
        inputs.append((q_ref, k_ref, kp_ref, v_ref, vp_ref))
    for (has_prev, row_steps), group_inputs in zip(plans, inputs):
        first_step = (pl.program_id(1) % row_steps == 0) if has_prev else True
        _attend(*group_inputs, next(refs), first_step)


def _attention(qkv, batch, seq):
    in_specs, operands, out_specs, out_shapes, plans, steps = [], [], [], [], [], set()
    for qkv_g, dilation in zip(qkv, DILATIONS):
        m_len = seq // dilation
        mb = min(m_len, ATTN_ROWS_MAX)
        nblk = mb // ATTN_BLOCK
        n_res = min(dilation, ATTN_ROWS_MAX // mb)
        width = n_res * GROUP_WIDTH
        row_steps = m_len // mb
        has_prev = row_steps > 1
        steps.add(row_steps * (dilation // n_res))
        view = qkv_g.reshape(3, batch, m_len, dilation * GROUP_WIDTH)

        def rows(which, row_steps=row_steps, mb=mb, width=width):
            return pl.BlockSpec((None, None, mb, width),
                                lambda b, t: (which, b, t % row_steps, t // row_steps))

        def prev(which, row_steps=row_steps, nblk=nblk, width=width):
            return pl.BlockSpec(
                (None, None, ATTN_BLOCK, width),
                lambda b, t: (which, b, jnp.maximum((t % row_steps) * nblk - 1, 0), t // row_steps))

        if has_prev:
            specs = [rows(0), rows(1), prev(1), rows(2), prev(2)]
        else:
            specs = [rows(0), rows(1), rows(2)]
        in_specs += specs
        operands += [view] * len(specs)
        out_specs.append(pl.BlockSpec(
            (2, None, mb, width),
            lambda b, t, row_steps=row_steps: (0, b, t % row_steps, t // row_steps)))
        out_shapes.append(jax.ShapeDtypeStruct((2, batch, m_len, dilation * GROUP_WIDTH), F32))
        plans.append((has_prev, row_steps))
    (n_steps,) = steps
    outs = pl.pallas_call(
        functools.partial(_attn_kernel, plans=tuple(plans)),
        grid=(batch, n_steps),
        in_specs=in_specs,
        out_specs=out_specs,
        out_shape=out_shapes,
        compiler_params=pltpu.CompilerParams(
            dimension_semantics=("arbitrary", "arbitrary"), vmem_limit_bytes=VMEM_LIMIT_BYTES),
        name="dilated_attn",
    )(*operands)
    return [x.reshape(2, -1, x.shape[-1]) for x in outs]


def _mix_kernel(h_ref, g_ref, ol0_ref, ol1_ref, ol2_ref,
                wr_ref, cw_ref, lng_ref, lnb_ref, sgw_ref, sgb_ref,
                wa_ref, wb_ref, wc_ref, wo_ref, out_ref, carry_ref, yc_ref, nat_ref, stage_ref,
                *, tiles_per_seq, conv_w, sg_w):
    tm, d_model = h_ref.shape
    halves = GROUP_WIDTH // LANES
    w_skip = 3 * len(DILATIONS) * GROUP_WIDTH

    @pl.when(pl.program_id(0) % tiles_per_seq == 0)
    def _():
        carry_ref[...] = jnp.zeros_like(carry_ref)

    att_refs = tuple(ol_ref.at[k] for ol_ref in (ol0_ref, ol1_ref, ol2_ref) for k in range(2))
    stage = 0
    for slot, src_ref in enumerate(att_refs):
        d = DILATIONS[slot // 2]
        if d == 1:
            continue
        d1 = d if d <= FREE_STRIDE else FREE_STRIDE
        d2 = d // d1
        for half in range(halves):
            dst = slot * halves + half
            for r1 in range(d1):
                if d2 == 1:
                    l0 = r1 * GROUP_WIDTH + half * LANES
                    nat_ref[dst, pl.ds(r1, tm // d1, stride=d1), :] = src_ref[:, l0:l0 + LANES]
                    continue
                for r2 in range(d2):
                    l0 = (r1 + d1 * r2) * GROUP_WIDTH + half * LANES
                    stage_ref[stage, pl.ds(r2, tm // d, stride=d2), :] = src_ref[:, l0:l0 + LANES]
                nat_ref[dst, pl.ds(r1, tm // d1, stride=d1), :] = stage_ref[stage]
                stage += 1

    def natural(slot, rs):
        if DILATIONS[slot // 2] == 1:
            return att_refs[slot][rs, :]
        return jnp.concatenate([nat_ref[slot * halves + half, rs, :] for half in range(halves)], axis=1)

    c_b, c_c, c_u, c_v, c_g = conv_w, 2 * conv_w, 3 * conv_w, 3 * conv_w + sg_w, 3 * conv_w + 2 * sg_w
    ti = lax.broadcasted_iota(jnp.int32, (SG_CHUNK, SG_CHUNK), 0)
    si = lax.broadcasted_iota(jnp.int32, (SG_CHUNK, SG_CHUNK), 1)
    gch = sg_w // SG_GROUPS
    w_sg = [jnp.where(ti >= si, sgw_ref[g], 0.0).astype(BF16) for g in range(SG_GROUPS)]
    cw = cw_ref[...]
    carry = carry_ref[...]

    for s in range(tm // MIX_SUB_ROWS):
        r0 = s * MIX_SUB_ROWS
        rs = slice(r0, r0 + MIX_SUB_ROWS)
        h = h_ref[rs, :]
        a = _rms_norm(h, g_ref[...]).astype(BF16)

        def project(c0, c1):
            return jnp.dot(a, wr_ref[:, w_skip + c0:w_skip + c1], preferred_element_type=F32)

        z_conv = project(0, c_u)
        z_sg = project(c_u, c_g)

        zx, zb, zc = z_conv[:, :c_b], z_conv[:, c_b:c_c], z_conv[:, c_c:c_u]
        u = zc * zx
        ext = jnp.concatenate([carry, u], axis=0)
        u1 = pltpu.roll(ext, 1, 0)[SUBLANES:]
        u2 = pltpu.roll(ext, 2, 0)[SUBLANES:]
        carry = u[MIX_SUB_ROWS - SUBLANES:]
        yb = (zb * (cw[0:1] * u2 + cw[1:2] * u1 + cw[2:3] * u)).astype(BF16)

        z_g0 = project(c_g, c_g + d_model)

        l0, l1, l2 = natural(1, rs), natural(3, rs), natural(5, rs)
        mx = jnp.maximum(jnp.maximum(l0, l1), l2)
        e0, e1, e2 = jnp.exp2(l0 - mx), jnp.exp2(l1 - mx), jnp.exp2(l2 - mx)
        inv = 1.0 / (e0 + e1 + e2)
        ya = jnp.concatenate([natural(0, rs) * (e0 * inv), natural(2, rs) * (e1 * inv),
                              natural(4, rs) * (e2 * inv)], axis=1).astype(BF16)
        pa = jnp.dot(ya, wa_ref[...], preferred_element_type=F32)

        z_g1 = project(c_g + d_model, c_g + 2 * d_model)
        pb = jnp.dot(yb, wb_ref[...], preferred_element_type=F32)
        m = _sigmoid(z_g0) * pa + _sigmoid(z_g1) * pb

        for n in range(MIX_SUB_ROWS // SG_CHUNK):
            ns = slice(n * SG_CHUNK, (n + 1) * SG_CHUNK)
            us = _gelu(z_sg[ns, :sg_w])
            vs = _gelu(z_sg[ns, sg_w:])
            mu = jnp.mean(vs, axis=-1, keepdims=True)
            xc = vs - mu
            var = jnp.mean(xc * xc, axis=-1, keepdims=True)
            vln = (xc * lax.rsqrt(var + LN_EPS) * lng_ref[...] + lnb_ref[...]).astype(BF16)
            for g in range(SG_GROUPS):
                cs = slice(g * gch, (g + 1) * gch)
                sv = jnp.dot(w_sg[g], vln[:, cs], preferred_element_type=F32) + sgb_ref[g]
                yc_ref[r0 + n * SG_CHUNK:r0 + (n + 1) * SG_CHUNK, cs] = (us[:, cs] * sv).astype(BF16)

        z_g2 = project(c_g + 2 * d_model, c_g + 3 * d_model)
        pc = jnp.dot(yc_ref[rs, :], wc_ref[...], preferred_element_type=F32)
        m = m + _sigmoid(z_g2) * pc
        out_ref[rs, :] = h + jnp.dot(m.astype(BF16), wo_ref[...], preferred_element_type=F32)

    carry_ref[...] = carry


def _mixer_merge(layer, h2d, g, att, w_in, conv_w, ln_g, ln_b, sg_w, sg_b, w_a, w_b, w_c, w_o, seq):
    n_tok, d_model = h2d.shape
    tm = MIX_ROWS
    conv_width = conv_w.shape[-1]
    sg_width = ln_g.shape[-1]
    row_tile = lambda width: pl.BlockSpec((tm, width), lambda i: (i, 0))
    att_tiles = [pl.BlockSpec((2, tm // d, d * GROUP_WIDTH), lambda i: (0, i, 0)) for d in DILATIONS]
    n_stage = 2 * (GROUP_WIDTH // LANES) * FREE_STRIDE * sum(d > FREE_STRIDE for d in DILATIONS)
    kern = functools.partial(_mix_kernel, tiles_per_seq=seq // tm, conv_w=conv_width, sg_w=sg_width)
    return pl.pallas_call(
        kern,
        grid=(n_tok // tm,),
        in_specs=[row_tile(d_model), _resident(g, layer)]
        + att_tiles
        + [_resident(w_in, 0)]
        + [_resident(x, layer) for x in (conv_w, ln_g, ln_b, sg_w, sg_b, w_a, w_b, w_c, w_o)],
        out_specs=row_tile(d_model),
        out_shape=jax.ShapeDtypeStruct((n_tok, d_model), F32),
        scratch_shapes=[pltpu.VMEM((SUBLANES, conv_width), F32),
                        pltpu.VMEM((tm, sg_width), BF16),
                        pltpu.VMEM((2 * len(DILATIONS) * (GROUP_WIDTH // LANES), tm, LANES), F32),
                        pltpu.VMEM((max(n_stage, 1), tm // FREE_STRIDE, LANES), F32)],
        compiler_params=pltpu.CompilerParams(
            dimension_semantics=("arbitrary",), vmem_limit_bytes=VMEM_LIMIT_BYTES),
        name="mixer_merge",
    )(h2d, g, *att, w_in, conv_w, ln_g, ln_b, sg_w, sg_b, w_a, w_b, w_c, w_o)


def _mlp_kernel(h_ref, gm_ref, wu_ref, wd_ref, p_ref, gp_ref, wg_ref, wp_ref, gf_ref, *refs,
                final_norm):
    n_cast = len(refs) // 2
    out_ref = refs[n_cast]
    for src_ref, dst_ref in zip(refs[:n_cast], refs[n_cast + 1:]):
        dst_ref[...] = src_ref[...].astype(BF16)
    d_ff = wu_ref.shape[1]

    for s in range(h_ref.shape[0] // MLP_SUB_ROWS):
        rs = slice(s * MLP_SUB_ROWS, (s + 1) * MLP_SUB_ROWS)
        h = h_ref[rs, :]
        c = _rms_norm(h, gm_ref[...]).astype(BF16)
        acc = h
        for f0 in range(0, d_ff, MLP_FF_TILE):
            t = jnp.dot(c, wu_ref[:, f0:f0 + MLP_FF_TILE], preferred_element_type=F32)
            t = jnp.square(jnp.maximum(t, 0.0)).astype(BF16)
            acc = acc + jnp.dot(t, wd_ref[f0:f0 + MLP_FF_TILE, :], preferred_element_type=F32)
        e = _rms_norm(acc, gp_ref[...]).astype(BF16)
        gate = _sigmoid(jnp.dot(e, wg_ref[...], preferred_element_type=F32))
        emb = jnp.dot(p_ref[rs, :].astype(BF16), wp_ref[...], preferred_element_type=F32)
        out = acc + gate * emb
        out_ref[rs, :] = _rms_norm(out, gf_ref[...]) if final_norm else out


def _mlp_ple(layer, h2d, g_mlp, w_up, w_down, p3d, g_ple, w_pg, w_pe, g_final, next_weights):
    n_tok, d_model = h2d.shape
    tm = MLP_ROWS
    steps = n_tok // tm
    final_norm = not next_weights
    chunk = lambda w: (None, w.shape[1] // steps, w.shape[2])
    assert all(w.shape[1] % (steps * BF16_SUBLANES) == 0 for w in next_weights)
    row_tile = pl.BlockSpec((tm, d_model), lambda i: (i, 0))
    outs = pl.pallas_call(
        functools.partial(_mlp_kernel, final_norm=final_norm),
        grid=(steps,),
        in_specs=[
            row_tile,
            _resident(g_mlp, layer),
            _resident(w_up, 0),
            _resident(w_down, 0),
            pl.BlockSpec((None, tm, p3d.shape[-1]), lambda i: (layer, i, 0)),
            _resident(g_ple, layer),
            _resident(w_pg, layer),
            _resident(w_pe, layer),
            _resident(g_final, 0),
        ] + [pl.BlockSpec(chunk(w), lambda i: (layer + 1, i, 0)) for w in next_weights],
        out_specs=[row_tile] + [pl.BlockSpec(chunk(w), lambda i: (0, i, 0)) for w in next_weights],
        out_shape=[jax.ShapeDtypeStruct((n_tok, d_model), F32)]
        + [jax.ShapeDtypeStruct((1,) + w.shape[1:], BF16) for w in next_weights],
        compiler_params=pltpu.CompilerParams(
            dimension_semantics=("arbitrary",), vmem_limit_bytes=VMEM_LIMIT_BYTES),
        name="mlp_ple",
    )(h2d, g_mlp, w_up, w_down, p3d, g_ple, w_pg, w_pe, g_final, *next_weights)
    return outs[0], outs[1:]


def _cast_kernel(src_ref, dst_ref):
    dst_ref[...] = src_ref[...].astype(BF16)


def _cast_leading_columns(w, width):
    block = pl.BlockSpec((None, w.shape[1] // CAST_STEPS, width), lambda i: (0, i, 0))
    return pl.pallas_call(
        _cast_kernel,
        grid=(CAST_STEPS,),
        in_specs=[block],
        out_specs=block,
        out_shape=jax.ShapeDtypeStruct((1, w.shape[1], width), BF16),
        compiler_params=pltpu.CompilerParams(dimension_semantics=("arbitrary",)),
        name="cast_qkv_weights",
    )(w)


def _cast_plan(first_layer_of, whole, steps):
    in_specs, out_specs, out_shape, operands = [], [], [], []
    for w in first_layer_of:
        block = (None, w.shape[1] // steps, w.shape[2])
        in_specs.append(pl.BlockSpec(block, lambda i: (0, i, 0)))
        out_specs.append(pl.BlockSpec(block, lambda i: (0, i, 0)))
        out_shape.append(jax.ShapeDtypeStruct((1,) + w.shape[1:], BF16))
        operands.append(w)
    for w in whole:
        flat = w.reshape(-1, w.shape[-1])
        block = (flat.shape[0] // steps, flat.shape[1])
        in_specs.append(pl.BlockSpec(block, lambda i: (i, 0)))
        out_specs.append(pl.BlockSpec(block, lambda i: (i, 0)))
        out_shape.append(jax.ShapeDtypeStruct(flat.shape, BF16))
        operands.append(flat)
    assert all(spec.block_shape[-2] % BF16_SUBLANES == 0 for spec in out_specs)
    n = len(first_layer_of)

    def restore(outs):
        return outs[:n], [o.reshape(w.shape) for o, w in zip(outs[n:], whole)]

    return operands, in_specs, out_specs, out_shape, restore


def _rotary_tables(positions):
    inv_freq = ROPE_THETA ** (-(jnp.arange(0, 2 * ROT_HALF, 2, dtype=F32) / (2 * ROT_HALF)))
    n_tok = positions.size
    pack = LANES // (2 * ROT_HALF)
    ang = (positions.astype(F32).reshape(n_tok // pack, pack, 1) * inv_freq).reshape(n_tok // pack, -1)
    cos8 = jnp.cos(ang).reshape(n_tok, ROT_HALF)
    sin8 = jnp.sin(ang).reshape(n_tok, ROT_HALF)
    in_head = jnp.arange(LANES) % HEAD_DIM
    rotated = in_head < 2 * ROT_HALF
    sign = jnp.where(in_head < ROT_HALF, -1.0, 1.0).astype(F32)
    spread = lambda t: jnp.tile(t, (1, LANES // ROT_HALF))
    return (jnp.where(rotated, spread(cos8), 1.0), jnp.where(rotated, sign * spread(sin8), 0.0))


def kernel(x, p, positions, norm_mix_g, w_in, conv_w, sg_ln_g, sg_ln_b, sg_w, sg_b,
           w_branch_a, w_branch_b, w_branch_c, w_out, norm_mlp_g, w_up, w_down,
           norm_ple_g, w_ple_gate, w_ple_proj, norm_final_g):
    batch, seq, d_model = x.shape
    depth = w_in.shape[0]
    n_tok = batch * seq
    qkv_width = 3 * len(DILATIONS) * GROUP_WIDTH
    assert seq % (max(DILATIONS) * ATTN_BLOCK) == 0 and seq % ATTN_ROWS_MAX == 0
    assert n_tok % MLP_ROWS == 0 and seq % MIX_ROWS == 0 and n_tok % QKV_ROWS == 0
    assert conv_w.shape[1] == 3, "the mixer kernel writes out the three conv taps"

    cos_t, sin_t = _rotary_tables(positions)
    rows = lambda v: v.reshape(v.shape[0], 1, v.shape[-1])
    sg_bias = jnp.broadcast_to(sg_b[:, :, :, None], sg_w.shape)
    big_f32 = (w_in, w_up, w_down)
    small_f32 = (w_branch_a, w_branch_b, w_branch_c, w_out, w_ple_gate, w_ple_proj)
    w_in_l = _cast_leading_columns(w_in, qkv_width)
    first_casts = _cast_plan(big_f32, small_f32, n_tok // QKV_ROWS)
    g_mix, g_mlp, g_ple = rows(norm_mix_g), rows(norm_mlp_g), rows(norm_ple_g)
    ln_g, ln_b = rows(sg_ln_g), rows(sg_ln_b)
    p3d = p.reshape(depth, n_tok, p.shape[-1])

    g_final = norm_final_g.reshape(1, 1, -1)

    h = x.reshape(n_tok, d_model)
    for i in range(depth):
        qkv, cast = _qkv_projection(i, h, g_mix, w_in_l, cos_t, sin_t,
                                    first_casts if i == 0 else None)
        if cast:
            (w_in_l, w_up_l, w_down_l), (w_a, w_b, w_c, w_o, w_pg, w_pe) = cast
        att = _attention(qkv, batch, seq)
        h = _mixer_merge(i, h, g_mix, att, w_in_l, conv_w, ln_g, ln_b, sg_w, sg_bias,
                         w_a, w_b, w_c, w_o, seq)
        h, nxt = _mlp_ple(i, h, g_mlp, w_up_l, w_down_l, p3d, g_ple, w_pg, w_pe, g_final,
                          big_f32 if i + 1 < depth else ())
        if nxt:
            w_in_l, w_up_l, w_down_l = nxt
    return h.reshape(batch, seq, d_model)
```
